```python
import jax, jax.numpy as jnp
from jax import lax
import numpy as np

D_MODEL = 2048
BATCH = 4
SEQ = 2048
DEPTH = 4
DEC_BATCH = 128
DEC_SEQ = 8
PAST_LEN = 16384
PAGE_SIZE = 128

RWKV_HEAD = 64
RWKV_WIDTH = D_MODEL // 2
RWKV_HEADS = RWKV_WIDTH // RWKV_HEAD
R_DECAY = 64
R_AAA = 64
R_GATE = 160
RWKV_PROJ = 3 * RWKV_WIDTH + R_DECAY + R_AAA + R_GATE
RWKV_SPLITS = (RWKV_WIDTH, 2 * RWKV_WIDTH, 3 * RWKV_WIDTH, 3 * RWKV_WIDTH + R_DECAY, 3 * RWKV_WIDTH + R_DECAY + R_AAA)
GN_EPS = 64e-5
POOL_WIDTH = D_MODEL // 4
POOL_WINDOWS = (2, 4, 8, 16)
POOL_GROUP = POOL_WIDTH // len(POOL_WINDOWS)
POOL_BUF = max(POOL_WINDOWS) - 1
S5_WIDTH = D_MODEL // 4
S5_GROUP_CH = 16
S5_GROUPS = S5_WIDTH // S5_GROUP_CH
S5_STATE = 64
MIX_WIDTH = RWKV_WIDTH + POOL_WIDTH + S5_WIDTH
N_BRANCH = 3
IN_WIDTH = RWKV_PROJ + POOL_WIDTH + S5_WIDTH + N_BRANCH * D_MODEL
IN_SPLITS = (RWKV_PROJ, RWKV_PROJ + POOL_WIDTH, RWKV_PROJ + POOL_WIDTH + S5_WIDTH)
N_EXPERTS = 32
TOP_K = 4
D_FF = D_MODEL
SWIGLU_LIMIT = 7.0
SWIGLU_ALPHA = 1.702
EXPERT_BLOCK = 128
NORM_EPS = 1e-5

kernel_name = 'gated_rwkv7_pool_s5_moe_decoder_step'


def _rmsnorm(x, g):
    xf = x.astype(jnp.float32)
    y = xf * lax.rsqrt(jnp.mean(xf * xf, axis=-1, keepdims=True) + NORM_EPS)
    return (y * g.astype(jnp.float32)).astype(x.dtype)


def _rwkv7(z, shift_prev, wkv0, mu, w0, w2, a0, a2, g2, k_k, k_a, r_k, lnx_g, lnx_b):
    f32 = jnp.float32
    bsz, seq, _ = z.shape
    z_prev = jnp.concatenate([shift_prev[:, None, :].astype(z.dtype), z[:, :-1]], axis=1)
    zs = z + (z_prev - z) * mu
    r, k, v, wd, ad, gd = jnp.split(zs, RWKV_SPLITS, axis=-1)
    w_log = -jax.nn.softplus(-(w0 + jnp.tanh(wd) @ w2).astype(f32)) - 0.5
    decay = jnp.exp(-jnp.exp(w_log))
    a = jax.nn.sigmoid((a0 + ad @ a2).astype(f32))
    g = (jax.nn.sigmoid(gd) @ g2).astype(f32)

    def heads(t):
        return t.astype(f32).reshape(bsz, seq, RWKV_HEADS, RWKV_HEAD)

    kk = heads(k * k_k)
    kk = kk / jnp.maximum(jnp.sqrt(jnp.sum(kk * kk, axis=-1, keepdims=True)), 1e-12)
    a_h = heads(a)
    k_h = heads(k) * (1.0 + (a_h - 1.0) * k_a.astype(f32).reshape(RWKV_HEADS, RWKV_HEAD))
    r_h, v_h, w_h = heads(r), heads(v), heads(decay)
    b_h = kk * a_h

    def step(S, inp):
        r_t, w_t, k_t, v_t, kk_t, b_t = inp
        sa = jnp.einsum('bhij,bhj->bhi', S, kk_t)
        S = S * w_t[:, :, None, :] - sa[..., None] * b_t[:, :, None, :] + v_t[..., None] * k_t[:, :, None, :]
        return S, jnp.einsum('bhij,bhj->bhi', S, r_t)

    xs = tuple(jnp.swapaxes(t, 0, 1) for t in (r_h, w_h, k_h, v_h, kk, b_h))
    s_fin, y = lax.scan(step, wkv0.astype(f32), xs)
    y = jnp.swapaxes(y, 0, 1)
    mean = jnp.mean(y, axis=-1, keepdims=True)
    var = jnp.mean(jnp.square(y - mean), axis=-1, keepdims=True)
    y = ((y - mean) * lax.rsqrt(var + GN_EPS)).reshape(bsz, seq, RWKV_WIDTH)
    y = y * lnx_g.astype(f32) + lnx_b.astype(f32)
    bonus = jnp.sum(r_h * k_h * r_k.astype(f32), axis=-1, keepdims=True) * v_h
    y = (y + bonus.reshape(bsz, seq, RWKV_WIDTH)) * g
    return y.astype(z.dtype), z[:, -1], s_fin


def _pool_mixer(z, buf, start_pos, pool_w, pool_scale):
    f32 = jnp.float32
    bsz, seq, _ = z.shape
    full = jnp.concatenate([buf.astype(z.dtype), z], axis=1)
    cs = jnp.cumsum(full.astype(f32), axis=1)
    cs = jnp.concatenate([jnp.zeros((bsz, 1, POOL_WIDTH), f32), cs], axis=1)
    pos = start_pos + jnp.arange(seq)
    means = []
    for gi, win in enumerate(POOL_WINDOWS):
        sl = slice(gi * POOL_GROUP, (gi + 1) * POOL_GROUP)
        hi = cs[:, POOL_BUF + 1:POOL_BUF + 1 + seq, sl]
        lo = cs[:, POOL_BUF + 1 - win:POOL_BUF + 1 - win + seq, sl]
        cnt = jnp.minimum(win, pos + 1).astype(f32)[None, :, None]
        means.append((hi - lo) / cnt)
    mean = jnp.stack(means, axis=2)
    d = mean - z.astype(f32).reshape(bsz, seq, len(POOL_WINDOWS), POOL_GROUP)
    y = jnp.einsum('blgc,gcd->blgd', d.astype(z.dtype), pool_w).reshape(bsz, seq, POOL_WIDTH)
    return y * pool_scale, full[:, -POOL_BUF:]


def _s5(u, re0, im0, a_re, a_im, log_dt, b_re, b_im, c_re, c_im, d, glu_w, glu_b):
    f32 = jnp.float32
    bsz, seq, _ = u.shape
    uf = u.astype(f32).reshape(bsz, seq, S5_GROUPS, S5_GROUP_CH)
    a_re, a_im = a_re.astype(f32), a_im.astype(f32)
    dt = jnp.exp(log_dt.astype(f32))[:, None]
    mag = jnp.exp(a_re * dt)
    lb_re, lb_im = mag * jnp.cos(a_im * dt), mag * jnp.sin(a_im * dt)
    nr, ni = lb_re - 1.0, lb_im
    den = a_re * a_re + a_im * a_im
    f_re = (nr * a_re + ni * a_im) / den
    f_im = (ni * a_re - nr * a_im) / den
    b_re, b_im = b_re.astype(f32), b_im.astype(f32)
    bb_re = f_re[..., None] * b_re - f_im[..., None] * b_im
    bb_im = f_re[..., None] * b_im + f_im[..., None] * b_re
    bu_re = jnp.einsum('blgc,gpc->blgp', uf, bb_re)
    bu_im = jnp.einsum('blgc,gpc->blgp', uf, bb_im)
    ar = jnp.broadcast_to(lb_re, bu_re.shape)
    ai = jnp.broadcast_to(lb_im, bu_re.shape)

    def combine(e1, e2):
        ar1, ai1, br1, bi1 = e1
        ar2, ai2, br2, bi2 = e2
        return (ar1 * ar2 - ai1 * ai2, ar1 * ai2 + ai1 * ar2,
                ar2 * br1 - ai2 * bi1 + br2, ar2 * bi1 + ai2 * br1 + bi2)

    pr, pi, sr, si = lax.associative_scan(combine, (ar, ai, bu_re, bu_im), axis=1)
    r0, i0 = re0.astype(f32)[:, None], im0.astype(f32)[:, None]
    x_re = pr * r0 - pi * i0 + sr
    x_im = pr * i0 + pi * r0 + si
    y = jnp.einsum('blgp,gcp->blgc', x_re, c_re.astype(f32)) - jnp.einsum('blgp,gcp->blgc', x_im, c_im.astype(f32))
    y = y.reshape(bsz, seq, S5_WIDTH) + d.astype(f32) * u.astype(f32)
    y = jax.nn.gelu(y)
    y = y * jax.nn.sigmoid(y @ glu_w.astype(f32) + glu_b.astype(f32))
    return y.astype(u.dtype), x_re[:, -1], x_im[:, -1]


def _moe(h, router_w, router_b, w_gu, b_gu, w_down, b_down):
    bsz, seq, dm = h.shape
    t = h.reshape(-1, dm)
    n_tok = t.shape[0]
    n_rows = n_tok * TOP_K
    logits = t.astype(jnp.float32) @ router_w.astype(jnp.float32) + router_b.astype(jnp.float32)
    top_v, top_e = lax.top_k(logits, TOP_K)
    gate = jax.nn.softmax(top_v, axis=-1)
    flat_e = top_e.reshape(-1)
    flat_tok = jnp.arange(n_rows, dtype=jnp.int32) // TOP_K
    order = jnp.argsort(flat_e)
    se, stok, sgate = flat_e[order], flat_tok[order], gate.reshape(-1)[order]
    counts = jnp.bincount(flat_e, length=N_EXPERTS)
    padded = (counts + EXPERT_BLOCK - 1) // EXPERT_BLOCK * EXPERT_BLOCK
    pad_end = jnp.cumsum(padded)
    pad_start = pad_end - padded
    grp_start = jnp.cumsum(counts) - counts
    dest = pad_start[se] + (jnp.arange(n_rows, dtype=jnp.int32) - grp_start[se])
    n_blocks = -(-n_rows // EXPERT_BLOCK) + N_EXPERTS
    rows_tok = jnp.zeros((n_blocks * EXPERT_BLOCK,), jnp.int32).at[dest].set(stok)
    starts = jnp.arange(n_blocks, dtype=jnp.int32) * EXPERT_BLOCK
    blk_e = jnp.minimum(jnp.searchsorted(pad_end, starts, side='right'), N_EXPERTS - 1).astype(jnp.int32)
    xb = t[rows_tok].reshape(n_blocks, EXPERT_BLOCK, dm)

    def expert_block(args):
        xi, e = args
        gu = xi @ w_gu[e] + b_gu[e]
        g_, u_ = gu[:, :D_FF], gu[:, D_FF:]
        g_ = jnp.minimum(g_, SWIGLU_LIMIT)
        u_ = jnp.clip(u_, -SWIGLU_LIMIT, SWIGLU_LIMIT)
        act = (u_ + 1.0) * (g_ * jax.nn.sigmoid(SWIGLU_ALPHA * g_))
        return act @ w_down[e] + b_down[e]

    yb = lax.map(expert_block, (xb, blk_e)).reshape(-1, dm)
    ys = yb[dest] * sgate[:, None].astype(yb.dtype)
    out = jnp.zeros_like(t).at[stok].add(ys)
    return out.reshape(bsz, seq, dm)


def _layer(x, c, shift_st, wkv_st, pool_st, s5re_st, s5im_st, start_pos,
           norm1_g, norm2_g, w_ada, b_ada, w_in,
           rw_mu, rw_w0, rw_w2, rw_a0, rw_a2, rw_g2, rw_kk, rw_ka, rw_rk, rw_lnx_g, rw_lnx_b,
           pool_w, pool_scale,
           s5_a_re, s5_a_im, s5_log_dt, s5_b_re, s5_b_im, s5_c_re, s5_c_im, s5_d, s5_glu_w, s5_glu_b,
           w_br, w_out, router_w, router_b, moe_w_gu, moe_b_gu, moe_w_down, moe_b_down):
    mod = (jax.nn.silu(c) @ w_ada + b_ada)[:, None, :]
    shift1, scale1, gate1, shift2, scale2, gate2 = jnp.split(mod, 6, axis=-1)
    h = _rmsnorm(x, norm1_g) * (1 + scale1) + shift1
    z = h @ w_in
    z_a, z_b, z_c, z_g = jnp.split(z, IN_SPLITS, axis=-1)
    y_a, new_shift, new_wkv = _rwkv7(z_a, shift_st, wkv_st, rw_mu, rw_w0, rw_w2, rw_a0, rw_a2, rw_g2,
                                     rw_kk, rw_ka, rw_rk, rw_lnx_g, rw_lnx_b)
    y_b, new_pool = _pool_mixer(z_b, pool_st, start_pos, pool_w, pool_scale)
    y_c, new_re, new_im = _s5(z_c, s5re_st, s5im_st, s5_a_re, s5_a_im, s5_log_dt, s5_b_re, s5_b_im,
                              s5_c_re, s5_c_im, s5_d, s5_glu_w, s5_glu_b)
    gates = jax.nn.sigmoid(z_g.astype(jnp.float32)).astype(x.dtype)
    g_a, g_b, g_c = jnp.split(gates, N_BRANCH, axis=-1)
    wb_a, wb_b, wb_c = jnp.split(w_br, (RWKV_WIDTH, RWKV_WIDTH + POOL_WIDTH), axis=0)
    merged = g_a * (y_a @ wb_a) + g_b * (y_b @ wb_b) + g_c * (y_c @ wb_c)
    x = x + gate1 * (merged @ w_out)
    h2 = _rmsnorm(x, norm2_g) * (1 + scale2) + shift2
    x = x + gate2 * _moe(h2, router_w, router_b, moe_w_gu, moe_b_gu, moe_w_down, moe_b_down)
    return x, new_shift, new_wkv, new_pool, new_re, new_im


def setup_inputs(seed: int = 0) -> dict:
    key = jax.random.key(seed)
    keys = iter(jax.random.split(key, 80))
    f32 = jnp.float32

    def nrm(shape, scale=1.0):
        return scale * jax.random.normal(next(keys), shape, f32)

    def unif(shape, lo, hi):
        return jax.random.uniform(next(keys), shape, f32, lo, hi)

    L = DEPTH
    x_prompt = nrm((BATCH, SEQ, D_MODEL))
    x_sample = nrm((DEC_BATCH, DEC_SEQ, D_MODEL))
    c_prompt = nrm((BATCH, D_MODEL))
    c_sample = nrm((DEC_BATCH, D_MODEL))
    state_wkv = nrm((L, DEC_BATCH, RWKV_HEADS, RWKV_HEAD, RWKV_HEAD))
    state_shift = nrm((L, DEC_BATCH, RWKV_PROJ))
    state_pool = nrm((L, DEC_BATCH, POOL_BUF, POOL_WIDTH))
    state_s5_re = nrm((L, DEC_BATCH, S5_GROUPS, S5_STATE), 0.1)
    state_s5_im = nrm((L, DEC_BATCH, S5_GROUPS, S5_STATE), 0.1)
    norm1_g = 1.0 + nrm((L, D_MODEL), 0.02)
    norm2_g = 1.0 + nrm((L, D_MODEL), 0.02)
    final_norm_g = 1.0 + nrm((D_MODEL,), 0.02)
    w_ada = nrm((L, D_MODEL, 6 * D_MODEL), 0.5 * D_MODEL ** -0.5)
    b_ada = nrm((L, 6 * D_MODEL), 0.01)
    w_in = nrm((L, D_MODEL, IN_WIDTH), D_MODEL ** -0.5)
    rw_mu = unif((L, RWKV_PROJ), 0.0, 1.0)
    rw_w0 = unif((L, RWKV_WIDTH), -5.0, -1.0)
    rw_w2 = nrm((L, R_DECAY, RWKV_WIDTH), 0.5 * R_DECAY ** -0.5)
    rw_a0 = nrm((L, RWKV_WIDTH), 0.1)
    rw_a2 = nrm((L, R_AAA, RWKV_WIDTH), 0.5 * R_AAA ** -0.5)
    rw_g2 = nrm((L, R_GATE, RWKV_WIDTH), R_GATE ** -0.5)
    rw_kk = 0.85 + nrm((L, RWKV_WIDTH), 0.02)
    rw_ka = 1.0 + nrm((L, RWKV_WIDTH), 0.02)
    rw_rk = nrm((L, RWKV_HEADS, RWKV_HEAD), 0.1)
    rw_lnx_g = 1.0 + nrm((L, RWKV_WIDTH), 0.02)
    rw_lnx_b = nrm((L, RWKV_WIDTH), 0.01)
    pool_w = nrm((L, len(POOL_WINDOWS), POOL_GROUP, POOL_GROUP), POOL_GROUP ** -0.5)
    pool_scale = 1.0 + nrm((L, POOL_WIDTH), 0.02)
    s5_a_re = -0.5 + nrm((L, S5_GROUPS, S5_STATE), 0.01)
    s5_a_im = jnp.pi * jnp.arange(S5_STATE, dtype=f32) + nrm((L, S5_GROUPS, S5_STATE), 0.01)
    s5_log_dt = unif((L, S5_GROUPS), float(np.log(1e-3)), float(np.log(1e-1)))
    s5_b_re = nrm((L, S5_GROUPS, S5_STATE, S5_GROUP_CH), (2 * S5_GROUP_CH) ** -0.5)
    s5_b_im = nrm((L, S5_GROUPS, S5_STATE, S5_GROUP_CH), (2 * S5_GROUP_CH) ** -0.5)
    s5_c_re = nrm((L, S5_GROUPS, S5_GROUP_CH, S5_STATE), S5_STATE ** -0.5)
    s5_c_im = nrm((L, S5_GROUPS, S5_GROUP_CH, S5_STATE), S5_STATE ** -0.5)
    s5_d = nrm((L, S5_WIDTH), 0.5)
    s5_glu_w = nrm((L, S5_WIDTH, S5_WIDTH), S5_WIDTH ** -0.5)
    s5_glu_b = nrm((L, S5_WIDTH), 0.01)
    w_br = nrm((L, MIX_WIDTH, D_MODEL), POOL_WIDTH ** -0.5)
    w_out = nrm((L, D_MODEL, D_MODEL), D_MODEL ** -0.5)
    router_w = nrm((L, D_MODEL, N_EXPERTS), D_MODEL ** -0.5)
    router_b = nrm((L, N_EXPERTS), 0.01)
    moe_w_gu = jnp.stack([nrm((N_EXPERTS, D_MODEL, 2 * D_FF), D_MODEL ** -0.5) for _ in range(L)])
    moe_b_gu = nrm((L, N_EXPERTS, 2 * D_FF), 0.01)
    moe_w_down = jnp.stack([nrm((N_EXPERTS, D_FF, D_MODEL), D_FF ** -0.5) for _ in range(L)])
    moe_b_down = nrm((L, N_EXPERTS, D_MODEL), 0.01)
    return {
        'x_prompt': x_prompt, 'x_sample': x_sample, 'c_prompt': c_prompt, 'c_sample': c_sample,
        'state_wkv': state_wkv, 'state_shift': state_shift, 'state_pool': state_pool,
        'state_s5_re': state_s5_re, 'state_s5_im': state_s5_im,
        'norm1_g': norm1_g, 'norm2_g': norm2_g, 'final_norm_g': final_norm_g,
        'w_ada': w_ada, 'b_ada': b_ada, 'w_in': w_in,
        'rw_mu': rw_mu, 'rw_w0': rw_w0, 'rw_w2': rw_w2, 'rw_a0': rw_a0, 'rw_a2': rw_a2, 'rw_g2': rw_g2,
        'rw_kk': rw_kk, 'rw_ka': rw_ka, 'rw_rk': rw_rk, 'rw_lnx_g': rw_lnx_g, 'rw_lnx_b': rw_lnx_b,
        'pool_w': pool_w, 'pool_scale': pool_scale,
        's5_a_re': s5_a_re, 's5_a_im': s5_a_im, 's5_log_dt': s5_log_dt, 's5_b_re': s5_b_re,
        's5_b_im': s5_b_im, 's5_c_re': s5_c_re, 's5_c_im': s5_c_im, 's5_d': s5_d,
        's5_glu_w': s5_glu_w, 's5_glu_b': s5_glu_b,
        'w_br': w_br, 'w_out': w_out, 'router_w': router_w, 'router_b': router_b,
        'moe_w_gu': moe_w_gu, 'moe_b_gu': moe_b_gu, 'moe_w_down': moe_w_down, 'moe_b_down': moe_b_down,
    }


def reference(x_prompt, x_sample, c_prompt, c_sample, state_wkv, state_shift, state_pool,
              state_s5_re, state_s5_im, norm1_g, norm2_g, final_norm_g, w_ada, b_ada, w_in,
              rw_mu, rw_w0, rw_w2, rw_a0, rw_a2, rw_g2, rw_kk, rw_ka, rw_rk, rw_lnx_g, rw_lnx_b,
              pool_w, pool_scale, s5_a_re, s5_a_im, s5_log_dt, s5_b_re, s5_b_im, s5_c_re, s5_c_im,
              s5_d, s5_glu_w, s5_glu_b, w_br, w_out, router_w, router_b,
              moe_w_gu, moe_b_gu, moe_w_down, moe_b_down):
    layer_weights = (norm1_g, norm2_g, w_ada, b_ada, w_in,
                     rw_mu, rw_w0, rw_w2, rw_a0, rw_a2, rw_g2, rw_kk, rw_ka, rw_rk, rw_lnx_g, rw_lnx_b,
                     pool_w, pool_scale,
                     s5_a_re, s5_a_im, s5_log_dt, s5_b_re, s5_b_im, s5_c_re, s5_c_im, s5_d, s5_glu_w, s5_glu_b,
                     w_br, w_out, router_w, router_b, moe_w_gu, moe_b_gu, moe_w_down, moe_b_down)
    f32 = jnp.float32
    xp, xs = x_prompt, x_sample
    bp = x_prompt.shape[0]
    outs_p, outs_s = [], []
    for i in range(DEPTH):
        lw = tuple(w[i] for w in layer_weights)
        zero_shift = jnp.zeros((bp, RWKV_PROJ), xp.dtype)
        zero_wkv = jnp.zeros((bp, RWKV_HEADS, RWKV_HEAD, RWKV_HEAD), f32)
        zero_pool = jnp.zeros((bp, POOL_BUF, POOL_WIDTH), xp.dtype)
        zero_s5 = jnp.zeros((bp, S5_GROUPS, S5_STATE), f32)
        xp, *sp = _layer(xp, c_prompt, zero_shift, zero_wkv, zero_pool, zero_s5, zero_s5, 0, *lw)
        xs, *ss = _layer(xs, c_sample, state_shift[i], state_wkv[i], state_pool[i],
                         state_s5_re[i], state_s5_im[i], PAST_LEN, *lw)
        outs_p.append(sp)
        outs_s.append(ss)
    p_shift, p_wkv, p_pool, p_s5_re, p_s5_im = (jnp.stack([o[j] for o in outs_p]) for j in range(5))
    s_shift, s_wkv, s_pool, s_s5_re, s_s5_im = (jnp.stack([o[j] for o in outs_s]) for j in range(5))
    y_prompt = _rmsnorm(xp, final_norm_g)
    y_sample = _rmsnorm(xs, final_norm_g)
    return (y_prompt, y_sample, p_wkv, p_shift, p_pool, p_s5_re, p_s5_im,
            s_wkv, s_shift, s_pool, s_s5_re, s_s5_im)
```

```python
import functools

import jax
import jax.numpy as jnp
from jax import lax
from jax.experimental import pallas as pl
from jax.experimental.pallas import tpu as pltpu

F32, BF16 = jnp.float32, jnp.bfloat16

LANES = 128
SUBLANES = 8
VMEM_LIMIT = 56 * 1024 * 1024

HEAD = 64
HEADS = 16
RW = HEAD * HEADS
R_DECAY, R_AAA, R_GATE = 64, 64, 160
RW_PROJ = 3 * RW + R_DECAY + R_AAA + R_GATE
RW_PAD = 3584
LORA_IN = RW_PAD - 3 * RW
POOL_W = 512
WINDOWS = (2, 4, 8, 16)
POOL_G = POOL_W // len(WINDOWS)
POOL_BUF = max(WINDOWS) - 1
S5_W = 512
S5_G, S5_CH, S5_P = 32, 16, 64
S5_N = S5_G * S5_P
MAIN_W = RW_PAD + POOL_W + S5_W
N_EXP, TOP_K = 32, 4
SW_LIMIT, SW_ALPHA = 7.0, 1.702
NORM_EPS, GN_EPS = 1e-5, 64e-5
MOD_ROWS = 128


def _params(sem):
    return pltpu.CompilerParams(dimension_semantics=sem, vmem_limit_bytes=VMEM_LIMIT)


def _dot(a, b):
    return jnp.dot(a, b, preferred_element_type=F32)


def _segsum(x, ones_bd):
    hi = x.astype(BF16)
    lo = (x - hi.astype(F32)).astype(BF16)
    return _dot(hi, ones_bd) + _dot(lo, ones_bd)


def _tile_rows(t, rows):
    return jnp.broadcast_to(t[None], (rows // MOD_ROWS,) + t.shape).reshape(rows, t.shape[-1])


def _ada_kernel(c_ref, w_ref, b_ref, o_ref):
    c = c_ref[...]
    a = (c * jax.nn.sigmoid(c)).astype(BF16)
    o_ref[0] = _dot(a, w_ref[0].astype(BF16)) + b_ref[0]


def _adaln(c_all, w_ada, b_ada):
    nl, d, n = w_ada.shape
    r = c_all.shape[0]
    tn = 1024
    return pl.pallas_call(
        _ada_kernel,
        grid=(nl, n // tn),
        in_specs=[pl.BlockSpec((r, d), lambda l, j: (0, 0)),
                  pl.BlockSpec((1, d, tn), lambda l, j: (l, 0, j)),
                  pl.BlockSpec((1, 1, tn), lambda l, j: (l, 0, j))],
        out_specs=pl.BlockSpec((1, r, tn), lambda l, j: (l, 0, j)),
        out_shape=jax.ShapeDtypeStruct((nl, r, n), F32),
        compiler_params=_params(("parallel", "parallel")),
        name="adaln",
    )(c_all, w_ada, b_ada.reshape(nl, 1, n))


def _norm_mod(x, g, scale, shift):
    y = x * lax.rsqrt(jnp.mean(x * x, axis=-1, keepdims=True) + NORM_EPS) * g
    rows = x.shape[0]
    return y * (1.0 + _tile_rows(scale, rows)) + _tile_rows(shift, rows)


def _inproj_kernel(x_ref, g_ref, sc_ref, sh_ref, w_ref, o_ref, h_scr):
    @pl.when(pl.program_id(1) == 0)
    def _():
        h_scr[...] = _norm_mod(x_ref[...], g_ref[...], sc_ref[...], sh_ref[...]).astype(BF16)

    o_ref[...] = _dot(h_scr[...], w_ref[...])


def _inproj(x, g, scale_t, shift_t, w, mod_idx, tm):
    m, d = x.shape
    n = w.shape[1]
    tn = 512
    return pl.pallas_call(
        _inproj_kernel,
        grid=(m // tm, n // tn),
        in_specs=[pl.BlockSpec((tm, d), lambda i, j: (i, 0)),
                  pl.BlockSpec((1, d), lambda i, j: (0, 0)),
                  pl.BlockSpec((MOD_ROWS, d), lambda i, j: (mod_idx(i), 0)),
                  pl.BlockSpec((MOD_ROWS, d), lambda i, j: (mod_idx(i), 0)),
                  pl.BlockSpec((d, tn), lambda i, j: (0, j))],
        out_specs=pl.BlockSpec((tm, tn), lambda i, j: (i, j)),
        out_shape=jax.ShapeDtypeStruct((m, n), F32),
        scratch_shapes=[pltpu.VMEM((tm, d), BF16)],
        compiler_params=_params(("parallel", "arbitrary")),
        name="inproj",
    )(x, g, scale_t, shift_t, w)


def _softplus(x):
    return jnp.maximum(x, 0.0) + jnp.log1p(jnp.exp(-jnp.abs(x)))


def _rwkv_pre_kernel(roll_mode, z_ref, zp_ref, mu_ref, w0_ref, a0_ref, kkw_ref, ka_ref, rk_ref,
                     wl_ref, ones_ref, r_o, w_o, k_o, v_o, kk_o, b_o, g_o, bon_o):
    z = z_ref[...]
    if roll_mode:
        row = lax.broadcasted_iota(jnp.int32, z.shape, 0)
        zp = jnp.where(row == 0, zp_ref[0], pltpu.roll(z, 1, axis=0))
    else:
        zp = zp_ref[...]
    zs = z + (zp - z) * mu_ref[...]
    r, k, v = zs[:, :RW], zs[:, RW:2 * RW], zs[:, 2 * RW:3 * RW]
    l0 = zs[:, 3 * RW:3 * RW + LANES]
    lane = lax.broadcasted_iota(jnp.int32, l0.shape, 1)
    l0 = jnp.where(lane < R_DECAY, jnp.tanh(l0), l0)
    l1 = jax.nn.sigmoid(zs[:, 3 * RW + LANES:])
    lin = jnp.concatenate([l0, l1], axis=1).astype(BF16)
    lo = _dot(lin, wl_ref[...])
    w_log = -_softplus(-(w0_ref[...] + lo[:, :RW])) - 0.5
    a = jax.nn.sigmoid(a0_ref[...] + lo[:, RW:2 * RW])
    ones_bd = ones_ref[...]
    kk = k * kkw_ref[...]
    kkn = kk / jnp.maximum(jnp.sqrt(_segsum(kk * kk, ones_bd)), 1e-12)
    kh = k * (1.0 + (a - 1.0) * ka_ref[...])
    r_o[...] = r
    w_o[...] = jnp.exp(-jnp.exp(w_log))
    k_o[...] = kh
    v_o[...] = v
    kk_o[...] = kkn
    b_o[...] = kkn * a
    g_o[...] = lo[:, 2 * RW:]
    bon_o[...] = _segsum(r * kh * rk_ref[...], ones_bd) * v


def _rwkv_pre(z_main, zp, row0, rows, tt, roll_mode, mu, w0, a0, kkw, ka, rk, wl, ones_bd):
    blk0 = row0 // tt
    vec = lambda w: pl.BlockSpec((1, w), lambda i: (0, 0))
    if roll_mode:
        zp_spec = pl.BlockSpec((1, 1, RW_PAD), lambda i: (i, 0, 0))
    else:
        zp_spec = pl.BlockSpec((tt, RW_PAD), lambda i: (i, 0))
    out_spec = pl.BlockSpec((tt, RW), lambda i: (i, 0))
    return pl.pallas_call(
        functools.partial(_rwkv_pre_kernel, roll_mode),
        grid=(rows // tt,),
        in_specs=[pl.BlockSpec((tt, RW_PAD), lambda i: (blk0 + i, 0)), zp_spec,
                  vec(RW_PAD), vec(RW), vec(RW), vec(RW), vec(RW), vec(RW),
                  pl.BlockSpec((LORA_IN, 3 * RW), lambda i: (0, 0)),
                  pl.BlockSpec((RW, RW), lambda i: (0, 0))],
        out_specs=[out_spec] * 8,
        out_shape=[jax.ShapeDtypeStruct((rows, RW), F32)] * 8,
        compiler_params=_params(("parallel",)),
        name="rwkv_pre",
    )(z_main, zp, mu, w0, a0, kkw, ka, rk, wl, ones_bd)


def _rwkv_scan_kernel(r_ref, w_ref, k_ref, kk_ref, b_ref, v_ref, s0_ref, y_ref, sT_ref, s_scr):
    tc = pl.program_id(1)
    n_t = r_ref.shape[0]
    n_i = s_scr.shape[0]

    @pl.when(tc == 0)
    def _():
        s_scr[...] = s0_ref[...]

    def step(t, carry):
        r_t, w_t, k_t, kk_t, b_t = r_ref[t], w_ref[t], k_ref[t], kk_ref[t], b_ref[t]

        def rows8(ib, c2):
            i0 = pl.multiple_of(ib * SUBLANES, SUBLANES)
            v8 = v_ref[t, pl.ds(i0, SUBLANES), :]
            ys = []
            for ii in range(SUBLANES):
                s_old = s_scr[i0 + ii]
                sa = jnp.sum(s_old * kk_t, axis=0, keepdims=True)
                s_new = s_old * w_t - sa * b_t + v8[ii:ii + 1] * k_t
                ys.append(jnp.sum(s_new * r_t, axis=0, keepdims=True))
                s_scr[i0 + ii] = s_new
            y_ref[t, pl.ds(i0, SUBLANES), :] = jnp.concatenate(ys, axis=0)
            return c2

        return lax.fori_loop(0, n_i // SUBLANES, rows8, carry)

    lax.fori_loop(0, n_t, step, 0)

    @pl.when(tc == pl.num_programs(1) - 1)
    def _():
        sT_ref[...] = s_scr[...]


def _rwkv_scan(r, w, k, kk, b, v, s0, tchunk):
    n_t, _, n_l = r.shape
    n_i = s0.shape[0]
    op = pl.BlockSpec((tchunk, HEAD, LANES), lambda l, t: (t, 0, l))
    vy = pl.BlockSpec((tchunk, n_i, LANES), lambda l, t: (t, 0, l))
    st = pl.BlockSpec((n_i, HEAD, LANES), lambda l, t: (0, 0, l))
    return pl.pallas_call(
        _rwkv_scan_kernel,
        grid=(n_l // LANES, n_t // tchunk),
        in_specs=[op, op, op, op, op, vy, st],
        out_specs=[vy, st],
        out_shape=[jax.ShapeDtypeStruct((n_t, n_i, n_l), F32),
                   jax.ShapeDtypeStruct((n_i, HEAD, n_l), F32)],
        scratch_shapes=[pltpu.VMEM((n_i, HEAD, LANES), F32)],
        compiler_params=_params(("parallel", "arbitrary")),
        name="rwkv_scan",
    )(r, w, k, kk, b, v, s0)


def _rwkv_post_kernel(y_ref, bon_ref, g_ref, lg_ref, lb_ref, ones_ref, o_ref):
    y = y_ref[...]
    ones_bd = ones_ref[...]
    yc = y - _segsum(y, ones_bd) * (1.0 / HEAD)
    var = _segsum(yc * yc, ones_bd) * (1.0 / HEAD)
    yn = yc * lax.rsqrt(var + GN_EPS) * lg_ref[...] + lb_ref[...]
    o_ref[...] = ((yn + bon_ref[...]) * g_ref[...]).astype(BF16)


def _rwkv_post(y, bonus, g, lnx_g, lnx_b, ones_bd, tt):
    rows = y.shape[0]
    blk = pl.BlockSpec((tt, RW), lambda i: (i, 0))
    vec = pl.BlockSpec((1, RW), lambda i: (0, 0))
    return pl.pallas_call(
        _rwkv_post_kernel,
        grid=(rows // tt,),
        in_specs=[blk, blk, blk, vec, vec, pl.BlockSpec((RW, RW), lambda i: (0, 0))],
        out_specs=blk,
        out_shape=jax.ShapeDtypeStruct((rows, RW), BF16),
        compiler_params=_params(("parallel",)),
        name="rwkv_post",
    )(y, bonus, g, lnx_g, lnx_b, ones_bd)


def _pool_prompt_kernel(start_pos, z_ref, pw_ref, ps_ref, y_ref):
    n = z_ref.shape[0]
    row = lax.broadcasted_iota(jnp.int32, (n, POOL_G), 0)
    for gi, win in enumerate(WINDOWS):
        sl = slice(gi * POOL_G, (gi + 1) * POOL_G)
        x = z_ref[:, sl]
        s = x
        k = 1
        while k < win:
            s = s + jnp.where(row >= k, pltpu.roll(s, k, axis=0), 0.0)
            k *= 2
        cnt = jnp.minimum(win, row + (start_pos + 1)).astype(F32)
        d = s / cnt - x
        y_ref[:, sl] = (_dot(d.astype(BF16), pw_ref[gi]) * ps_ref[:, sl]).astype(BF16)


def _pool_prompt(z_main, n_b, n_t, pw, ps):
    cb = RW_PAD // POOL_W
    return pl.pallas_call(
        functools.partial(_pool_prompt_kernel, 0),
        grid=(n_b,),
        in_specs=[pl.BlockSpec((n_t, POOL_W), lambda b: (b, cb)),
                  pl.BlockSpec((len(WINDOWS), POOL_G, POOL_G), lambda b: (0, 0, 0)),
                  pl.BlockSpec((1, POOL_W), lambda b: (0, 0))],
        out_specs=pl.BlockSpec((n_t, POOL_W), lambda b: (b, 0)),
        out_shape=jax.ShapeDtypeStruct((n_b * n_t, POOL_W), BF16),
        compiler_params=_params(("parallel",)),
        name="pool_prompt",
    )(z_main, pw, ps)


def _pool_sample_kernel(start_pos, n_t, full_ref, pw_ref, ps_ref, y_ref):
    n_b = full_ref.shape[1]
    for gi, win in enumerate(WINDOWS):
        sl = slice(gi * POOL_G, (gi + 1) * POOL_G)
        f = full_ref[:, :, sl]
        s = f
        k = 1
        while k < win:
            s = s[k:] + s[:-k]
            k *= 2
        s = s[s.shape[0] - n_t:]
        x = f[POOL_BUF:]
        pos = start_pos + lax.broadcasted_iota(jnp.int32, s.shape, 0)
        cnt = jnp.minimum(win, pos + 1).astype(F32)
        d = (s / cnt - x).reshape(n_t * n_b, POOL_G)
        y_ref[:, sl] = (_dot(d.astype(BF16), pw_ref[gi]) * ps_ref[:, sl]).astype(BF16)


def _pool_sample(full, n_t, start_pos, pw, ps):
    n_f, n_b, _ = full.shape
    return pl.pallas_call(
        functools.partial(_pool_sample_kernel, start_pos, n_t),
        grid=(1,),
        in_specs=[pl.BlockSpec((n_f, n_b, POOL_W), lambda i: (0, 0, 0)),
                  pl.BlockSpec((len(WINDOWS), POOL_G, POOL_G), lambda i: (0, 0, 0)),
                  pl.BlockSpec((1, POOL_W), lambda i: (0, 0))],
        out_specs=pl.BlockSpec((n_t * n_b, POOL_W), lambda i: (0, 0)),
        out_shape=jax.ShapeDtypeStruct((n_t * n_b, POOL_W), BF16),
        compiler_params=_params(("arbitrary",)),
        name="pool_sample",
    )(full, pw, ps)


def _s5_tail(x_all, u, cmat_ref, d_ref, gw_ref, gb_ref):
    y = _dot(x_all.astype(BF16), cmat_ref[...]) + d_ref[...] * u
    y = jax.nn.gelu(y)
    return (y * jax.nn.sigmoid(_dot(y.astype(BF16), gw_ref[...]) + gb_ref[...])).astype(BF16)


S5_LC = 512


def _s5_prompt_kernel(u_ref, bmat_ref, cmat_ref, lam_ref, pw_ref, d_ref, gw_ref, gb_ref,
                      y_ref, xre_o, xim_o, x_scr, c_scr):
    tc = pl.program_id(1)

    @pl.when(tc == 0)
    def _():
        c_scr[...] = jnp.zeros_like(c_scr)

    u = u_ref[...]
    n = u.shape[0]
    x_scr[...] = _dot(u.astype(BF16), bmat_ref[...])
    row = lax.broadcasted_iota(jnp.int32, (SUBLANES, S5_LC), 0)
    for lc in range(S5_N // S5_LC):
        re_sl = pl.ds(lc * S5_LC, S5_LC)
        im_sl = pl.ds(S5_N + lc * S5_LC, S5_LC)
        lam = [(lam_ref[2 * j:2 * j + 1, re_sl], lam_ref[2 * j + 1:2 * j + 2, re_sl]) for j in range(3)]
        p_re, p_im = pw_ref[:, re_sl], pw_ref[:, im_sl]

        def blk(rb, carry, re_sl=re_sl, im_sl=im_sl, lam=lam, p_re=p_re, p_im=p_im):
            c_re, c_im = carry
            rows = pl.ds(pl.multiple_of(rb * SUBLANES, SUBLANES), SUBLANES)
            xr, xi = x_scr[rows, re_sl], x_scr[rows, im_sl]
            for j, (l_re, l_im) in enumerate(lam):
                kshift = 1 << j
                sr = jnp.where(row >= kshift, pltpu.roll(xr, kshift, axis=0), 0.0)
                si = jnp.where(row >= kshift, pltpu.roll(xi, kshift, axis=0), 0.0)
                xr, xi = xr + (l_re * sr - l_im * si), xi + (l_re * si + l_im * sr)
            xr, xi = xr + (p_re * c_re - p_im * c_im), xi + (p_re * c_im + p_im * c_re)
            x_scr[rows, re_sl] = xr
            x_scr[rows, im_sl] = xi
            return xr[SUBLANES - 1:], xi[SUBLANES - 1:]

        c_re, c_im = lax.fori_loop(0, n // SUBLANES, blk, (c_scr[0:1, re_sl], c_scr[0:1, im_sl]))
        c_scr[0:1, re_sl] = c_re
        c_scr[0:1, im_sl] = c_im

    y_ref[...] = _s5_tail(x_scr[...], u, cmat_ref, d_ref, gw_ref, gb_ref)

    @pl.when(tc == pl.num_programs(1) - 1)
    def _():
        xre_o[0] = c_scr[0:1, :S5_N]
        xim_o[0] = c_scr[0:1, S5_N:]


def _s5_prompt(z_main, n_b, n_t, tt, bmat, cmat, lam_t, pw_t, dskip, gw, gb):
    cb = (RW_PAD + POOL_W) // S5_W
    n_tc = n_t // tt
    const = lambda shape: pl.BlockSpec(shape, lambda b, t: (0,) * len(shape))
    return pl.pallas_call(
        _s5_prompt_kernel,
        grid=(n_b, n_tc),
        in_specs=[pl.BlockSpec((tt, S5_W), lambda b, t: (b * n_tc + t, cb)),
                  const((S5_W, 2 * S5_N)), const((2 * S5_N, S5_W)),
                  const((SUBLANES, S5_N)), const((SUBLANES, 2 * S5_N)),
                  const((1, S5_W)), const((S5_W, S5_W)), const((1, S5_W))],
        out_specs=[pl.BlockSpec((tt, S5_W), lambda b, t: (b * n_tc + t, 0)),
                   pl.BlockSpec((1, 1, S5_N), lambda b, t: (b, 0, 0)),
                   pl.BlockSpec((1, 1, S5_N), lambda b, t: (b, 0, 0))],
        out_shape=[jax.ShapeDtypeStruct((n_b * n_t, S5_W), BF16),
                   jax.ShapeDtypeStruct((n_b, 1, S5_N), F32),
                   jax.ShapeDtypeStruct((n_b, 1, S5_N), F32)],
        scratch_shapes=[pltpu.VMEM((tt, 2 * S5_N), F32), pltpu.VMEM((SUBLANES, 2 * S5_N), F32)],
        compiler_params=_params(("parallel", "arbitrary")),
        name="s5_prompt",
    )(z_main, bmat, cmat, lam_t, pw_t, dskip, gw, gb)


def _s5_sample_kernel(n_t, u_ref, x0re_ref, x0im_ref, bmat_ref, cmat_ref, lam_ref, d_ref, gw_ref,
                      gb_ref, y_ref, xre_o, xim_o, x_scr):
    u = u_ref[...]
    n_b = u.shape[0] // n_t
    x_scr[...] = _dot(u.astype(BF16), bmat_ref[...])
    for lc in range(S5_N // S5_LC):
        re_sl = pl.ds(lc * S5_LC, S5_LC)
        im_sl = pl.ds(S5_N + lc * S5_LC, S5_LC)
        l_re, l_im = lam_ref[0:1, re_sl], lam_ref[1:2, re_sl]
        xr, xi = x0re_ref[:, re_sl], x0im_ref[:, re_sl]
        for t in range(n_t):
            rows = pl.ds(t * n_b, n_b)
            xr, xi = (l_re * xr - l_im * xi + x_scr[rows, re_sl],
                      l_re * xi + l_im * xr + x_scr[rows, im_sl])
            x_scr[rows, re_sl] = xr
            x_scr[rows, im_sl] = xi
        xre_o[:, re_sl] = xr
        xim_o[:, re_sl] = xi
    y_ref[...] = _s5_tail(x_scr[...], u, cmat_ref, d_ref, gw_ref, gb_ref)


def _s5_sample(z_main, row0, n_b, n_t, x0re, x0im, bmat, cmat, lam_t, dskip, gw, gb):
    rows = n_b * n_t
    cb = (RW_PAD + POOL_W) // S5_W
    const = lambda shape: pl.BlockSpec(shape, lambda i: (0,) * len(shape))
    return pl.pallas_call(
        functools.partial(_s5_sample_kernel, n_t),
        grid=(1,),
        in_specs=[pl.BlockSpec((rows, S5_W), lambda i: (row0 // rows, cb)),
                  const((n_b, S5_N)), const((n_b, S5_N)),
                  const((S5_W, 2 * S5_N)), const((2 * S5_N, S5_W)), const((SUBLANES, S5_N)),
                  const((1, S5_W)), const((S5_W, S5_W)), const((1, S5_W))],
        out_specs=[const((rows, S5_W)), const((n_b, S5_N)), const((n_b, S5_N))],
        out_shape=[jax.ShapeDtypeStruct((rows, S5_W), BF16),
                   jax.ShapeDtypeStruct((n_b, S5_N), F32),
                   jax.ShapeDtypeStruct((n_b, S5_N), F32)],
        scratch_shapes=[pltpu.VMEM((rows, 2 * S5_N), F32)],
        compiler_params=_params(("arbitrary",)),
        name="s5_sample",
    )(z_main, x0re, x0im, bmat, cmat, lam_t, dskip, gw, gb)


def _merge_kernel(ya_ref, yb_ref, yc_ref, zg_ref, wbr_ref, wout_ref, x_ref, g1_ref, n2_ref,
                  sc2_ref, sh2_ref, rw_ref, rb_ref, xo_ref, h2_ref, lg_ref):
    d = x_ref.shape[1]
    rows = x_ref.shape[0]
    m = jax.nn.sigmoid(zg_ref[:, :d]) * _dot(ya_ref[...], wbr_ref[:RW, :])
    m = m + jax.nn.sigmoid(zg_ref[:, d:2 * d]) * _dot(yb_ref[...], wbr_ref[RW:RW + POOL_W, :])
    m = m + jax.nn.sigmoid(zg_ref[:, 2 * d:]) * _dot(yc_ref[...], wbr_ref[RW + POOL_W:, :])
    xn = x_ref[...] + _tile_rows(g1_ref[...], rows) * _dot(m.astype(BF16), wout_ref[...])
    xo_ref[...] = xn
    h2 = _norm_mod(xn, n2_ref[...], sc2_ref[...], sh2_ref[...])
    h2_ref[...] = h2.astype(BF16)
    lg_ref[...] = jnp.dot(h2, rw_ref[...], precision=lax.Precision.HIGHEST,
                          preferred_element_type=F32) + rb_ref[...]


def _merge(ya, yb, yc, zg, wbr, wout, x, g1_t, n2g, sc2_t, sh2_t, rw, rb, mod_idx, tm):
    m, d = x.shape
    row = lambda w: pl.BlockSpec((tm, w), lambda i: (i, 0))
    const = lambda shape: pl.BlockSpec(shape, lambda i: (0,) * len(shape), pipeline_mode=pl.Buffered(1))
    mod = pl.BlockSpec((MOD_ROWS, d), lambda i: (mod_idx(i), 0))
    return pl.pallas_call(
        _merge_kernel,
        grid=(m // tm,),
        in_specs=[row(RW), row(POOL_W), row(S5_W), row(3 * d), const(wbr.shape), const(wout.shape),
                  row(d), mod, const((1, d)), mod, mod, const(rw.shape), const((1, LANES))],
        out_specs=[row(d), row(d), row(LANES)],
        out_shape=[jax.ShapeDtypeStruct((m, d), F32), jax.ShapeDtypeStruct((m, d), BF16),
                   jax.ShapeDtypeStruct((m, LANES), F32)],
        compiler_params=_params(("parallel",)),
        name="merge",
    )(ya, yb, yc, zg, wbr, wout, x, g1_t, n2g, sc2_t, sh2_t, rw, rb)


MOE_BM = 512
MOE_TF = 512


def _moe_kernel(be_ref, nb_ref, x_ref, wg_ref, wu_ref, bg_ref, bu_ref, wd_ref, bd_ref, o_ref):
    s, f = pl.program_id(0), pl.program_id(1)
    used = s < nb_ref[0]

    @pl.when(jnp.logical_and(jnp.logical_not(used), f == 0))
    def _():
        o_ref[...] = jnp.zeros_like(o_ref)

    @pl.when(used)
    def _():
        x = x_ref[...]
        g = _dot(x, wg_ref[0].astype(BF16)) + bg_ref[0]
        u = _dot(x, wu_ref[0].astype(BF16)) + bu_ref[0]
        g = jnp.minimum(g, SW_LIMIT)
        u = jnp.clip(u, -SW_LIMIT, SW_LIMIT)
        act = (u + 1.0) * (g * jax.nn.sigmoid(SW_ALPHA * g))
        part = _dot(act.astype(BF16), wd_ref[0].astype(BF16))

        @pl.when(f == 0)
        def _():
            o_ref[...] = part + bd_ref[0]

        @pl.when(f > 0)
        def _():
            o_ref[...] += part


def _moe(xs, blk_e, n_used, w_gu, b_gu, w_down, b_down):
    rows, d = xs.shape
    n_e, _, two_ff = w_gu.shape
    dff = two_ff // 2
    nf = dff // MOE_TF
    n_blk = rows // MOE_BM

    def f_eff(s, f, nb):
        return jnp.where(s < nb[0], f, nf - 1)

    grid_spec = pltpu.PrefetchScalarGridSpec(
        num_scalar_prefetch=2,
        grid=(n_blk, nf),
        in_specs=[
            pl.BlockSpec((MOE_BM, d), lambda s, f, be, nb: (jnp.minimum(s, nb[0] - 1), 0)),
            pl.BlockSpec((1, d, MOE_TF), lambda s, f, be, nb: (be[s], 0, f_eff(s, f, nb))),
            pl.BlockSpec((1, d, MOE_TF), lambda s, f, be, nb: (be[s], 0, nf + f_eff(s, f, nb))),
            pl.BlockSpec((1, 1, MOE_TF), lambda s, f, be, nb: (be[s], 0, f_eff(s, f, nb))),
            pl.BlockSpec((1, 1, MOE_TF), lambda s, f, be, nb: (be[s], 0, nf + f_eff(s, f, nb))),
            pl.BlockSpec((1, MOE_TF, d), lambda s, f, be, nb: (be[s], f_eff(s, f, nb), 0)),
            pl.BlockSpec((1, 1, d), lambda s, f, be, nb: (be[s], 0, 0)),
        ],
        out_specs=pl.BlockSpec((MOE_BM, d), lambda s, f, be, nb: (s, 0)),
    )
    return pl.pallas_call(
        _moe_kernel,
        grid_spec=grid_spec,
        out_shape=jax.ShapeDtypeStruct((rows, d), F32),
        compiler_params=_params(("arbitrary", "arbitrary")),
        name="moe",
    )(blk_e, n_used, xs, w_gu, w_gu, b_gu.reshape(n_e, 1, two_ff), b_gu.reshape(n_e, 1, two_ff),
      w_down, b_down.reshape(n_e, 1, d))


def _route(logits, n_rows_pad):
    n_tok = logits.shape[0]
    top_v, top_e = lax.top_k(logits, TOP_K)
    gate = jax.nn.softmax(top_v, axis=-1)
    flat_e = top_e.reshape(-1)
    onehot = (flat_e[:, None] == jnp.arange(N_EXP, dtype=jnp.int32)[None, :]).astype(jnp.int32)
    csum = jnp.cumsum(onehot, axis=0)
    counts = csum[-1]
    rank = jnp.sum((csum - 1) * onehot, axis=1)
    padded = (counts + MOE_BM - 1) // MOE_BM * MOE_BM
    pad_end = jnp.cumsum(padded)
    pad_start = pad_end - padded
    dest = (pad_start[flat_e] + rank).astype(jnp.int32)
    flat_tok = jnp.arange(n_tok * TOP_K, dtype=jnp.int32) // TOP_K
    rows_tok = jnp.zeros((n_rows_pad,), jnp.int32).at[dest].set(flat_tok)
    n_used = (pad_end[-1] // MOE_BM).astype(jnp.int32)
    starts = jnp.arange(n_rows_pad // MOE_BM, dtype=jnp.int32) * MOE_BM
    starts = jnp.minimum(starts, pad_end[-1] - MOE_BM)
    blk_e = jnp.minimum(jnp.searchsorted(pad_end, starts, side='right'), N_EXP - 1).astype(jnp.int32)
    return gate, dest.reshape(n_tok, TOP_K), rows_tok, blk_e, n_used.reshape(1)


def _combine_kernel(x_ref, yg_ref, gw_ref, g2_ref, o_ref):
    d = x_ref.shape[1]
    rows = x_ref.shape[0]
    gw = gw_ref[...]
    acc = gw[:, 0:1] * yg_ref[:, :d]
    for k in range(1, TOP_K):
        acc = acc + gw[:, k:k + 1] * yg_ref[:, k * d:(k + 1) * d]
    o_ref[...] = x_ref[...] + _tile_rows(g2_ref[...], rows) * acc


def _combine(x, yg, gw, g2_t, mod_idx, tm):
    m, d = x.shape
    return pl.pallas_call(
        _combine_kernel,
        grid=(m // tm,),
        in_specs=[pl.BlockSpec((tm, d), lambda i: (i, 0)),
                  pl.BlockSpec((tm, TOP_K * d), lambda i: (i, 0)),
                  pl.BlockSpec((tm, TOP_K), lambda i: (i, 0)),
                  pl.BlockSpec((MOD_ROWS, d), lambda i: (mod_idx(i), 0))],
        out_specs=pl.BlockSpec((tm, d), lambda i: (i, 0)),
        out_shape=jax.ShapeDtypeStruct((m, d), F32),
        compiler_params=_params(("parallel",)),
        name="combine",
    )(x, yg, gw, g2_t)


def _final_norm_kernel(x_ref, g_ref, o_ref):
    x = x_ref[...]
    o_ref[...] = x * lax.rsqrt(jnp.mean(x * x, axis=-1, keepdims=True) + NORM_EPS) * g_ref[...]


def _final_norm(x, g, tm):
    m, d = x.shape
    return pl.pallas_call(
        _final_norm_kernel,
        grid=(m // tm,),
        in_specs=[pl.BlockSpec((tm, d), lambda i: (i, 0)), pl.BlockSpec((1, d), lambda i: (0, 0))],
        out_specs=pl.BlockSpec((tm, d), lambda i: (i, 0)),
        out_shape=jax.ShapeDtypeStruct((m, d), F32),
        compiler_params=_params(("parallel",)),
        name="final_norm",
    )(x, g)


def _s5_params(a_re, a_im, log_dt, b_re, b_im, c_re, c_im):
    dt = jnp.exp(log_dt)[:, None]
    mag = jnp.exp(a_re * dt)
    lb_re, lb_im = mag * jnp.cos(a_im * dt), mag * jnp.sin(a_im * dt)
    nr, ni = lb_re - 1.0, lb_im
    den = a_re * a_re + a_im * a_im
    f_re = (nr * a_re + ni * a_im) / den
    f_im = (ni * a_re - nr * a_im) / den
    bb_re = f_re[..., None] * b_re - f_im[..., None] * b_im
    bb_im = f_re[..., None] * b_im + f_im[..., None] * b_re
    eye = jnp.eye(S5_G, dtype=F32)
    bd_in = lambda w: jnp.einsum('gpc,gh->gchp', w, eye).reshape(S5_W, S5_N)
    bd_out = lambda w: jnp.einsum('gcp,gh->gphc', w, eye).reshape(S5_N, S5_W)
    bmat = jnp.concatenate([bd_in(bb_re), bd_in(bb_im)], axis=1).astype(BF16)
    cmat = jnp.concatenate([bd_out(c_re), -bd_out(c_im)], axis=0).astype(BF16)
    l_re, l_im = lb_re.reshape(1, S5_N), lb_im.reshape(1, S5_N)
    pows = [(l_re, l_im)]
    for _ in range(SUBLANES - 1):
        p_re, p_im = pows[-1]
        pows.append((p_re * l_re - p_im * l_im, p_re * l_im + p_im * l_re))
    lam_t = jnp.concatenate([pows[0][0], pows[0][1], pows[1][0], pows[1][1], pows[3][0], pows[3][1],
                             jnp.zeros((2, S5_N), F32)], axis=0)
    pw_t = jnp.concatenate([jnp.concatenate([p[0] for p in pows], axis=0),
                            jnp.concatenate([p[1] for p in pows], axis=0)], axis=1)
    return bmat, cmat, lam_t, pw_t


def _pad_cols(w, width):
    return jnp.pad(w, ((0, 0), (0, width - w.shape[1])))


def kernel(x_prompt, x_sample, c_prompt, c_sample, state_wkv, state_shift, state_pool, state_s5_re, state_s5_im, norm1_g, norm2_g, final_norm_g, w_ada, b_ada, w_in, rw_mu, rw_w0, rw_w2, rw_a0, rw_a2, rw_g2, rw_kk, rw_ka, rw_rk, rw_lnx_g, rw_lnx_b, pool_w, pool_scale, s5_a_re, s5_a_im, s5_log_dt, s5_b_re, s5_b_im, s5_c_re, s5_c_im, s5_d, s5_glu_w, s5_glu_b, w_br, w_out, router_w, router_b, moe_w_gu, moe_b_gu, moe_w_down, moe_b_down):
    bp, lp, d = x_prompt.shape
    bs, ls, _ = x_sample.shape
    depth = w_in.shape[0]
    mp, ms = bp * lp, bs * ls
    m = mp + ms
    past_len = 16384
    assert bs == MOD_ROWS and bp * HEADS * 2 == LANES and lp % 1024 == 0 and ms == 1024

    def mod_idx_for(tm):
        n_p = mp // tm
        return lambda i: jnp.where(i < n_p, (i * tm) // lp, bp)

    tm_big, tm_mid = 1024, 256
    ones_bd = jnp.kron(jnp.eye(HEADS, dtype=F32), jnp.ones((HEAD, HEAD), F32)).astype(BF16)

    x = jnp.concatenate([x_prompt.reshape(mp, d), jnp.swapaxes(x_sample, 0, 1).reshape(ms, d)], axis=0)
    c_all = jnp.concatenate([c_prompt, c_sample, jnp.zeros((4, d), F32)], axis=0)
    mod = _adaln(c_all, w_ada, b_ada)

    def table(layer, idx):
        v = mod[layer, :, idx * d:(idx + 1) * d]
        return jnp.concatenate([jnp.repeat(v[:bp], MOD_ROWS, axis=0), v[bp:bp + bs]], axis=0)

    outs_p, outs_s = [], []
    for l in range(depth):
        shift1, scale1, gate1, shift2, scale2, gate2 = (table(l, i) for i in range(6))

        w_l = w_in[l]
        w_main = jnp.concatenate([_pad_cols(w_l[:, :RW_PROJ], RW_PAD),
                                  w_l[:, RW_PROJ:RW_PROJ + POOL_W + S5_W]], axis=1).astype(BF16)
        w_gate = w_l[:, RW_PROJ + POOL_W + S5_W:].astype(BF16)
        n1 = norm1_g[l].reshape(1, d)
        z_main = _inproj(x, n1, scale1, shift1, w_main, mod_idx_for(tm_big), tm_big)
        z_gate = _inproj(x, n1, scale1, shift1, w_gate, mod_idx_for(tm_big), tm_big)

        mu = _pad_cols(rw_mu[l].reshape(1, RW_PROJ), RW_PAD)
        wl = jnp.zeros((LORA_IN, 3 * RW), F32)
        wl = wl.at[:R_DECAY, :RW].set(rw_w2[l])
        wl = wl.at[R_DECAY:R_DECAY + R_AAA, RW:2 * RW].set(rw_a2[l])
        wl = wl.at[R_DECAY + R_AAA:R_DECAY + R_AAA + R_GATE, 2 * RW:].set(rw_g2[l]).astype(BF16)
        vecs = [v.reshape(1, RW) for v in (rw_w0[l], rw_a0[l], rw_kk[l], rw_ka[l], rw_rk[l])]
        tt_p = 256
        n_tp = mp // tt_p
        last_rows = z_main[tt_p - 1:mp:tt_p, :RW_PAD]
        prev_p = jnp.concatenate([jnp.zeros((1, RW_PAD), F32), last_rows[:-1]], axis=0)
        first_of_seq = (jnp.arange(n_tp) % (lp // tt_p) == 0)[:, None]
        prev_p = jnp.where(first_of_seq, 0.0, prev_p).reshape(n_tp, 1, RW_PAD)
        pre_p = _rwkv_pre(z_main, prev_p, 0, mp, tt_p, True, mu, *vecs, wl, ones_bd)
        zs_main = z_main[mp:, :RW_PAD]
        prev_s = jnp.concatenate([_pad_cols(state_shift[l], RW_PAD), zs_main[:ms - bs]], axis=0)
        pre_s = _rwkv_pre(z_main, prev_s, mp, ms, bs, False, mu, *vecs, wl, ones_bd)

        i_lo = LANES // (bp * HEADS)

        def p_keyed(a):
            a = a.reshape(bp, lp, HEADS, HEAD).transpose(1, 3, 0, 2).reshape(lp, HEAD, bp * HEADS)
            return jnp.concatenate([a] * i_lo, axis=-1)

        r_p, w_p, k_p, v_p, kk_p, b_p, g_p, bon_p = pre_p
        v_pt = v_p.reshape(bp, lp, HEADS, HEAD // i_lo, i_lo).transpose(1, 3, 4, 0, 2)
        v_pt = v_pt.reshape(lp, HEAD // i_lo, LANES)
        y_pt, st_p = _rwkv_scan(p_keyed(r_p), p_keyed(w_p), p_keyed(k_p), p_keyed(kk_p), p_keyed(b_p),
                                v_pt, jnp.zeros((HEAD // i_lo, HEAD, LANES), F32), 64)
        y_pn = y_pt.reshape(lp, HEAD // i_lo, i_lo, bp, HEADS).transpose(3, 0, 4, 1, 2).reshape(mp, RW)
        new_wkv_p = st_p.reshape(HEAD // i_lo, HEAD, i_lo, bp, HEADS).transpose(3, 4, 0, 2, 1)
        new_wkv_p = new_wkv_p.reshape(bp, HEADS, HEAD, HEAD)

        def s_keyed(a):
            return a.reshape(ls, bs, HEADS, HEAD).transpose(0, 3, 1, 2).reshape(ls, HEAD, bs * HEADS)

        r_s, w_s, k_s, v_s, kk_s, b_s, g_s, bon_s = pre_s
        s0_s = state_wkv[l].transpose(2, 3, 0, 1).reshape(HEAD, HEAD, bs * HEADS)
        y_st, st_s = _rwkv_scan(s_keyed(r_s), s_keyed(w_s), s_keyed(k_s), s_keyed(kk_s), s_keyed(b_s),
                                s_keyed(v_s), s0_s, ls)
        y_sn = y_st.reshape(ls, HEAD, bs, HEADS).transpose(0, 2, 3, 1).reshape(ms, RW)
        new_wkv_s = st_s.reshape(HEAD, HEAD, bs, HEADS).transpose(2, 3, 0, 1)

        lnx_g, lnx_b = rw_lnx_g[l].reshape(1, RW), rw_lnx_b[l].reshape(1, RW)
        y_a = _rwkv_post(jnp.concatenate([y_pn, y_sn], axis=0), jnp.concatenate([bon_p, bon_s], axis=0),
                         jnp.concatenate([g_p, g_s], axis=0), lnx_g, lnx_b, ones_bd, tm_mid)
        new_shift_p = z_main[lp - 1:mp:lp, :RW_PROJ]
        new_shift_s = z_main[m - bs:, :RW_PROJ]

        pw = pool_w[l].astype(BF16)
        ps = pool_scale[l].reshape(1, POOL_W)
        y_b_p = _pool_prompt(z_main, bp, lp, pw, ps)
        zb_s = z_main[mp:, RW_PAD:RW_PAD + POOL_W].reshape(ls, bs, POOL_W)
        full_s = jnp.concatenate([jnp.swapaxes(state_pool[l], 0, 1), zb_s], axis=0)
        y_b_s = _pool_sample(full_s, ls, past_len, pw, ps)
        y_b = jnp.concatenate([y_b_p, y_b_s], axis=0)
        new_pool_p = z_main[:mp, RW_PAD:RW_PAD + POOL_W].reshape(bp, lp, POOL_W)[:, lp - POOL_BUF:]
        new_pool_s = jnp.swapaxes(full_s[full_s.shape[0] - POOL_BUF:], 0, 1)

        bmat, cmat, lam_t, pw_t = _s5_params(s5_a_re[l], s5_a_im[l], s5_log_dt[l], s5_b_re[l], s5_b_im[l],
                                             s5_c_re[l], s5_c_im[l])
        dskip, gw, gb = s5_d[l].reshape(1, S5_W), s5_glu_w[l].astype(BF16), s5_glu_b[l].reshape(1, S5_W)
        y_c_p, re_p, im_p = _s5_prompt(z_main, bp, lp, 512, bmat, cmat, lam_t, pw_t, dskip, gw, gb)
        y_c_s, re_s, im_s = _s5_sample(z_main, mp, bs, ls, state_s5_re[l].reshape(bs, S5_N),
                                       state_s5_im[l].reshape(bs, S5_N), bmat, cmat, lam_t, dskip, gw, gb)
        y_c = jnp.concatenate([y_c_p, y_c_s], axis=0)

        rw_pad = _pad_cols(router_w[l], LANES)
        rb_pad = jnp.concatenate([router_b[l], jnp.full((LANES - N_EXP,), -1e30, F32)]).reshape(1, LANES)
        x, h2, logits = _merge(y_a, y_b, y_c, z_gate, w_br[l].astype(BF16), w_out[l].astype(BF16), x, gate1,
                               norm2_g[l].reshape(1, d), scale2, shift2, rw_pad, rb_pad,
                               mod_idx_for(tm_mid), tm_mid)

        n_rows_pad = (m * TOP_K // MOE_BM + N_EXP) * MOE_BM
        gate_w, dest, rows_tok, blk_e, n_used = _route(logits[:, :N_EXP], n_rows_pad)
        yb = _moe(h2[rows_tok], blk_e, n_used, moe_w_gu[l], moe_b_gu[l], moe_w_down[l], moe_b_down[l])
        x = _combine(x, yb[dest].reshape(m, TOP_K * d), gate_w, gate2, mod_idx_for(tm_mid), tm_mid)

        outs_p.append((new_shift_p, new_wkv_p, new_pool_p, re_p.reshape(bp, S5_G, S5_P),
                       im_p.reshape(bp, S5_G, S5_P)))
        outs_s.append((new_shift_s, new_wkv_s, new_pool_s, re_s.reshape(bs, S5_G, S5_P),
                       im_s.reshape(bs, S5_G, S5_P)))

    y = _final_norm(x, final_norm_g.reshape(1, d), tm_mid)
    y_prompt = y[:mp].reshape(bp, lp, d)
    y_sample = jnp.swapaxes(y[mp:].reshape(ls, bs, d), 0, 1)
    p_shift, p_wkv, p_pool, p_re, p_im = (jnp.stack([o[j] for o in outs_p]) for j in range(5))
    s_shift, s_wkv, s_pool, s_re, s_im = (jnp.stack([o[j] for o in outs_s]) for j in range(5))
    return (y_prompt, y_sample, p_wkv, p_shift, p_pool, p_re, p_im,
            s_wkv, s_shift, s_pool, s_re, s_im)
```

```python
import functools

import jax
import jax.numpy as jnp
from jax import lax
from jax.experimental import pallas as pl
from jax.experimental.pallas import tpu as pltpu

F32, BF16 = jnp.float32, jnp.bfloat16

LANES = 128
SUBLANES = 8
VMEM_LIMIT = 56 * 1024 * 1024

HEAD = 64
HEADS = 16
RW = HEAD * HEADS
R_DECAY, R_AAA, R_GATE = 64, 64, 160
RW_PROJ = 3 * RW + R_DECAY + R_AAA + R_GATE
RW_PAD = 3584
LORA_IN = RW_PAD - 3 * RW
POOL_W = 512
WINDOWS = (2, 4, 8, 16)
POOL_G = POOL_W // len(WINDOWS)
POOL_BUF = max(WINDOWS) - 1
S5_W = 512
S5_G, S5_CH, S5_P = 32, 16, 64
S5_N = S5_G * S5_P
MAIN_W = RW_PAD + POOL_W + S5_W
N_EXP, TOP_K = 32, 4
SW_LIMIT, SW_ALPHA = 7.0, 1.702
NORM_EPS, GN_EPS = 1e-5, 64e-5
MOD_ROWS = 128


def _params(sem):
    return pltpu.CompilerParams(dimension_semantics=sem, vmem_limit_bytes=VMEM_LIMIT)


def _dot(a, b):
    return jnp.dot(a, b, preferred_element_type=F32)


def _segsum(x, ones_bd):
    hi = x.astype(BF16)
    lo = (x - hi.astype(F32)).astype(BF16)
    return _dot(hi, ones_bd) + _dot(lo, ones_bd)


def _tile_rows(t, rows):
    return jnp.broadcast_to(t[None], (rows // MOD_ROWS,) + t.shape).reshape(rows, t.shape[-1])


def _ada_kernel(c_ref, w_ref, b_ref, o_ref):
    c = c_ref[...]
    a = (c * jax.nn.sigmoid(c)).astype(BF16)
    o_ref[0] = _dot(a, w_ref[0].astype(BF16)) + b_ref[0]


def _adaln(c_all, w_ada, b_ada):
    nl, d, n = w_ada.shape
    r = c_all.shape[0]
    tn = 1024
    return pl.pallas_call(
        _ada_kernel,
        grid=(nl, n // tn),
        in_specs=[pl.BlockSpec((r, d), lambda l, j: (0, 0)),
                  pl.BlockSpec((1, d, tn), lambda l, j: (l, 0, j)),
                  pl.BlockSpec((1, 1, tn), lambda l, j: (l, 0, j))],
        out_specs=pl.BlockSpec((1, r, tn), lambda l, j: (l, 0, j)),
        out_shape=jax.ShapeDtypeStruct((nl, r, n), F32),
        compiler_params=_params(("parallel", "parallel")),
        name="adaln",
    )(c_all, w_ada, b_ada.reshape(nl, 1, n))


def _norm_mod(x, g, scale, shift):
    y = x * lax.rsqrt(jnp.mean(x * x, axis=-1, keepdims=True) + NORM_EPS) * g
    rows = x.shape[0]
    return y * (1.0 + _tile_rows(scale, rows)) + _tile_rows(shift, rows)


def _inproj_kernel(x_ref, g_ref, sc_ref, sh_ref, w_ref, o_ref, h_scr):
    @pl.when(pl.program_id(1) == 0)
    def _():
        h_scr[...] = _norm_mod(x_ref[...], g_ref[...], sc_ref[...], sh_ref[...]).astype(BF16)

    o_ref[...] = _dot(h_scr[...], w_ref[...])


def _inproj(x, g, scale_t, shift_t, w, mod_idx, tm):
    m, d = x.shape
    n = w.shape[1]
    tn = 512
    return pl.pallas_call(
        _inproj_kernel,
        grid=(m // tm, n // tn),
        in_specs=[pl.BlockSpec((tm, d), lambda i, j: (i, 0)),
                  pl.BlockSpec((1, d), lambda i, j: (0, 0)),
                  pl.BlockSpec((MOD_ROWS, d), lambda i, j: (mod_idx(i), 0)),
                  pl.BlockSpec((MOD_ROWS, d), lambda i, j: (mod_idx(i), 0)),
                  pl.BlockSpec((d, tn), lambda i, j: (0, j))],
        out_specs=pl.BlockSpec((tm, tn), lambda i, j: (i, j)),
        out_shape=jax.ShapeDtypeStruct((m, n), F32),
        scratch_shapes=[pltpu.VMEM((tm, d), BF16)],
        compiler_params=_params(("parallel", "arbitrary")),
        name="inproj",
    )(x, g, scale_t, shift_t, w)


def _softplus(x):
    return jnp.maximum(x, 0.0) + jnp.log1p(jnp.exp(-jnp.abs(x)))


def _rwkv_pre_kernel(roll_mode, z_ref, zp_ref, mu_ref, w0_ref, a0_ref, kkw_ref, ka_ref, rk_ref,
                     wl_ref, ones_ref, r_o, w_o, k_o, v_o, kk_o, b_o, g_o, bon_o):
    z = z_ref[...]
    if roll_mode:
        row = lax.broadcasted_iota(jnp.int32, z.shape, 0)
        zp = jnp.where(row == 0, zp_ref[0], pltpu.roll(z, 1, axis=0))
    else:
        zp = zp_ref[...]
    zs = z + (zp - z) * mu_ref[...]
    r, k, v = zs[:, :RW], zs[:, RW:2 * RW], zs[:, 2 * RW:3 * RW]
    l0 = zs[:, 3 * RW:3 * RW + LANES]
    lane = lax.broadcasted_iota(jnp.int32, l0.shape, 1)
    l0 = jnp.where(lane < R_DECAY, jnp.tanh(l0), l0)
    l1 = jax.nn.sigmoid(zs[:, 3 * RW + LANES:])
    lin = jnp.concatenate([l0, l1], axis=1).astype(BF16)
    lo = _dot(lin, wl_ref[...])
    w_log = -_softplus(-(w0_ref[...] + lo[:, :RW])) - 0.5
    a = jax.nn.sigmoid(a0_ref[...] + lo[:, RW:2 * RW])
    ones_bd = ones_ref[...]
    kk = k * kkw_ref[...]
    kkn = kk / jnp.maximum(jnp.sqrt(_segsum(kk * kk, ones_bd)), 1e-12)
    kh = k * (1.0 + (a - 1.0) * ka_ref[...])
    r_o[...] = r
    w_o[...] = jnp.exp(-jnp.exp(w_log))
    k_o[...] = kh
    v_o[...] = v
    kk_o[...] = kkn
    b_o[...] = kkn * a
    g_o[...] = lo[:, 2 * RW:]
    bon_o[...] = _segsum(r * kh * rk_ref[...], ones_bd) * v


def _rwkv_pre(z_main, zp, row0, rows, tt, roll_mode, mu, w0, a0, kkw, ka, rk, wl, ones_bd):
    blk0 = row0 // tt
    vec = lambda w: pl.BlockSpec((1, w), lambda i: (0, 0))
    if roll_mode:
        zp_spec = pl.BlockSpec((1, 1, RW_PAD), lambda i: (i, 0, 0))
    else:
        zp_spec = pl.BlockSpec((tt, RW_PAD), lambda i: (i, 0))
    out_spec = pl.BlockSpec((tt, RW), lambda i: (i, 0))
    return pl.pallas_call(
        functools.partial(_rwkv_pre_kernel, roll_mode),
        grid=(rows // tt,),
        in_specs=[pl.BlockSpec((tt, RW_PAD), lambda i: (blk0 + i, 0)), zp_spec,
                  vec(RW_PAD), vec(RW), vec(RW), vec(RW), vec(RW), vec(RW),
                  pl.BlockSpec((LORA_IN, 3 * RW), lambda i: (0, 0)),
                  pl.BlockSpec((RW, RW), lambda i: (0, 0))],
        out_specs=[out_spec] * 8,
        out_shape=[jax.ShapeDtypeStruct((rows, RW), F32)] * 8,
        compiler_params=_params(("parallel",)),
        name="rwkv_pre",
    )(z_main, zp, mu, w0, a0, kkw, ka, rk, wl, ones_bd)


def _rwkv_scan_kernel(r_ref, w_ref, k_ref, kk_ref, b_ref, v_ref, s0_ref, y_ref, sT_ref, s_scr):
    tc = pl.program_id(1)
    n_t = r_ref.shape[0]
    n_i = s_scr.shape[0]

    @pl.when(tc == 0)
    def _():
        s_scr[...] = s0_ref[...]

    def step(t, carry):
        r_t, w_t, k_t, kk_t, b_t = r_ref[t], w_ref[t], k_ref[t], kk_ref[t], b_ref[t]

        def rows8(ib, c2):
            i0 = pl.multiple_of(ib * SUBLANES, SUBLANES)
            v8 = v_ref[t, pl.ds(i0, SUBLANES), :]
            ys = []
            for ii in range(SUBLANES):
                s_old = s_scr[i0 + ii]
                sa = jnp.sum(s_old * kk_t, axis=0, keepdims=True)
                s_new = s_old * w_t - sa * b_t + v8[ii:ii + 1] * k_t
                ys.append(jnp.sum(s_new * r_t, axis=0, keepdims=True))
                s_scr[i0 + ii] = s_new
            y_ref[t, pl.ds(i0, SUBLANES), :] = jnp.concatenate(ys, axis=0)
            return c2

        return lax.fori_loop(0, n_i // SUBLANES, rows8, carry)

    lax.fori_loop(0, n_t, step, 0)

    @pl.when(tc == pl.num_programs(1) - 1)
    def _():
        sT_ref[...] = s_scr[...]


def _rwkv_scan(r, w, k, kk, b, v, s0, tchunk):
    n_t, _, n_l = r.shape
    n_i = s0.shape[0]
    op = pl.BlockSpec((tchunk, HEAD, LANES), lambda l, t: (t, 0, l))
    vy = pl.BlockSpec((tchunk, n_i, LANES), lambda l, t: (t, 0, l))
    st = pl.BlockSpec((n_i, HEAD, LANES), lambda l, t: (0, 0, l))
    return pl.pallas_call(
        _rwkv_scan_kernel,
        grid=(n_l // LANES, n_t // tchunk),
        in_specs=[op, op, op, op, op, vy, st],
        out_specs=[vy, st],
        out_shape=[jax.ShapeDtypeStruct((n_t, n_i, n_l), F32),
                   jax.ShapeDtypeStruct((n_i, HEAD, n_l), F32)],
        scratch_shapes=[pltpu.VMEM((n_i, HEAD, LANES), F32)],
        compiler_params=_params(("parallel", "arbitrary")),
        name="rwkv_scan",
    )(r, w, k, kk, b, v, s0)


def _rwkv_post_kernel(y_ref, bon_ref, g_ref, lg_ref, lb_ref, ones_ref, o_ref):
    y = y_ref[...]
    ones_bd = ones_ref[...]
    yc = y - _segsum(y, ones_bd) * (1.0 / HEAD)
    var = _segsum(yc * yc, ones_bd) * (1.0 / HEAD)
    yn = yc * lax.rsqrt(var + GN_EPS) * lg_ref[...] + lb_ref[...]
    o_ref[...] = ((yn + bon_ref[...]) * g_ref[...]).astype(BF16)


def _rwkv_post(y, bonus, g, lnx_g, lnx_b, ones_bd, tt):
    rows = y.shape[0]
    blk = pl.BlockSpec((tt, RW), lambda i: (i, 0))
    vec = pl.BlockSpec((1, RW), lambda i: (0, 0))
    return pl.pallas_call(
        _rwkv_post_kernel,
        grid=(rows // tt,),
        in_specs=[blk, blk, blk, vec, vec, pl.BlockSpec((RW, RW), lambda i: (0, 0))],
        out_specs=blk,
        out_shape=jax.ShapeDtypeStruct((rows, RW), BF16),
        compiler_params=_params(("parallel",)),
        name="rwkv_post",
    )(y, bonus, g, lnx_g, lnx_b, ones_bd)


def _pool_prompt_kernel(start_pos, z_ref, pw_ref, ps_ref, y_ref):
    n = z_ref.shape[0]
    row = lax.broadcasted_iota(jnp.int32, (n, POOL_G), 0)
    for gi, win in enumerate(WINDOWS):
        sl = slice(gi * POOL_G, (gi + 1) * POOL_G)
        x = z_ref[:, sl]
        s = x
        k = 1
        while k < win:
            s = s + jnp.where(row >= k, pltpu.roll(s, k, axis=0), 0.0)
            k *= 2
        cnt = jnp.minimum(win, row + (start_pos + 1)).astype(F32)
        d = s / cnt - x
        y_ref[:, sl] = (_dot(d.astype(BF16), pw_ref[gi]) * ps_ref[:, sl]).astype(BF16)


def _pool_prompt(z_main, n_b, n_t, pw, ps):
    cb = RW_PAD // POOL_W
    return pl.pallas_call(
        functools.partial(_pool_prompt_kernel, 0),
        grid=(n_b,),
        in_specs=[pl.BlockSpec((n_t, POOL_W), lambda b: (b, cb)),
                  pl.BlockSpec((len(WINDOWS), POOL_G, POOL_G), lambda b: (0, 0, 0)),
                  pl.BlockSpec((1, POOL_W), lambda b: (0, 0))],
        out_specs=pl.BlockSpec((n_t, POOL_W), lambda b: (b, 0)),
        out_shape=jax.ShapeDtypeStruct((n_b * n_t, POOL_W), BF16),
        compiler_params=_params(("parallel",)),
        name="pool_prompt",
    )(z_main, pw, ps)


def _pool_sample_kernel(start_pos, n_t, full_ref, pw_ref, ps_ref, y_ref):
    n_b = full_ref.shape[1]
    for gi, win in enumerate(WINDOWS):
        sl = slice(gi * POOL_G, (gi + 1) * POOL_G)
        f = full_ref[:, :, sl]
        s = f
        k = 1
        while k < win:
            s = s[k:] + s[:-k]
            k *= 2
        s = s[s.shape[0] - n_t:]
        x = f[POOL_BUF:]
        pos = start_pos + lax.broadcasted_iota(jnp.int32, s.shape, 0)
        cnt = jnp.minimum(win, pos + 1).astype(F32)
        d = (s / cnt - x).reshape(n_t * n_b, POOL_G)
        y_ref[:, sl] = (_dot(d.astype(BF16), pw_ref[gi]) * ps_ref[:, sl]).astype(BF16)


def _pool_sample(full, n_t, start_pos, pw, ps):
    n_f, n_b, _ = full.shape
    return pl.pallas_call(
        functools.partial(_pool_sample_kernel, start_pos, n_t),
        grid=(1,),
        in_specs=[pl.BlockSpec((n_f, n_b, POOL_W), lambda i: (0, 0, 0)),
                  pl.BlockSpec((len(WINDOWS), POOL_G, POOL_G), lambda i: (0, 0, 0)),
                  pl.BlockSpec((1, POOL_W), lambda i: (0, 0))],
        out_specs=pl.BlockSpec((n_t * n_b, POOL_W), lambda i: (0, 0)),
        out_shape=jax.ShapeDtypeStruct((n_t * n_b, POOL_W), BF16),
        compiler_params=_params(("arbitrary",)),
        name="pool_sample",
    )(full, pw, ps)


def _s5_tail(x_all, u, cmat_ref, d_ref, gw_ref, gb_ref):
    y = _dot(x_all.astype(BF16), cmat_ref[...]) + d_ref[...] * u
    y = jax.nn.gelu(y)
    return (y * jax.nn.sigmoid(_dot(y.astype(BF16), gw_ref[...]) + gb_ref[...])).astype(BF16)


S5_LC = 512


def _s5_prompt_kernel(u_ref, bmat_ref, cmat_ref, lam_ref, pw_ref, d_ref, gw_ref, gb_ref,
                      y_ref, xre_o, xim_o, x_scr, c_scr):
    tc = pl.program_id(1)

    @pl.when(tc == 0)
    def _():
        c_scr[...] = jnp.zeros_like(c_scr)

    u = u_ref[...]
    n = u.shape[0]
    x_scr[...] = _dot(u.astype(BF16), bmat_ref[...])
    row = lax.broadcasted_iota(jnp.int32, (SUBLANES, S5_LC), 0)
    for lc in range(S5_N // S5_LC):
        re_sl = pl.ds(lc * S5_LC, S5_LC)
        im_sl = pl.ds(S5_N + lc * S5_LC, S5_LC)
        lam = [(lam_ref[2 * j:2 * j + 1, re_sl], lam_ref[2 * j + 1:2 * j + 2, re_sl]) for j in range(3)]
        p_re, p_im = pw_ref[:, re_sl], pw_ref[:, im_sl]

        def blk(rb, carry, re_sl=re_sl, im_sl=im_sl, lam=lam, p_re=p_re, p_im=p_im):
            c_re, c_im = carry
            rows = pl.ds(pl.multiple_of(rb * SUBLANES, SUBLANES), SUBLANES)
            xr, xi = x_scr[rows, re_sl], x_scr[rows, im_sl]
            for j, (l_re, l_im) in enumerate(lam):
                kshift = 1 << j
                sr = jnp.where(row >= kshift, pltpu.roll(xr, kshift, axis=0), 0.0)
                si = jnp.where(row >= kshift, pltpu.roll(xi, kshift, axis=0), 0.0)
                xr, xi = xr + (l_re * sr - l_im * si), xi + (l_re * si + l_im * sr)
            xr, xi = xr + (p_re * c_re - p_im * c_im), xi + (p_re * c_im + p_im * c_re)
            x_scr[rows, re_sl] = xr
            x_scr[rows, im_sl] = xi
            return xr[SUBLANES - 1:], xi[SUBLANES - 1:]

        c_re, c_im = lax.fori_loop(0, n // SUBLANES, blk, (c_scr[0:1, re_sl], c_scr[0:1, im_sl]))
        c_scr[0:1, re_sl] = c_re
        c_scr[0:1, im_sl] = c_im

    y_ref[...] = _s5_tail(x_scr[...], u, cmat_ref, d_ref, gw_ref, gb_ref)

    @pl.when(tc == pl.num_programs(1) - 1)
    def _():
        xre_o[0] = c_scr[0:1, :S5_N]
        xim_o[0] = c_scr[0:1, S5_N:]


def _s5_prompt(z_main, n_b, n_t, tt, bmat, cmat, lam_t, pw_t, dskip, gw, gb):
    cb = (RW_PAD + POOL_W) // S5_W
    n_tc = n_t // tt
    const = lambda shape: pl.BlockSpec(shape, lambda b, t: (0,) * len(shape))
    return pl.pallas_call(
        _s5_prompt_kernel,
        grid=(n_b, n_tc),
        in_specs=[pl.BlockSpec((tt, S5_W), lambda b, t: (b * n_tc + t, cb)),
                  const((S5_W, 2 * S5_N)), const((2 * S5_N, S5_W)),
                  const((SUBLANES, S5_N)), const((SUBLANES, 2 * S5_N)),
                  const((1, S5_W)), const((S5_W, S5_W)), const((1, S5_W))],
        out_specs=[pl.BlockSpec((tt, S5_W), lambda b, t: (b * n_tc + t, 0)),
                   pl.BlockSpec((1, 1, S5_N), lambda b, t: (b, 0, 0)),
                   pl.BlockSpec((1, 1, S5_N), lambda b, t: (b, 0, 0))],
        out_shape=[jax.ShapeDtypeStruct((n_b * n_t, S5_W), BF16),
                   jax.ShapeDtypeStruct((n_b, 1, S5_N), F32),
                   jax.ShapeDtypeStruct((n_b, 1, S5_N), F32)],
        scratch_shapes=[pltpu.VMEM((tt, 2 * S5_N), F32), pltpu.VMEM((SUBLANES, 2 * S5_N), F32)],
        compiler_params=_params(("parallel", "arbitrary")),
        name="s5_prompt",
    )(z_main, bmat, cmat, lam_t, pw_t, dskip, gw, gb)


def _s5_sample_kernel(n_t, u_ref, x0re_ref, x0im_ref, bmat_ref, cmat_ref, lam_ref, d_ref, gw_ref,
                      gb_ref, y_ref, xre_o, xim_o, x_scr):
    u = u_ref[...]
    n_b = u.shape[0] // n_t
    x_scr[...] = _dot(u.astype(BF16), bmat_ref[...])
    for lc in range(S5_N // S5_LC):
        re_sl = pl.ds(lc * S5_LC, S5_LC)
        im_sl = pl.ds(S5_N + lc * S5_LC, S5_LC)
        l_re, l_im = lam_ref[0:1, re_sl], lam_ref[1:2, re_sl]
        xr, xi = x0re_ref[:, re_sl], x0im_ref[:, re_sl]
        for t in range(n_t):
            rows = pl.ds(t * n_b, n_b)
            xr, xi = (l_re * xr - l_im * xi + x_scr[rows, re_sl],
                      l_re * xi + l_im * xr + x_scr[rows, im_sl])
            x_scr[rows, re_sl] = xr
            x_scr[rows, im_sl] = xi
        xre_o[:, re_sl] = xr
        xim_o[:, re_sl] = xi
    y_ref[...] = _s5_tail(x_scr[...], u, cmat_ref, d_ref, gw_ref, gb_ref)


def _s5_sample(z_main, row0, n_b, n_t, x0re, x0im, bmat, cmat, lam_t, dskip, gw, gb):
    rows = n_b * n_t
    cb = (RW_PAD + POOL_W) // S5_W
    const = lambda shape: pl.BlockSpec(shape, lambda i: (0,) * len(shape))
    return pl.pallas_call(
        functools.partial(_s5_sample_kernel, n_t),
        grid=(1,),
        in_specs=[pl.BlockSpec((rows, S5_W), lambda i: (row0 // rows, cb)),
                  const((n_b, S5_N)), const((n_b, S5_N)),
                  const((S5_W, 2 * S5_N)), const((2 * S5_N, S5_W)), const((SUBLANES, S5_N)),
                  const((1, S5_W)), const((S5_W, S5_W)), const((1, S5_W))],
        out_specs=[const((rows, S5_W)), const((n_b, S5_N)), const((n_b, S5_N))],
        out_shape=[jax.ShapeDtypeStruct((rows, S5_W), BF16),
                   jax.ShapeDtypeStruct((n_b, S5_N), F32),
                   jax.ShapeDtypeStruct((n_b, S5_N), F32)],
        scratch_shapes=[pltpu.VMEM((rows, 2 * S5_N), F32)],
        compiler_params=_params(("arbitrary",)),
        name="s5_sample",
    )(z_main, x0re, x0im, bmat, cmat, lam_t, dskip, gw, gb)


def _merge_kernel(ya_ref, yb_ref, yc_ref, zg_ref, wbr_ref, wout_ref, x_ref, g1_ref, n2_ref,
                  sc2_ref, sh2_ref, rw_ref, rb_ref, xo_ref, h2_ref, lg_ref):
    d = x_ref.shape[1]
    rows = x_ref.shape[0]
    m = jax.nn.sigmoid(zg_ref[:, :d]) * _dot(ya_ref[...], wbr_ref[:RW, :])
    m = m + jax.nn.sigmoid(zg_ref[:, d:2 * d]) * _dot(yb_ref[...], wbr_ref[RW:RW + POOL_W, :])
    m = m + jax.nn.sigmoid(zg_ref[:, 2 * d:]) * _dot(yc_ref[...], wbr_ref[RW + POOL_W:, :])
    xn = x_ref[...] + _tile_rows(g1_ref[...], rows) * _dot(m.astype(BF16), wout_ref[...])
    xo_ref[...] = xn
    h2 = _norm_mod(xn, n2_ref[...], sc2_ref[...], sh2_ref[...])
    h2_ref[...] = h2
    lg_ref[...] = jnp.dot(h2, rw_ref[...], precision=lax.Precision.HIGHEST,
                          preferred_element_type=F32) + rb_ref[...]


def _merge(ya, yb, yc, zg, wbr, wout, x, g1_t, n2g, sc2_t, sh2_t, rw, rb, mod_idx, tm):
    m, d = x.shape
    row = lambda w: pl.BlockSpec((tm, w), lambda i: (i, 0))
    const = lambda shape: pl.BlockSpec(shape, lambda i: (0,) * len(shape), pipeline_mode=pl.Buffered(1))
    mod = pl.BlockSpec((MOD_ROWS, d), lambda i: (mod_idx(i), 0))
    return pl.pallas_call(
        _merge_kernel,
        grid=(m // tm,),
        in_specs=[row(RW), row(POOL_W), row(S5_W), row(3 * d), const(wbr.shape), const(wout.shape),
                  row(d), mod, const((1, d)), mod, mod, const(rw.shape), const((1, LANES))],
        out_specs=[row(d), row(d), row(LANES)],
        out_shape=[jax.ShapeDtypeStruct((m, d), F32), jax.ShapeDtypeStruct((m, d), F32),
                   jax.ShapeDtypeStruct((m, LANES), F32)],
        compiler_params=_params(("parallel",)),
        name="merge",
    )(ya, yb, yc, zg, wbr, wout, x, g1_t, n2g, sc2_t, sh2_t, rw, rb)


MOE_BM = 512
MOE_TF = 512
GATHER_ROWS = 1024


def _gather_rows_kernel(idx_ref, src_ref, dst_ref, sem):
    base = pl.program_id(0) * GATHER_ROWS

    def row_copy(src_row, k):
        return pltpu.make_async_copy(src_ref.at[pl.ds(src_row, 1)], dst_ref.at[pl.ds(base + k, 1)], sem)

    def issue(k, c):
        row_copy(idx_ref[0, 0, k], k).start()
        return c

    lax.fori_loop(0, GATHER_ROWS, issue, 0)

    def drain(k, c):
        row_copy(0, k).wait()
        return c

    lax.fori_loop(0, GATHER_ROWS, drain, 0)


def _gather_rows(src, idx):
    n_out = idx.shape[0]
    d = src.shape[1]
    n_steps = n_out // GATHER_ROWS
    return pl.pallas_call(
        _gather_rows_kernel,
        grid=(n_steps,),
        in_specs=[pl.BlockSpec((1, 1, GATHER_ROWS), lambda i: (i, 0, 0), memory_space=pltpu.SMEM),
                  pl.BlockSpec(memory_space=pl.ANY)],
        out_specs=pl.BlockSpec(memory_space=pl.ANY),
        out_shape=jax.ShapeDtypeStruct((n_out, d), src.dtype),
        scratch_shapes=[pltpu.SemaphoreType.DMA(())],
        compiler_params=pltpu.CompilerParams(dimension_semantics=("arbitrary",), has_side_effects=True,
                                             disable_bounds_checks=True),
        name="gather_rows",
    )(idx.reshape(n_steps, 1, GATHER_ROWS), src)


def _moe_kernel(be_ref, nb_ref, x_ref, wg_ref, wu_ref, bg_ref, bu_ref, wd_ref, bd_ref, o_ref):
    s, f = pl.program_id(0), pl.program_id(1)
    used = s < nb_ref[0]

    @pl.when(jnp.logical_and(jnp.logical_not(used), f == 0))
    def _():
        o_ref[...] = jnp.zeros_like(o_ref)

    @pl.when(used)
    def _():
        x = x_ref[...].astype(BF16)
        g = _dot(x, wg_ref[0].astype(BF16)) + bg_ref[0]
        u = _dot(x, wu_ref[0].astype(BF16)) + bu_ref[0]
        g = jnp.minimum(g, SW_LIMIT)
        u = jnp.clip(u, -SW_LIMIT, SW_LIMIT)
        act = (u + 1.0) * (g * jax.nn.sigmoid(SW_ALPHA * g))
        part = _dot(act.astype(BF16), wd_ref[0].astype(BF16))

        @pl.when(f == 0)
        def _():
            o_ref[...] = part + bd_ref[0]

        @pl.when(f > 0)
        def _():
            o_ref[...] += part


def _moe(xs, blk_e, n_used, layer, w_gu, b_gu, w_down, b_down):
    rows, d = xs.shape
    n_l, n_e, _, two_ff = w_gu.shape
    dff = two_ff // 2
    nf = dff // MOE_TF
    n_blk = rows // MOE_BM
    e0 = layer * n_e

    def f_eff(s, f, nb):
        return jnp.where(s < nb[0], f, nf - 1)

    grid_spec = pltpu.PrefetchScalarGridSpec(
        num_scalar_prefetch=2,
        grid=(n_blk, nf),
        in_specs=[
            pl.BlockSpec((MOE_BM, d), lambda s, f, be, nb: (jnp.minimum(s, nb[0] - 1), 0)),
            pl.BlockSpec((1, d, MOE_TF), lambda s, f, be, nb: (e0 + be[s], 0, f_eff(s, f, nb))),
            pl.BlockSpec((1, d, MOE_TF), lambda s, f, be, nb: (e0 + be[s], 0, nf + f_eff(s, f, nb))),
            pl.BlockSpec((1, 1, MOE_TF), lambda s, f, be, nb: (e0 + be[s], 0, f_eff(s, f, nb))),
            pl.BlockSpec((1, 1, MOE_TF), lambda s, f, be, nb: (e0 + be[s], 0, nf + f_eff(s, f, nb))),
            pl.BlockSpec((1, MOE_TF, d), lambda s, f, be, nb: (e0 + be[s], f_eff(s, f, nb), 0)),
            pl.BlockSpec((1, 1, d), lambda s, f, be, nb: (e0 + be[s], 0, 0)),
        ],
        out_specs=pl.BlockSpec((MOE_BM, d), lambda s, f, be, nb: (s, 0)),
    )
    w_gu = w_gu.reshape(n_l * n_e, d, two_ff)
    b_gu = b_gu.reshape(n_l * n_e, 1, two_ff)
    return pl.pallas_call(
        _moe_kernel,
        grid_spec=grid_spec,
        out_shape=jax.ShapeDtypeStruct((rows, d), F32),
        compiler_params=_params(("arbitrary", "arbitrary")),
        name="moe",
    )(blk_e, n_used, xs, w_gu, w_gu, b_gu, b_gu, w_down.reshape(n_l * n_e, dff, d),
      b_down.reshape(n_l * n_e, 1, d))


def _route(logits, n_rows_pad):
    n_tok = logits.shape[0]
    top_v, top_e = lax.top_k(logits, TOP_K)
    gate = jax.nn.softmax(top_v, axis=-1)
    flat_e = top_e.reshape(-1)
    onehot = (flat_e[:, None] == jnp.arange(N_EXP, dtype=jnp.int32)[None, :]).astype(jnp.int32)
    csum = jnp.cumsum(onehot, axis=0)
    counts = csum[-1]
    rank = jnp.sum((csum - 1) * onehot, axis=1)
    padded = (counts + MOE_BM - 1) // MOE_BM * MOE_BM
    pad_end = jnp.cumsum(padded)
    pad_start = pad_end - padded
    dest = (pad_start[flat_e] + rank).astype(jnp.int32)
    flat_tok = jnp.arange(n_tok * TOP_K, dtype=jnp.int32) // TOP_K
    rows_tok = jnp.zeros((n_rows_pad,), jnp.int32).at[dest].set(flat_tok)
    n_rows_used = pad_end[-1].astype(jnp.int32)
    starts = jnp.arange(n_rows_pad // MOE_BM, dtype=jnp.int32) * MOE_BM
    starts = jnp.minimum(starts, n_rows_used - MOE_BM)
    blk_e = jnp.minimum(jnp.searchsorted(pad_end, starts, side='right'), N_EXP - 1).astype(jnp.int32)
    return gate, dest.reshape(n_tok, TOP_K), rows_tok, blk_e, n_rows_used.reshape(1)


def _combine_kernel(x_ref, y0_ref, y1_ref, y2_ref, y3_ref, gw_ref, g2_ref, o_ref):
    rows = x_ref.shape[0]
    gw = gw_ref[...]
    acc = gw[:, 0:1] * y0_ref[...]
    for k, y_ref in enumerate((y1_ref, y2_ref, y3_ref), start=1):
        acc = acc + gw[:, k:k + 1] * y_ref[...]
    o_ref[...] = x_ref[...] + _tile_rows(g2_ref[...], rows) * acc


def _combine(x, yg, gw, g2_t, mod_idx, tm):
    m, d = x.shape
    n_i = m // tm
    slab = lambda k: pl.BlockSpec((tm, d), lambda i: (k * n_i + i, 0))
    return pl.pallas_call(
        _combine_kernel,
        grid=(n_i,),
        in_specs=[pl.BlockSpec((tm, d), lambda i: (i, 0)), slab(0), slab(1), slab(2), slab(3),
                  pl.BlockSpec((tm, TOP_K), lambda i: (i, 0)),
                  pl.BlockSpec((MOD_ROWS, d), lambda i: (mod_idx(i), 0))],
        out_specs=pl.BlockSpec((tm, d), lambda i: (i, 0)),
        out_shape=jax.ShapeDtypeStruct((m, d), F32),
        compiler_params=_params(("parallel",)),
        name="combine",
    )(x, yg, yg, yg, yg, gw, g2_t)


def _final_norm_kernel(x_ref, g_ref, o_ref):
    x = x_ref[...]
    o_ref[...] = x * lax.rsqrt(jnp.mean(x * x, axis=-1, keepdims=True) + NORM_EPS) * g_ref[...]


def _final_norm(x, g, tm):
    m, d = x.shape
    return pl.pallas_call(
        _final_norm_kernel,
        grid=(m // tm,),
        in_specs=[pl.BlockSpec((tm, d), lambda i: (i, 0)), pl.BlockSpec((1, d), lambda i: (0, 0))],
        out_specs=pl.BlockSpec((tm, d), lambda i: (i, 0)),
        out_shape=jax.ShapeDtypeStruct((m, d), F32),
        compiler_params=_params(("parallel",)),
        name="final_norm",
    )(x, g)


def _s5_params(a_re, a_im, log_dt, b_re, b_im, c_re, c_im):
    dt = jnp.exp(log_dt)[:, None]
    mag = jnp.exp(a_re * dt)
    lb_re, lb_im = mag * jnp.cos(a_im * dt), mag * jnp.sin(a_im * dt)
    nr, ni = lb_re - 1.0, lb_im
    den = a_re * a_re + a_im * a_im
    f_re = (nr * a_re + ni * a_im) / den
    f_im = (ni * a_re - nr * a_im) / den
    bb_re = f_re[..., None] * b_re - f_im[..., None] * b_im
    bb_im = f_re[..., None] * b_im + f_im[..., None] * b_re
    eye = jnp.eye(S5_G, dtype=F32)
    bd_in = lambda w: jnp.einsum('gpc,gh->gchp', w, eye).reshape(S5_W, S5_N)
    bd_out = lambda w: jnp.einsum('gcp,gh->gphc', w, eye).reshape(S5_N, S5_W)
    bmat = jnp.concatenate([bd_in(bb_re), bd_in(bb_im)], axis=1).astype(BF16)
    cmat = jnp.concatenate([bd_out(c_re), -bd_out(c_im)], axis=0).astype(BF16)
    l_re, l_im = lb_re.reshape(1, S5_N), lb_im.reshape(1, S5_N)
    pows = [(l_re, l_im)]
    for _ in range(SUBLANES - 1):
        p_re, p_im = pows[-1]
        pows.append((p_re * l_re - p_im * l_im, p_re * l_im + p_im * l_re))
    lam_t = jnp.concatenate([pows[0][0], pows[0][1], pows[1][0], pows[1][1], pows[3][0], pows[3][1],
                             jnp.zeros((2, S5_N), F32)], axis=0)
    pw_t = jnp.concatenate([jnp.concatenate([p[0] for p in pows], axis=0),
                            jnp.concatenate([p[1] for p in pows], axis=0)], axis=1)
    return bmat, cmat, lam_t, pw_t


def _pad_cols(w, width):
    return jnp.pad(w, ((0, 0), (0, width - w.shape[1])))


def kernel(x_prompt, x_sample, c_prompt, c_sample, state_wkv, state_shift, state_pool, state_s5_re, state_s5_im, norm1_g, norm2_g, final_norm_g, w_ada, b_ada, w_in, rw_mu, rw_w0, rw_w2, rw_a0, rw_a2, rw_g2, rw_kk, rw_ka, rw_rk, rw_lnx_g, rw_lnx_b, pool_w, pool_scale, s5_a_re, s5_a_im, s5_log_dt, s5_b_re, s5_b_im, s5_c_re, s5_c_im, s5_d, s5_glu_w, s5_glu_b, w_br, w_out, router_w, router_b, moe_w_gu, moe_b_gu, moe_w_down, moe_b_down):
    bp, lp, d = x_prompt.shape
    bs, ls, _ = x_sample.shape
    depth = w_in.shape[0]
    mp, ms = bp * lp, bs * ls
    m = mp + ms
    past_len = 16384
    assert bs == MOD_ROWS and bp * HEADS * 2 == LANES and lp % 1024 == 0 and ms == 1024

    def mod_idx_for(tm):
        n_p = mp // tm
        return lambda i: jnp.where(i < n_p, (i * tm) // lp, bp)

    tm_big, tm_mid = 1024, 256
    ones_bd = jnp.kron(jnp.eye(HEADS, dtype=F32), jnp.ones((HEAD, HEAD), F32)).astype(BF16)

    x = jnp.concatenate([x_prompt.reshape(mp, d), jnp.swapaxes(x_sample, 0, 1).reshape(ms, d)], axis=0)
    c_all = jnp.concatenate([c_prompt, c_sample, jnp.zeros((4, d), F32)], axis=0)
    mod = _adaln(c_all, w_ada, b_ada)

    def table(layer, idx):
        v = mod[layer, :, idx * d:(idx + 1) * d]
        return jnp.concatenate([jnp.repeat(v[:bp], MOD_ROWS, axis=0), v[bp:bp + bs]], axis=0)

    outs_p, outs_s = [], []
    for l in range(depth):
        shift1, scale1, gate1, shift2, scale2, gate2 = (table(l, i) for i in range(6))

        w_l = w_in[l]
        w_main = jnp.concatenate([_pad_cols(w_l[:, :RW_PROJ], RW_PAD),
                                  w_l[:, RW_PROJ:RW_PROJ + POOL_W + S5_W]], axis=1).astype(BF16)
        w_gate = w_l[:, RW_PROJ + POOL_W + S5_W:].astype(BF16)
        n1 = norm1_g[l].reshape(1, d)
        z_main = _inproj(x, n1, scale1, shift1, w_main, mod_idx_for(tm_big), tm_big)
        z_gate = _inproj(x, n1, scale1, shift1, w_gate, mod_idx_for(tm_big), tm_big)

        mu = _pad_cols(rw_mu[l].reshape(1, RW_PROJ), RW_PAD)
        wl = jnp.zeros((LORA_IN, 3 * RW), F32)
        wl = wl.at[:R_DECAY, :RW].set(rw_w2[l])
        wl = wl.at[R_DECAY:R_DECAY + R_AAA, RW:2 * RW].set(rw_a2[l])
        wl = wl.at[R_DECAY + R_AAA:R_DECAY + R_AAA + R_GATE, 2 * RW:].set(rw_g2[l]).astype(BF16)
        vecs = [v.reshape(1, RW) for v in (rw_w0[l], rw_a0[l], rw_kk[l], rw_ka[l], rw_rk[l])]
        tt_p = 256
        n_tp = mp // tt_p
        last_rows = z_main[tt_p - 1:mp:tt_p, :RW_PAD]
        prev_p = jnp.concatenate([jnp.zeros((1, RW_PAD), F32), last_rows[:-1]], axis=0)
        first_of_seq = (jnp.arange(n_tp) % (lp // tt_p) == 0)[:, None]
        prev_p = jnp.where(first_of_seq, 0.0, prev_p).reshape(n_tp, 1, RW_PAD)
        pre_p = _rwkv_pre(z_main, prev_p, 0, mp, tt_p, True, mu, *vecs, wl, ones_bd)
        zs_main = z_main[mp:, :RW_PAD]
        prev_s = jnp.concatenate([_pad_cols(state_shift[l], RW_PAD), zs_main[:ms - bs]], axis=0)
        pre_s = _rwkv_pre(z_main, prev_s, mp, ms, bs, False, mu, *vecs, wl, ones_bd)

        i_lo = LANES // (bp * HEADS)

        def p_keyed(a):
            a = a.reshape(bp, lp, HEADS, HEAD).transpose(1, 3, 0, 2).reshape(lp, HEAD, bp * HEADS)
            return jnp.concatenate([a] * i_lo, axis=-1)

        r_p, w_p, k_p, v_p, kk_p, b_p, g_p, bon_p = pre_p
        v_pt = v_p.reshape(bp, lp, HEADS, HEAD // i_lo, i_lo).transpose(1, 3, 4, 0, 2)
        v_pt = v_pt.reshape(lp, HEAD // i_lo, LANES)
        y_pt, st_p = _rwkv_scan(p_keyed(r_p), p_keyed(w_p), p_keyed(k_p), p_keyed(kk_p), p_keyed(b_p),
                                v_pt, jnp.zeros((HEAD // i_lo, HEAD, LANES), F32), 64)
        y_pn = y_pt.reshape(lp, HEAD // i_lo, i_lo, bp, HEADS).transpose(3, 0, 4, 1, 2).reshape(mp, RW)
        new_wkv_p = st_p.reshape(HEAD // i_lo, HEAD, i_lo, bp, HEADS).transpose(3, 4, 0, 2, 1)
        new_wkv_p = new_wkv_p.reshape(bp, HEADS, HEAD, HEAD)

        def s_keyed(a):
            return a.reshape(ls, bs, HEADS, HEAD).transpose(0, 3, 1, 2).reshape(ls, HEAD, bs * HEADS)

        r_s, w_s, k_s, v_s, kk_s, b_s, g_s, bon_s = pre_s
        s0_s = state_wkv[l].transpose(2, 3, 0, 1).reshape(HEAD, HEAD, bs * HEADS)
        y_st, st_s = _rwkv_scan(s_keyed(r_s), s_keyed(w_s), s_keyed(k_s), s_keyed(kk_s), s_keyed(b_s),
                                s_keyed(v_s), s0_s, ls)
        y_sn = y_st.reshape(ls, HEAD, bs, HEADS).transpose(0, 2, 3, 1).reshape(ms, RW)
        new_wkv_s = st_s.reshape(HEAD, HEAD, bs, HEADS).transpose(2, 3, 0, 1)

        lnx_g, lnx_b = rw_lnx_g[l].reshape(1, RW), rw_lnx_b[l].reshape(1, RW)
        y_a = _rwkv_post(jnp.concatenate([y_pn, y_sn], axis=0), jnp.concatenate([bon_p, bon_s], axis=0),
                         jnp.concatenate([g_p, g_s], axis=0), lnx_g, lnx_b, ones_bd, tm_mid)
        new_shift_p = z_main[lp - 1:mp:lp, :RW_PROJ]
        new_shift_s = z_main[m - bs:, :RW_PROJ]

        pw = pool_w[l].astype(BF16)
        ps = pool_scale[l].reshape(1, POOL_W)
        y_b_p = _pool_prompt(z_main, bp, lp, pw, ps)
        zb_s = z_main[mp:, RW_PAD:RW_PAD + POOL_W].reshape(ls, bs, POOL_W)
        full_s = jnp.concatenate([jnp.swapaxes(state_pool[l], 0, 1), zb_s], axis=0)
        y_b_s = _pool_sample(full_s, ls, past_len, pw, ps)
        y_b = jnp.concatenate([y_b_p, y_b_s], axis=0)
        new_pool_p = z_main[:mp, RW_PAD:RW_PAD + POOL_W].reshape(bp, lp, POOL_W)[:, lp - POOL_BUF:]
        new_pool_s = jnp.swapaxes(full_s[full_s.shape[0] - POOL_BUF:], 0, 1)

        bmat, cmat, lam_t, pw_t = _s5_params(s5_a_re[l], s5_a_im[l], s5_log_dt[l], s5_b_re[l], s5_b_im[l],
                                             s5_c_re[l], s5_c_im[l])
        dskip, gw, gb = s5_d[l].reshape(1, S5_W), s5_glu_w[l].astype(BF16), s5_glu_b[l].reshape(1, S5_W)
        y_c_p, re_p, im_p = _s5_prompt(z_main, bp, lp, 512, bmat, cmat, lam_t, pw_t, dskip, gw, gb)
        y_c_s, re_s, im_s = _s5_sample(z_main, mp, bs, ls, state_s5_re[l].reshape(bs, S5_N),
                                       state_s5_im[l].reshape(bs, S5_N), bmat, cmat, lam_t, dskip, gw, gb)
        y_c = jnp.concatenate([y_c_p, y_c_s], axis=0)

        rw_pad = _pad_cols(router_w[l], LANES)
        rb_pad = jnp.concatenate([router_b[l], jnp.full((LANES - N_EXP,), -1e30, F32)]).reshape(1, LANES)
        x, h2, logits = _merge(y_a, y_b, y_c, z_gate, w_br[l].astype(BF16), w_out[l].astype(BF16), x, gate1,
                               norm2_g[l].reshape(1, d), scale2, shift2, rw_pad, rb_pad,
                               mod_idx_for(tm_mid), tm_mid)

        n_rows_pad = (m * TOP_K // MOE_BM + N_EXP) * MOE_BM
        gate_w, dest, rows_tok, blk_e, n_rows_used = _route(logits[:, :N_EXP], n_rows_pad)
        xs = _gather_rows(h2, rows_tok)
        yb = _moe(xs, blk_e, n_rows_used // MOE_BM, l, moe_w_gu, moe_b_gu, moe_w_down, moe_b_down)
        yg = _gather_rows(yb, dest.T.reshape(-1))
        x = _combine(x, yg, gate_w, gate2, mod_idx_for(tm_mid), tm_mid)

        outs_p.append((new_shift_p, new_wkv_p, new_pool_p, re_p.reshape(bp, S5_G, S5_P),
                       im_p.reshape(bp, S5_G, S5_P)))
        outs_s.append((new_shift_s, new_wkv_s, new_pool_s, re_s.reshape(bs, S5_G, S5_P),
                       im_s.reshape(bs, S5_G, S5_P)))

    y = _final_norm(x, final_norm_g.reshape(1, d), tm_mid)
    y_prompt = y[:mp].reshape(bp, lp, d)
    y_sample = jnp.swapaxes(y[mp:].reshape(ls, bs, d), 0, 1)
    p_shift, p_wkv, p_pool, p_re, p_im = (jnp.stack([o[j] for o in outs_p]) for j in range(5))
    s_shift, s_wkv, s_pool, s_re, s_im = (jnp.stack([o[j] for o in outs_s]) for j in range(5))
    return (y_prompt, y_sample, p_wkv, p_shift, p_pool, p_re, p_im,
            s_wkv, s_shift, s_pool, s_re, s_im)
```

```python
import functools

import jax
import jax.numpy as jnp
from jax import lax
from jax.experimental import pallas as pl
from jax.experimental.pallas import tpu as pltpu

F32, BF16 = jnp.float32, jnp.bfloat16

LANES = 128
SUBLANES = 8
VMEM_LIMIT = 56 * 1024 * 1024

HEAD = 64
HEADS = 16
RW = HEAD * HEADS
R_DECAY, R_AAA, R_GATE = 64, 64, 160
RW_PROJ = 3 * RW + R_DECAY + R_AAA + R_GATE
RW_PAD = 3584
LORA_IN = RW_PAD - 3 * RW
POOL_W = 512
WINDOWS = (2, 4, 8, 16)
POOL_G = POOL_W // len(WINDOWS)
POOL_BUF = max(WINDOWS) - 1
S5_W = 512
S5_G, S5_CH, S5_P = 32, 16, 64
S5_N = S5_G * S5_P
MAIN_W = RW_PAD + POOL_W + S5_W
N_EXP, TOP_K = 32, 4
SW_LIMIT, SW_ALPHA = 7.0, 1.702
NORM_EPS, GN_EPS = 1e-5, 64e-5
MOD_ROWS = 128


def _params(sem):
    return pltpu.CompilerParams(dimension_semantics=sem, vmem_limit_bytes=VMEM_LIMIT)


def _dot(a, b):
    return jnp.dot(a, b, preferred_element_type=F32)


def _segsum(x, ones_bd):
    hi = x.astype(BF16)
    lo = (x - hi.astype(F32)).astype(BF16)
    return _dot(hi, ones_bd) + _dot(lo, ones_bd)


def _tile_rows(t, rows):
    return jnp.broadcast_to(t[None], (rows // MOD_ROWS,) + t.shape).reshape(rows, t.shape[-1])


def _ada_kernel(c_ref, w_ref, b_ref, o_ref):
    c = c_ref[...]
    a = (c * jax.nn.sigmoid(c)).astype(BF16)
    o_ref[0] = _dot(a, w_ref[0].astype(BF16)) + b_ref[0]


def _adaln(c_all, w_ada, b_ada):
    nl, d, n = w_ada.shape
    r = c_all.shape[0]
    tn = 1024
    return pl.pallas_call(
        _ada_kernel,
        grid=(nl, n // tn),
        in_specs=[pl.BlockSpec((r, d), lambda l, j: (0, 0)),
                  pl.BlockSpec((1, d, tn), lambda l, j: (l, 0, j)),
                  pl.BlockSpec((1, 1, tn), lambda l, j: (l, 0, j))],
        out_specs=pl.BlockSpec((1, r, tn), lambda l, j: (l, 0, j)),
        out_shape=jax.ShapeDtypeStruct((nl, r, n), F32),
        compiler_params=_params(("parallel", "parallel")),
        name="adaln",
    )(c_all, w_ada, b_ada.reshape(nl, 1, n))


def _norm_mod(x, g, scale, shift):
    y = x * lax.rsqrt(jnp.mean(x * x, axis=-1, keepdims=True) + NORM_EPS) * g
    rows = x.shape[0]
    return y * (1.0 + _tile_rows(scale, rows)) + _tile_rows(shift, rows)


def _inproj_kernel(x_ref, g_ref, sc_ref, sh_ref, w_ref, o_ref, h_scr):
    @pl.when(pl.program_id(1) == 0)
    def _():
        h_scr[...] = _norm_mod(x_ref[...], g_ref[...], sc_ref[...], sh_ref[...]).astype(BF16)

    o_ref[...] = _dot(h_scr[...], w_ref[...])


def _inproj(x, g, scale_t, shift_t, w, mod_idx, tm):
    m, d = x.shape
    n = w.shape[1]
    tn = 512
    return pl.pallas_call(
        _inproj_kernel,
        grid=(m // tm, n // tn),
        in_specs=[pl.BlockSpec((tm, d), lambda i, j: (i, 0)),
                  pl.BlockSpec((1, d), lambda i, j: (0, 0)),
                  pl.BlockSpec((MOD_ROWS, d), lambda i, j: (mod_idx(i), 0)),
                  pl.BlockSpec((MOD_ROWS, d), lambda i, j: (mod_idx(i), 0)),
                  pl.BlockSpec((d, tn), lambda i, j: (0, j))],
        out_specs=pl.BlockSpec((tm, tn), lambda i, j: (i, j)),
        out_shape=jax.ShapeDtypeStruct((m, n), F32),
        scratch_shapes=[pltpu.VMEM((tm, d), BF16)],
        compiler_params=_params(("parallel", "arbitrary")),
        name="inproj",
    )(x, g, scale_t, shift_t, w)


def _softplus(x):
    return jnp.maximum(x, 0.0) + jnp.log1p(jnp.exp(-jnp.abs(x)))


RW_TT = 128
HALF = LANES // 2


def _rwkv_pre_math(z, zp, mu_ref, w0_ref, a0_ref, kkw_ref, ka_ref, rk_ref, wl_ref, ones_ref):
    zs = z + (zp - z) * mu_ref[...]
    r, k, v = zs[:, :RW], zs[:, RW:2 * RW], zs[:, 2 * RW:3 * RW]
    l0 = zs[:, 3 * RW:3 * RW + LANES]
    lane = lax.broadcasted_iota(jnp.int32, l0.shape, 1)
    l0 = jnp.where(lane < R_DECAY, jnp.tanh(l0), l0)
    l1 = jax.nn.sigmoid(zs[:, 3 * RW + LANES:])
    lin = jnp.concatenate([l0, l1], axis=1).astype(BF16)
    lo = _dot(lin, wl_ref[...])
    w_log = -_softplus(-(w0_ref[...] + lo[:, :RW])) - 0.5
    a = jax.nn.sigmoid(a0_ref[...] + lo[:, RW:2 * RW])
    ones_bd = ones_ref[...]
    kk = k * kkw_ref[...]
    kkn = kk / jnp.maximum(jnp.sqrt(_segsum(kk * kk, ones_bd)), 1e-12)
    kh = k * (1.0 + (a - 1.0) * ka_ref[...])
    bonus = _segsum(r * kh * rk_ref[...], ones_bd) * v
    return r, jnp.exp(-jnp.exp(w_log)), kh, v, kkn, kkn * a, lo[:, 2 * RW:], bonus


def _rwkv_pre_prompt_kernel(z_ref, prev_ref, mu_ref, w0_ref, a0_ref, kkw_ref, ka_ref, rk_ref, wl_ref,
                            ones_ref, r_o, w_o, k_o, v_o, kk_o, b_o, g_o, bon_o, zt_scr):
    bi = pl.program_id(1)
    z = z_ref[...]
    row = lax.broadcasted_iota(jnp.int32, z.shape, 0)
    zp = jnp.where(row == 0, prev_ref[0], pltpu.roll(z, 1, axis=0))
    r, w, kh, v, kkn, b, g, bonus = _rwkv_pre_math(z, zp, mu_ref, w0_ref, a0_ref, kkw_ref, ka_ref,
                                                   rk_ref, wl_ref, ones_ref)
    g_o[...] = g
    bon_o[...] = bonus
    for n, x in enumerate((r, w, kh, v, kkn, b)):
        for p in range(RW // LANES):
            row0 = pl.multiple_of((bi * (RW // LANES) + p) * LANES, LANES)
            zt_scr[n, pl.ds(row0, LANES), :] = x[:, p * LANES:(p + 1) * LANES].T

    @pl.when(bi == pl.num_programs(1) - 1)
    def _():
        for n, o_ref in enumerate((r_o, w_o, k_o, v_o, kk_o, b_o)):
            paired = o_ref is v_o

            def channel(ch, c, n=n, o_ref=o_ref, paired=paired):
                if paired:
                    lo_half = zt_scr[n, pl.ds(2 * ch, HALF, stride=HEAD), :]
                    hi_half = zt_scr[n, pl.ds(2 * ch + 1, HALF, stride=HEAD), :]
                else:
                    lo_half = hi_half = zt_scr[n, pl.ds(ch, HALF, stride=HEAD), :]
                rows = HEAD // 2 if paired else HEAD
                o_ref[pl.ds(ch, RW_TT, stride=rows), :] = jnp.concatenate([lo_half, hi_half], axis=0).T
                return c

            lax.fori_loop(0, HEAD // 2 if paired else HEAD, channel, 0)


def _rwkv_pre_prompt(z_main, prev, n_b, n_t, mu, w0, a0, kkw, ka, rk, wl, ones_bd):
    nt = n_t // RW_TT
    vec = lambda w: pl.BlockSpec((1, w), lambda i, b: (0, 0))
    keyed = pl.BlockSpec((RW_TT * HEAD, LANES), lambda i, b: (i, 0), pipeline_mode=pl.Buffered(1))
    paired = pl.BlockSpec((RW_TT * (HEAD // 2), LANES), lambda i, b: (i, 0), pipeline_mode=pl.Buffered(1))
    keyed_shape = jax.ShapeDtypeStruct((n_t * HEAD, LANES), F32)
    paired_shape = jax.ShapeDtypeStruct((n_t * (HEAD // 2), LANES), F32)
    nat = pl.BlockSpec((RW_TT, RW), lambda i, b: (b * nt + i, 0))
    return pl.pallas_call(
        _rwkv_pre_prompt_kernel,
        grid=(nt, n_b),
        in_specs=[pl.BlockSpec((RW_TT, RW_PAD), lambda i, b: (b * nt + i, 0)),
                  pl.BlockSpec((1, 1, RW_PAD), lambda i, b: (b * nt + i, 0, 0)),
                  vec(RW_PAD), vec(RW), vec(RW), vec(RW), vec(RW), vec(RW),
                  pl.BlockSpec((LORA_IN, 3 * RW), lambda i, b: (0, 0)),
                  pl.BlockSpec((RW, RW), lambda i, b: (0, 0))],
        out_specs=[keyed] * 3 + [paired] + [keyed] * 2 + [nat] * 2,
        out_shape=[keyed_shape] * 3 + [paired_shape] + [keyed_shape] * 2
        + [jax.ShapeDtypeStruct((n_b * n_t, RW), F32)] * 2,
        scratch_shapes=[pltpu.VMEM((6, n_b * RW, RW_TT), F32)],
        compiler_params=_params(("arbitrary", "arbitrary")),
        name="rwkv_pre_prompt",
    )(z_main, prev, mu, w0, a0, kkw, ka, rk, wl, ones_bd)


def _rwkv_pre_sample_kernel(z_ref, zp_ref, st_ref, mu_ref, w0_ref, a0_ref, kkw_ref, ka_ref, rk_ref, wl_ref,
                            ones_ref, r_o, w_o, k_o, v_o, kk_o, b_o, g_o, bon_o):
    zp = jnp.where(pl.program_id(0) == 0, st_ref[...], zp_ref[...])
    r, w, kh, v, kkn, b, g, bonus = _rwkv_pre_math(z_ref[...], zp, mu_ref, w0_ref, a0_ref, kkw_ref, ka_ref,
                                                   rk_ref, wl_ref, ones_ref)
    g_o[...] = g
    bon_o[...] = bonus
    for x, o_ref in zip((r, w, kh, v, kkn, b), (r_o, w_o, k_o, v_o, kk_o, b_o)):
        for p in range(RW // LANES):
            xt = x[:, p * LANES:(p + 1) * LANES].T
            o_ref[0, :, 2 * p * LANES:(2 * p + 1) * LANES] = xt[:HEAD]
            o_ref[0, :, (2 * p + 1) * LANES:(2 * p + 2) * LANES] = xt[HEAD:]


def _rwkv_pre_sample(z_main, state, row0, n_b, n_t, mu, w0, a0, kkw, ka, rk, wl, ones_bd):
    blk0 = row0 // n_b
    vec = lambda w: pl.BlockSpec((1, w), lambda t: (0, 0))
    keyed = pl.BlockSpec((1, HEAD, HEADS * n_b), lambda t: (t, 0, 0))
    nat = pl.BlockSpec((n_b, RW), lambda t: (t, 0))
    return pl.pallas_call(
        _rwkv_pre_sample_kernel,
        grid=(n_t,),
        in_specs=[pl.BlockSpec((n_b, RW_PAD), lambda t: (blk0 + t, 0)),
                  pl.BlockSpec((n_b, RW_PAD), lambda t: (blk0 + jnp.maximum(t - 1, 0), 0)),
                  pl.BlockSpec((n_b, RW_PAD), lambda t: (0, 0)),
                  vec(RW_PAD), vec(RW), vec(RW), vec(RW), vec(RW), vec(RW),
                  pl.BlockSpec((LORA_IN, 3 * RW), lambda t: (0, 0)),
                  pl.BlockSpec((RW, RW), lambda t: (0, 0))],
        out_specs=[keyed] * 6 + [nat] * 2,
        out_shape=[jax.ShapeDtypeStruct((n_t, HEAD, HEADS * n_b), F32)] * 6
        + [jax.ShapeDtypeStruct((n_t * n_b, RW), F32)] * 2,
        compiler_params=_params(("arbitrary",)),
        name="rwkv_pre_sample",
    )(z_main, z_main, state, mu, w0, a0, kkw, ka, rk, wl, ones_bd)


def _rwkv_scan_kernel(nat_state, r_ref, w_ref, k_ref, kk_ref, b_ref, v_ref, s0_ref, y_ref, sT_ref, s_scr):
    tc = pl.program_id(1)
    n_t = r_ref.shape[0]
    n_i = s_scr.shape[0]

    @pl.when(tc == 0)
    def _():
        if nat_state:
            s_scr[...] = s0_ref[...].T.reshape(s_scr.shape)
        else:
            s_scr[...] = s0_ref[...]

    def step(t, carry):
        r_t, w_t, k_t, kk_t, b_t = r_ref[t], w_ref[t], k_ref[t], kk_ref[t], b_ref[t]

        def rows8(ib, c2):
            i0 = pl.multiple_of(ib * SUBLANES, SUBLANES)
            v8 = v_ref[t, pl.ds(i0, SUBLANES), :]
            ys = []
            for ii in range(SUBLANES):
                s_old = s_scr[i0 + ii]
                sa = jnp.sum(s_old * kk_t, axis=0, keepdims=True)
                s_new = s_old * w_t - sa * b_t + v8[ii:ii + 1] * k_t
                ys.append(jnp.sum(s_new * r_t, axis=0, keepdims=True))
                s_scr[i0 + ii] = s_new
            y_ref[t, pl.ds(i0, SUBLANES), :] = jnp.concatenate(ys, axis=0)
            return c2

        return lax.fori_loop(0, n_i // SUBLANES, rows8, carry)

    lax.fori_loop(0, n_t, step, 0)

    @pl.when(tc == pl.num_programs(1) - 1)
    def _():
        if nat_state:
            sT_ref[...] = s_scr[...].reshape(n_i * s_scr.shape[1], LANES).T
        else:
            sT_ref[...] = s_scr[...]


def _rwkv_scan_prompt(r, w, k, kk, b, v, s0, tchunk):
    n_t = r.shape[0]
    n_i = s0.shape[0]
    op = pl.BlockSpec((tchunk, HEAD, LANES), lambda l, t: (t, 0, 0))
    vy = pl.BlockSpec((tchunk, n_i, LANES), lambda l, t: (t, 0, 0))
    st = pl.BlockSpec((n_i, HEAD, LANES), lambda l, t: (0, 0, 0))
    return pl.pallas_call(
        functools.partial(_rwkv_scan_kernel, False),
        grid=(1, n_t // tchunk),
        in_specs=[op] * 5 + [vy, st],
        out_specs=[vy, st],
        out_shape=[jax.ShapeDtypeStruct((n_t, n_i, LANES), F32),
                   jax.ShapeDtypeStruct((n_i, HEAD, LANES), F32)],
        scratch_shapes=[pltpu.VMEM((n_i, HEAD, LANES), F32)],
        compiler_params=_params(("arbitrary", "arbitrary")),
        name="rwkv_scan_prompt",
    )(r, w, k, kk, b, v, s0)


def _rwkv_scan_sample(r, w, k, kk, b, v, s0, layer):
    n_t, _, n_l = r.shape
    n_b = n_l // HEADS
    op = pl.BlockSpec((n_t, HEAD, LANES), lambda h, t: (0, 0, h))
    st = pl.BlockSpec((n_b, HEAD * HEAD), lambda h, t: (layer, h))
    st_out = pl.BlockSpec((n_b, HEAD * HEAD), lambda h, t: (0, h))
    return pl.pallas_call(
        functools.partial(_rwkv_scan_kernel, True),
        grid=(HEADS, 1),
        in_specs=[op] * 6 + [st],
        out_specs=[op, st_out],
        out_shape=[jax.ShapeDtypeStruct((n_t, HEAD, n_l), F32),
                   jax.ShapeDtypeStruct((n_b, HEADS * HEAD * HEAD), F32)],
        scratch_shapes=[pltpu.VMEM((HEAD, HEAD, LANES), F32)],
        compiler_params=_params(("parallel", "arbitrary")),
        name="rwkv_scan_sample",
    )(r, w, k, kk, b, v, s0)


def _rwkv_post_math(y, bon_ref, g_ref, lg_ref, lb_ref, ones_ref):
    ones_bd = ones_ref[...]
    yc = y - _segsum(y, ones_bd) * (1.0 / HEAD)
    var = _segsum(yc * yc, ones_bd) * (1.0 / HEAD)
    yn = yc * lax.rsqrt(var + GN_EPS) * lg_ref[...] + lb_ref[...]
    return ((yn + bon_ref[...]) * g_ref[...]).astype(BF16)


def _rwkv_post_prompt_kernel(y_ref, bon_ref, g_ref, lg_ref, lb_ref, ones_ref, o_ref, zt_scr):
    bi = pl.program_id(1)

    @pl.when(bi == 0)
    def _():
        def pair(ch, c):
            mt = y_ref[pl.ds(ch, RW_TT, stride=HEAD // 2), :].T
            zt_scr[pl.ds(2 * ch, HALF, stride=HEAD), :] = mt[:HALF]
            zt_scr[pl.ds(2 * ch + 1, HALF, stride=HEAD), :] = mt[HALF:]
            return c

        lax.fori_loop(0, HEAD // 2, pair, 0)

    pieces = []
    for p in range(RW // LANES):
        row0 = pl.multiple_of((bi * (RW // LANES) + p) * LANES, LANES)
        pieces.append(zt_scr[pl.ds(row0, LANES), :].T)
    y = jnp.concatenate(pieces, axis=1)
    o_ref[...] = _rwkv_post_math(y, bon_ref, g_ref, lg_ref, lb_ref, ones_ref)


def _rwkv_post_prompt(y, bonus, g, n_b, n_t, lnx_g, lnx_b, ones_bd):
    nt = n_t // RW_TT
    nat = pl.BlockSpec((RW_TT, RW), lambda i, b: (b * nt + i, 0))
    vec = pl.BlockSpec((1, RW), lambda i, b: (0, 0))
    return pl.pallas_call(
        _rwkv_post_prompt_kernel,
        grid=(nt, n_b),
        in_specs=[pl.BlockSpec((RW_TT * (HEAD // 2), LANES), lambda i, b: (i, 0)), nat, nat, vec, vec,
                  pl.BlockSpec((RW, RW), lambda i, b: (0, 0))],
        out_specs=nat,
        out_shape=jax.ShapeDtypeStruct((n_b * n_t, RW), BF16),
        scratch_shapes=[pltpu.VMEM((n_b * RW, RW_TT), F32)],
        compiler_params=_params(("arbitrary", "arbitrary")),
        name="rwkv_post_prompt",
    )(y, bonus, g, lnx_g, lnx_b, ones_bd)


def _rwkv_post_sample_kernel(y_ref, bon_ref, g_ref, lg_ref, lb_ref, ones_ref, o_ref):
    pieces = []
    for p in range(RW // LANES):
        m = jnp.concatenate([y_ref[0, :, 2 * p * LANES:(2 * p + 1) * LANES],
                             y_ref[0, :, (2 * p + 1) * LANES:(2 * p + 2) * LANES]], axis=0)
        pieces.append(m.T)
    y = jnp.concatenate(pieces, axis=1)
    o_ref[...] = _rwkv_post_math(y, bon_ref, g_ref, lg_ref, lb_ref, ones_ref)


def _rwkv_post_sample(y, bonus, g, n_b, n_t, lnx_g, lnx_b, ones_bd):
    nat = pl.BlockSpec((n_b, RW), lambda t: (t, 0))
    vec = pl.BlockSpec((1, RW), lambda t: (0, 0))
    return pl.pallas_call(
        _rwkv_post_sample_kernel,
        grid=(n_t,),
        in_specs=[pl.BlockSpec((1, HEAD, HEADS * n_b), lambda t: (t, 0, 0)), nat, nat, vec, vec,
                  pl.BlockSpec((RW, RW), lambda t: (0, 0))],
        out_specs=nat,
        out_shape=jax.ShapeDtypeStruct((n_t * n_b, RW), BF16),
        compiler_params=_params(("parallel",)),
        name="rwkv_post_sample",
    )(y, bonus, g, lnx_g, lnx_b, ones_bd)


def _pool_prompt_kernel(start_pos, z_ref, pw_ref, ps_ref, y_ref):
    n = z_ref.shape[0]
    row = lax.broadcasted_iota(jnp.int32, (n, POOL_G), 0)
    for gi, win in enumerate(WINDOWS):
        sl = slice(gi * POOL_G, (gi + 1) * POOL_G)
        x = z_ref[:, sl]
        s = x
        k = 1
        while k < win:
            s = s + jnp.where(row >= k, pltpu.roll(s, k, axis=0), 0.0)
            k *= 2
        cnt = jnp.minimum(win, row + (start_pos + 1)).astype(F32)
        d = s / cnt - x
        y_ref[:, sl] = (_dot(d.astype(BF16), pw_ref[gi]) * ps_ref[:, sl]).astype(BF16)


def _pool_prompt(z_main, n_b, n_t, pw, ps):
    cb = RW_PAD // POOL_W
    return pl.pallas_call(
        functools.partial(_pool_prompt_kernel, 0),
        grid=(n_b,),
        in_specs=[pl.BlockSpec((n_t, POOL_W), lambda b: (b, cb)),
                  pl.BlockSpec((len(WINDOWS), POOL_G, POOL_G), lambda b: (0, 0, 0)),
                  pl.BlockSpec((1, POOL_W), lambda b: (0, 0))],
        out_specs=pl.BlockSpec((n_t, POOL_W), lambda b: (b, 0)),
        out_shape=jax.ShapeDtypeStruct((n_b * n_t, POOL_W), BF16),
        compiler_params=_params(("parallel",)),
        name="pool_prompt",
    )(z_main, pw, ps)


def _pool_sample_kernel(start_pos, n_t, full_ref, pw_ref, ps_ref, y_ref):
    n_b = full_ref.shape[1]
    for gi, win in enumerate(WINDOWS):
        sl = slice(gi * POOL_G, (gi + 1) * POOL_G)
        f = full_ref[:, :, sl]
        s = f
        k = 1
        while k < win:
            s = s[k:] + s[:-k]
            k *= 2
        s = s[s.shape[0] - n_t:]
        x = f[POOL_BUF:]
        pos = start_pos + lax.broadcasted_iota(jnp.int32, s.shape, 0)
        cnt = jnp.minimum(win, pos + 1).astype(F32)
        d = (s / cnt - x).reshape(n_t * n_b, POOL_G)
        y_ref[:, sl] = (_dot(d.astype(BF16), pw_ref[gi]) * ps_ref[:, sl]).astype(BF16)


def _pool_sample(full, n_t, start_pos, pw, ps):
    n_f, n_b, _ = full.shape
    return pl.pallas_call(
        functools.partial(_pool_sample_kernel, start_pos, n_t),
        grid=(1,),
        in_specs=[pl.BlockSpec((n_f, n_b, POOL_W), lambda i: (0, 0, 0)),
                  pl.BlockSpec((len(WINDOWS), POOL_G, POOL_G), lambda i: (0, 0, 0)),
                  pl.BlockSpec((1, POOL_W), lambda i: (0, 0))],
        out_specs=pl.BlockSpec((n_t * n_b, POOL_W), lambda i: (0, 0)),
        out_shape=jax.ShapeDtypeStruct((n_t * n_b, POOL_W), BF16),
        compiler_params=_params(("arbitrary",)),
        name="pool_sample",
    )(full, pw, ps)


def _s5_tail(x_all, u, cmat_ref, d_ref, gw_ref, gb_ref):
    y = _dot(x_all.astype(BF16), cmat_ref[...]) + d_ref[...] * u
    y = jax.nn.gelu(y)
    return (y * jax.nn.sigmoid(_dot(y.astype(BF16), gw_ref[...]) + gb_ref[...])).astype(BF16)


S5_LC = 512


def _s5_prompt_kernel(u_ref, bmat_ref, cmat_ref, lam_ref, pw_ref, d_ref, gw_ref, gb_ref,
                      y_ref, xre_o, xim_o, x_scr, c_scr):
    tc = pl.program_id(1)

    @pl.when(tc == 0)
    def _():
        c_scr[...] = jnp.zeros_like(c_scr)

    u = u_ref[...]
    n = u.shape[0]
    x_scr[...] = _dot(u.astype(BF16), bmat_ref[...])
    row = lax.broadcasted_iota(jnp.int32, (SUBLANES, S5_LC), 0)
    for lc in range(S5_N // S5_LC):
        re_sl = pl.ds(lc * S5_LC, S5_LC)
        im_sl = pl.ds(S5_N + lc * S5_LC, S5_LC)
        lam = [(lam_ref[2 * j:2 * j + 1, re_sl], lam_ref[2 * j + 1:2 * j + 2, re_sl]) for j in range(3)]
        p_re, p_im = pw_ref[:, re_sl], pw_ref[:, im_sl]

        def blk(rb, carry, re_sl=re_sl, im_sl=im_sl, lam=lam, p_re=p_re, p_im=p_im):
            c_re, c_im = carry
            rows = pl.ds(pl.multiple_of(rb * SUBLANES, SUBLANES), SUBLANES)
            xr, xi = x_scr[rows, re_sl], x_scr[rows, im_sl]
            for j, (l_re, l_im) in enumerate(lam):
                kshift = 1 << j
                sr = jnp.where(row >= kshift, pltpu.roll(xr, kshift, axis=0), 0.0)
                si = jnp.where(row >= kshift, pltpu.roll(xi, kshift, axis=0), 0.0)
                xr, xi = xr + (l_re * sr - l_im * si), xi + (l_re * si + l_im * sr)
            xr, xi = xr + (p_re * c_re - p_im * c_im), xi + (p_re * c_im + p_im * c_re)
            x_scr[rows, re_sl] = xr
            x_scr[rows, im_sl] = xi
            return xr[SUBLANES - 1:], xi[SUBLANES - 1:]

        c_re, c_im = lax.fori_loop(0, n // SUBLANES, blk, (c_scr[0:1, re_sl], c_scr[0:1, im_sl]))
        c_scr[0:1, re_sl] = c_re
        c_scr[0:1, im_sl] = c_im

    y_ref[...] = _s5_tail(x_scr[...], u, cmat_ref, d_ref, gw_ref, gb_ref)

    @pl.when(tc == pl.num_programs(1) - 1)
    def _():
        xre_o[0] = c_scr[0:1, :S5_N]
        xim_o[0] = c_scr[0:1, S5_N:]


def _s5_prompt(z_main, n_b, n_t, tt, bmat, cmat, lam_t, pw_t, dskip, gw, gb):
    cb = (RW_PAD + POOL_W) // S5_W
    n_tc = n_t // tt
    const = lambda shape: pl.BlockSpec(shape, lambda b, t: (0,) * len(shape))
    return pl.pallas_call(
        _s5_prompt_kernel,
        grid=(n_b, n_tc),
        in_specs=[pl.BlockSpec((tt, S5_W), lambda b, t: (b * n_tc + t, cb)),
                  const((S5_W, 2 * S5_N)), const((2 * S5_N, S5_W)),
                  const((SUBLANES, S5_N)), const((SUBLANES, 2 * S5_N)),
                  const((1, S5_W)), const((S5_W, S5_W)), const((1, S5_W))],
        out_specs=[pl.BlockSpec((tt, S5_W), lambda b, t: (b * n_tc + t, 0)),
                   pl.BlockSpec((1, 1, S5_N), lambda b, t: (b, 0, 0)),
                   pl.BlockSpec((1, 1, S5_N), lambda b, t: (b, 0, 0))],
        out_shape=[jax.ShapeDtypeStruct((n_b * n_t, S5_W), BF16),
                   jax.ShapeDtypeStruct((n_b, 1, S5_N), F32),
                   jax.ShapeDtypeStruct((n_b, 1, S5_N), F32)],
        scratch_shapes=[pltpu.VMEM((tt, 2 * S5_N), F32), pltpu.VMEM((SUBLANES, 2 * S5_N), F32)],
        compiler_params=_params(("parallel", "arbitrary")),
        name="s5_prompt",
    )(z_main, bmat, cmat, lam_t, pw_t, dskip, gw, gb)


def _s5_sample_kernel(n_t, u_ref, x0re_ref, x0im_ref, bmat_ref, cmat_ref, lam_ref, d_ref, gw_ref,
                      gb_ref, y_ref, xre_o, xim_o, x_scr):
    u = u_ref[...]
    n_b = u.shape[0] // n_t
    x_scr[...] = _dot(u.astype(BF16), bmat_ref[...])
    for lc in range(S5_N // S5_LC):
        re_sl = pl.ds(lc * S5_LC, S5_LC)
        im_sl = pl.ds(S5_N + lc * S5_LC, S5_LC)
        l_re, l_im = lam_ref[0:1, re_sl], lam_ref[1:2, re_sl]
        xr, xi = x0re_ref[:, re_sl], x0im_ref[:, re_sl]
        for t in range(n_t):
            rows = pl.ds(t * n_b, n_b)
            xr, xi = (l_re * xr - l_im * xi + x_scr[rows, re_sl],
                      l_re * xi + l_im * xr + x_scr[rows, im_sl])
            x_scr[rows, re_sl] = xr
            x_scr[rows, im_sl] = xi
        xre_o[:, re_sl] = xr
        xim_o[:, re_sl] = xi
    y_ref[...] = _s5_tail(x_scr[...], u, cmat_ref, d_ref, gw_ref, gb_ref)


def _s5_sample(z_main, row0, n_b, n_t, x0re, x0im, bmat, cmat, lam_t, dskip, gw, gb):
    rows = n_b * n_t
    cb = (RW_PAD + POOL_W) // S5_W
    const = lambda shape: pl.BlockSpec(shape, lambda i: (0,) * len(shape))
    return pl.pallas_call(
        functools.partial(_s5_sample_kernel, n_t),
        grid=(1,),
        in_specs=[pl.BlockSpec((rows, S5_W), lambda i: (row0 // rows, cb)),
                  const((n_b, S5_N)), const((n_b, S5_N)),
                  const((S5_W, 2 * S5_N)), const((2 * S5_N, S5_W)), const((SUBLANES, S5_N)),
                  const((1, S5_W)), const((S5_W, S5_W)), const((1, S5_W))],
        out_specs=[const((rows, S5_W)), const((n_b, S5_N)), const((n_b, S5_N))],
        out_shape=[jax.ShapeDtypeStruct((rows, S5_W), BF16),
                   jax.ShapeDtypeStruct((n_b, S5_N), F32),
                   jax.ShapeDtypeStruct((n_b, S5_N), F32)],
        scratch_shapes=[pltpu.VMEM((rows, 2 * S5_N), F32)],
        compiler_params=_params(("arbitrary",)),
        name="s5_sample",
    )(z_main, x0re, x0im, bmat, cmat, lam_t, dskip, gw, gb)


def _merge_kernel(ya_ref, yb_ref, yc_ref, zg_ref, wbr_ref, wout_ref, x_ref, g1_ref, n2_ref,
                  sc2_ref, sh2_ref, rw_ref, rb_ref, xo_ref, h2_ref, lg_ref):
    d = x_ref.shape[1]
    rows = x_ref.shape[0]
    m = jax.nn.sigmoid(zg_ref[:, :d]) * _dot(ya_ref[...], wbr_ref[:RW, :])
    m = m + jax.nn.sigmoid(zg_ref[:, d:2 * d]) * _dot(yb_ref[...], wbr_ref[RW:RW + POOL_W, :])
    m = m + jax.nn.sigmoid(zg_ref[:, 2 * d:]) * _dot(yc_ref[...], wbr_ref[RW + POOL_W:, :])
    xn = x_ref[...] + _tile_rows(g1_ref[...], rows) * _dot(m.astype(BF16), wout_ref[...])
    xo_ref[...] = xn
    h2 = _norm_mod(xn, n2_ref[...], sc2_ref[...], sh2_ref[...])
    h2_ref[...] = h2
    lg_ref[...] = jnp.dot(h2, rw_ref[...], precision=lax.Precision.HIGHEST,
                          preferred_element_type=F32) + rb_ref[...]


def _merge(ya, yb, yc, zg, wbr, wout, x, g1_t, n2g, sc2_t, sh2_t, rw, rb, mod_idx, tm):
    m, d = x.shape
    row = lambda w: pl.BlockSpec((tm, w), lambda i: (i, 0))
    const = lambda shape: pl.BlockSpec(shape, lambda i: (0,) * len(shape), pipeline_mode=pl.Buffered(1))
    mod = pl.BlockSpec((MOD_ROWS, d), lambda i: (mod_idx(i), 0))
    return pl.pallas_call(
        _merge_kernel,
        grid=(m // tm,),
        in_specs=[row(RW), row(POOL_W), row(S5_W), row(3 * d), const(wbr.shape), const(wout.shape),
                  row(d), mod, const((1, d)), mod, mod, const(rw.shape), const((1, LANES))],
        out_specs=[row(d), row(d), row(LANES)],
        out_shape=[jax.ShapeDtypeStruct((m, d), F32), jax.ShapeDtypeStruct((m, d), F32),
                   jax.ShapeDtypeStruct((m, LANES), F32)],
        compiler_params=_params(("parallel",)),
        name="merge",
    )(ya, yb, yc, zg, wbr, wout, x, g1_t, n2g, sc2_t, sh2_t, rw, rb)


MOE_BM = 512
MOE_TF = 512


def _moe_kernel(be_ref, nb_ref, x_ref, wg_ref, wu_ref, bg_ref, bu_ref, wd_ref, bd_ref, o_ref):
    s, f = pl.program_id(0), pl.program_id(1)
    used = s < nb_ref[0]

    @pl.when(jnp.logical_and(jnp.logical_not(used), f == 0))
    def _():
        o_ref[...] = jnp.zeros_like(o_ref)

    @pl.when(used)
    def _():
        x = x_ref[...].astype(BF16)
        g = _dot(x, wg_ref[0].astype(BF16)) + bg_ref[0]
        u = _dot(x, wu_ref[0].astype(BF16)) + bu_ref[0]
        g = jnp.minimum(g, SW_LIMIT)
        u = jnp.clip(u, -SW_LIMIT, SW_LIMIT)
        act = (u + 1.0) * (g * jax.nn.sigmoid(SW_ALPHA * g))
        part = _dot(act.astype(BF16), wd_ref[0].astype(BF16))

        @pl.when(f == 0)
        def _():
            o_ref[...] = part + bd_ref[0]

        @pl.when(f > 0)
        def _():
            o_ref[...] += part


def _moe(xs, blk_e, n_used, layer, w_gu, b_gu, w_down, b_down):
    rows, d = xs.shape
    n_l, n_e, _, two_ff = w_gu.shape
    dff = two_ff // 2
    nf = dff // MOE_TF
    n_blk = rows // MOE_BM
    e0 = layer * n_e

    def f_eff(s, f, nb):
        return jnp.where(s < nb[0], f, nf - 1)

    grid_spec = pltpu.PrefetchScalarGridSpec(
        num_scalar_prefetch=2,
        grid=(n_blk, nf),
        in_specs=[
            pl.BlockSpec((MOE_BM, d), lambda s, f, be, nb: (jnp.minimum(s, nb[0] - 1), 0)),
            pl.BlockSpec((1, d, MOE_TF), lambda s, f, be, nb: (e0 + be[s], 0, f_eff(s, f, nb))),
            pl.BlockSpec((1, d, MOE_TF), lambda s, f, be, nb: (e0 + be[s], 0, nf + f_eff(s, f, nb))),
            pl.BlockSpec((1, 1, MOE_TF), lambda s, f, be, nb: (e0 + be[s], 0, f_eff(s, f, nb))),
            pl.BlockSpec((1, 1, MOE_TF), lambda s, f, be, nb: (e0 + be[s], 0, nf + f_eff(s, f, nb))),
            pl.BlockSpec((1, MOE_TF, d), lambda s, f, be, nb: (e0 + be[s], f_eff(s, f, nb), 0)),
            pl.BlockSpec((1, 1, d), lambda s, f, be, nb: (e0 + be[s], 0, 0)),
        ],
        out_specs=pl.BlockSpec((MOE_BM, d), lambda s, f, be, nb: (s, 0)),
    )
    w_gu = w_gu.reshape(n_l * n_e, d, two_ff)
    b_gu = b_gu.reshape(n_l * n_e, 1, two_ff)
    return pl.pallas_call(
        _moe_kernel,
        grid_spec=grid_spec,
        out_shape=jax.ShapeDtypeStruct((rows, d), F32),
        compiler_params=_params(("arbitrary", "arbitrary")),
        name="moe",
    )(blk_e, n_used, xs, w_gu, w_gu, b_gu, b_gu, w_down.reshape(n_l * n_e, dff, d),
      b_down.reshape(n_l * n_e, 1, d))


def _route(logits, n_rows_pad):
    n_tok = logits.shape[0]
    top_v, top_e = lax.top_k(logits, TOP_K)
    gate = jax.nn.softmax(top_v, axis=-1)
    flat_e = top_e.reshape(-1)
    onehot = (flat_e[:, None] == jnp.arange(N_EXP, dtype=jnp.int32)[None, :]).astype(jnp.int32)
    csum = jnp.cumsum(onehot, axis=0)
    counts = csum[-1]
    rank = jnp.sum((csum - 1) * onehot, axis=1)
    padded = (counts + MOE_BM - 1) // MOE_BM * MOE_BM
    pad_end = jnp.cumsum(padded)
    pad_start = pad_end - padded
    dest = (pad_start[flat_e] + rank).astype(jnp.int32)
    flat_tok = jnp.arange(n_tok * TOP_K, dtype=jnp.int32) // TOP_K
    rows_tok = jnp.zeros((n_rows_pad,), jnp.int32).at[dest].set(flat_tok)
    n_rows_used = pad_end[-1].astype(jnp.int32)
    starts = jnp.arange(n_rows_pad // MOE_BM, dtype=jnp.int32) * MOE_BM
    starts = jnp.minimum(starts, n_rows_used - MOE_BM)
    blk_e = jnp.minimum(jnp.searchsorted(pad_end, starts, side='right'), N_EXP - 1).astype(jnp.int32)
    return gate, dest.reshape(n_tok, TOP_K), rows_tok, blk_e, n_rows_used.reshape(1)


def _combine_kernel(x_ref, y0_ref, y1_ref, y2_ref, y3_ref, gw_ref, g2_ref, o_ref):
    rows = x_ref.shape[0]
    gw = gw_ref[...]
    acc = gw[:, 0:1] * y0_ref[...]
    for k, y_ref in enumerate((y1_ref, y2_ref, y3_ref), start=1):
        acc = acc + gw[:, k:k + 1] * y_ref[...]
    o_ref[...] = x_ref[...] + _tile_rows(g2_ref[...], rows) * acc


def _combine(x, yg, gw, g2_t, mod_idx, tm):
    m, d = x.shape
    n_i = m // tm
    slab = lambda k: pl.BlockSpec((tm, d), lambda i: (k * n_i + i, 0))
    return pl.pallas_call(
        _combine_kernel,
        grid=(n_i,),
        in_specs=[pl.BlockSpec((tm, d), lambda i: (i, 0)), slab(0), slab(1), slab(2), slab(3),
                  pl.BlockSpec((tm, TOP_K), lambda i: (i, 0)),
                  pl.BlockSpec((MOD_ROWS, d), lambda i: (mod_idx(i), 0))],
        out_specs=pl.BlockSpec((tm, d), lambda i: (i, 0)),
        out_shape=jax.ShapeDtypeStruct((m, d), F32),
        compiler_params=_params(("parallel",)),
        name="combine",
    )(x, yg, yg, yg, yg, gw, g2_t)


def _final_norm_kernel(x_ref, g_ref, o_ref):
    x = x_ref[...]
    o_ref[...] = x * lax.rsqrt(jnp.mean(x * x, axis=-1, keepdims=True) + NORM_EPS) * g_ref[...]


def _final_norm(x, g, tm):
    m, d = x.shape
    return pl.pallas_call(
        _final_norm_kernel,
        grid=(m // tm,),
        in_specs=[pl.BlockSpec((tm, d), lambda i: (i, 0)), pl.BlockSpec((1, d), lambda i: (0, 0))],
        out_specs=pl.BlockSpec((tm, d), lambda i: (i, 0)),
        out_shape=jax.ShapeDtypeStruct((m, d), F32),
        compiler_params=_params(("parallel",)),
        name="final_norm",
    )(x, g)


def _s5_params(a_re, a_im, log_dt, b_re, b_im, c_re, c_im):
    dt = jnp.exp(log_dt)[:, None]
    mag = jnp.exp(a_re * dt)
    lb_re, lb_im = mag * jnp.cos(a_im * dt), mag * jnp.sin(a_im * dt)
    nr, ni = lb_re - 1.0, lb_im
    den = a_re * a_re + a_im * a_im
    f_re = (nr * a_re + ni * a_im) / den
    f_im = (ni * a_re - nr * a_im) / den
    bb_re = f_re[..., None] * b_re - f_im[..., None] * b_im
    bb_im = f_re[..., None] * b_im + f_im[..., None] * b_re
    eye = jnp.eye(S5_G, dtype=F32)
    bd_in = lambda w: jnp.einsum('gpc,gh->gchp', w, eye).reshape(S5_W, S5_N)
    bd_out = lambda w: jnp.einsum('gcp,gh->gphc', w, eye).reshape(S5_N, S5_W)
    bmat = jnp.concatenate([bd_in(bb_re), bd_in(bb_im)], axis=1).astype(BF16)
    cmat = jnp.concatenate([bd_out(c_re), -bd_out(c_im)], axis=0).astype(BF16)
    l_re, l_im = lb_re.reshape(1, S5_N), lb_im.reshape(1, S5_N)
    pows = [(l_re, l_im)]
    for _ in range(SUBLANES - 1):
        p_re, p_im = pows[-1]
        pows.append((p_re * l_re - p_im * l_im, p_re * l_im + p_im * l_re))
    lam_t = jnp.concatenate([pows[0][0], pows[0][1], pows[1][0], pows[1][1], pows[3][0], pows[3][1],
                             jnp.zeros((2, S5_N), F32)], axis=0)
    pw_t = jnp.concatenate([jnp.concatenate([p[0] for p in pows], axis=0),
                            jnp.concatenate([p[1] for p in pows], axis=0)], axis=1)
    return bmat, cmat, lam_t, pw_t


def _pad_cols(w, width):
    return jnp.pad(w, ((0, 0), (0, width - w.shape[1])))


def kernel(x_prompt, x_sample, c_prompt, c_sample, state_wkv, state_shift, state_pool, state_s5_re, state_s5_im, norm1_g, norm2_g, final_norm_g, w_ada, b_ada, w_in, rw_mu, rw_w0, rw_w2, rw_a0, rw_a2, rw_g2, rw_kk, rw_ka, rw_rk, rw_lnx_g, rw_lnx_b, pool_w, pool_scale, s5_a_re, s5_a_im, s5_log_dt, s5_b_re, s5_b_im, s5_c_re, s5_c_im, s5_d, s5_glu_w, s5_glu_b, w_br, w_out, router_w, router_b, moe_w_gu, moe_b_gu, moe_w_down, moe_b_down):
    bp, lp, d = x_prompt.shape
    bs, ls, _ = x_sample.shape
    depth = w_in.shape[0]
    mp, ms = bp * lp, bs * ls
    m = mp + ms
    past_len = 16384
    assert bs == MOD_ROWS and bp * HEADS * 2 == LANES and lp % 1024 == 0 and ms == 1024

    def mod_idx_for(tm):
        n_p = mp // tm
        return lambda i: jnp.where(i < n_p, (i * tm) // lp, bp)

    tm_big, tm_mid = 1024, 256
    ones_bd = jnp.kron(jnp.eye(HEADS, dtype=F32), jnp.ones((HEAD, HEAD), F32)).astype(BF16)

    x = jnp.concatenate([x_prompt.reshape(mp, d), jnp.swapaxes(x_sample, 0, 1).reshape(ms, d)], axis=0)
    c_all = jnp.concatenate([c_prompt, c_sample, jnp.zeros((4, d), F32)], axis=0)
    mod = _adaln(c_all, w_ada, b_ada)

    def table(layer, idx):
        v = mod[layer, :, idx * d:(idx + 1) * d]
        return jnp.concatenate([jnp.repeat(v[:bp], MOD_ROWS, axis=0), v[bp:bp + bs]], axis=0)

    outs_p, outs_s = [], []
    for l in range(depth):
        shift1, scale1, gate1, shift2, scale2, gate2 = (table(l, i) for i in range(6))

        w_l = w_in[l]
        w_main = jnp.concatenate([_pad_cols(w_l[:, :RW_PROJ], RW_PAD),
                                  w_l[:, RW_PROJ:RW_PROJ + POOL_W + S5_W]], axis=1).astype(BF16)
        w_gate = w_l[:, RW_PROJ + POOL_W + S5_W:].astype(BF16)
        n1 = norm1_g[l].reshape(1, d)
        z_main = _inproj(x, n1, scale1, shift1, w_main, mod_idx_for(tm_big), tm_big)
        z_gate = _inproj(x, n1, scale1, shift1, w_gate, mod_idx_for(tm_big), tm_big)

        mu = _pad_cols(rw_mu[l].reshape(1, RW_PROJ), RW_PAD)
        wl = jnp.zeros((LORA_IN, 3 * RW), F32)
        wl = wl.at[:R_DECAY, :RW].set(rw_w2[l])
        wl = wl.at[R_DECAY:R_DECAY + R_AAA, RW:2 * RW].set(rw_a2[l])
        wl = wl.at[R_DECAY + R_AAA:R_DECAY + R_AAA + R_GATE, 2 * RW:].set(rw_g2[l]).astype(BF16)
        vecs = [v.reshape(1, RW) for v in (rw_w0[l], rw_a0[l], rw_kk[l], rw_ka[l], rw_rk[l])]
        lnx_g, lnx_b = rw_lnx_g[l].reshape(1, RW), rw_lnx_b[l].reshape(1, RW)
        n_tp = mp // RW_TT
        last_rows = z_main[RW_TT - 1:mp:RW_TT, :RW_PAD]
        prev_p = jnp.concatenate([jnp.zeros((1, RW_PAD), F32), last_rows[:-1]], axis=0)
        first_of_seq = (jnp.arange(n_tp) % (lp // RW_TT) == 0)[:, None]
        prev_p = jnp.where(first_of_seq, 0.0, prev_p).reshape(n_tp, 1, RW_PAD)
        *ops_p, g_p, bon_p = _rwkv_pre_prompt(z_main, prev_p, bp, lp, mu, *vecs, wl, ones_bd)
        ops_p = [a.reshape(lp, a.shape[0] // lp, LANES) for a in ops_p]
        y_pt, st_p = _rwkv_scan_prompt(*ops_p[:3], *ops_p[4:], ops_p[3],
                                       jnp.zeros((HEAD // 2, HEAD, LANES), F32), 64)
        y_a_p = _rwkv_post_prompt(y_pt.reshape(lp * (HEAD // 2), LANES), bon_p, g_p, bp, lp,
                                  lnx_g, lnx_b, ones_bd)
        new_wkv_p = st_p.reshape(HEAD // 2, HEAD, 2, bp, HEADS).transpose(3, 4, 0, 2, 1)
        new_wkv_p = new_wkv_p.reshape(bp, HEADS, HEAD, HEAD)

        *ops_s, g_s, bon_s = _rwkv_pre_sample(z_main, _pad_cols(state_shift[l], RW_PAD), mp, bs, ls,
                                              mu, *vecs, wl, ones_bd)
        y_st, st_s = _rwkv_scan_sample(*ops_s[:3], *ops_s[4:], ops_s[3],
                                       state_wkv.reshape(depth * bs, HEADS * HEAD * HEAD), l)
        y_a_s = _rwkv_post_sample(y_st, bon_s, g_s, bs, ls, lnx_g, lnx_b, ones_bd)
        new_wkv_s = st_s.reshape(bs, HEADS, HEAD, HEAD)
        y_a = jnp.concatenate([y_a_p, y_a_s], axis=0)
        new_shift_p = z_main[lp - 1:mp:lp, :RW_PROJ]
        new_shift_s = z_main[m - bs:, :RW_PROJ]

        pw = pool_w[l].astype(BF16)
        ps = pool_scale[l].reshape(1, POOL_W)
        y_b_p = _pool_prompt(z_main, bp, lp, pw, ps)
        zb_s = z_main[mp:, RW_PAD:RW_PAD + POOL_W].reshape(ls, bs, POOL_W)
        full_s = jnp.concatenate([jnp.swapaxes(state_pool[l], 0, 1), zb_s], axis=0)
        y_b_s = _pool_sample(full_s, ls, past_len, pw, ps)
        y_b = jnp.concatenate([y_b_p, y_b_s], axis=0)
        new_pool_p = z_main[:mp, RW_PAD:RW_PAD + POOL_W].reshape(bp, lp, POOL_W)[:, lp - POOL_BUF:]
        new_pool_s = jnp.swapaxes(full_s[full_s.shape[0] - POOL_BUF:], 0, 1)

        bmat, cmat, lam_t, pw_t = _s5_params(s5_a_re[l], s5_a_im[l], s5_log_dt[l], s5_b_re[l], s5_b_im[l],
                                             s5_c_re[l], s5_c_im[l])
        dskip, gw, gb = s5_d[l].reshape(1, S5_W), s5_glu_w[l].astype(BF16), s5_glu_b[l].reshape(1, S5_W)
        y_c_p, re_p, im_p = _s5_prompt(z_main, bp, lp, 512, bmat, cmat, lam_t, pw_t, dskip, gw, gb)
        y_c_s, re_s, im_s = _s5_sample(z_main, mp, bs, ls, state_s5_re[l].reshape(bs, S5_N),
                                       state_s5_im[l].reshape(bs, S5_N), bmat, cmat, lam_t, dskip, gw, gb)
        y_c = jnp.concatenate([y_c_p, y_c_s], axis=0)

        rw_pad = _pad_cols(router_w[l], LANES)
        rb_pad = jnp.concatenate([router_b[l], jnp.full((LANES - N_EXP,), -1e30, F32)]).reshape(1, LANES)
        x, h2, logits = _merge(y_a, y_b, y_c, z_gate, w_br[l].astype(BF16), w_out[l].astype(BF16), x, gate1,
                               norm2_g[l].reshape(1, d), scale2, shift2, rw_pad, rb_pad,
                               mod_idx_for(tm_mid), tm_mid)

        n_rows_pad = (m * TOP_K // MOE_BM + N_EXP) * MOE_BM
        gate_w, dest, rows_tok, blk_e, n_rows_used = _route(logits[:, :N_EXP], n_rows_pad)
        yb = _moe(h2[rows_tok], blk_e, n_rows_used // MOE_BM, l, moe_w_gu, moe_b_gu, moe_w_down, moe_b_down)
        yg = yb[dest.T.reshape(-1)]
        x = _combine(x, yg, gate_w, gate2, mod_idx_for(tm_mid), tm_mid)

        outs_p.append((new_shift_p, new_wkv_p, new_pool_p, re_p.reshape(bp, S5_G, S5_P),
                       im_p.reshape(bp, S5_G, S5_P)))
        outs_s.append((new_shift_s, new_wkv_s, new_pool_s, re_s.reshape(bs, S5_G, S5_P),
                       im_s.reshape(bs, S5_G, S5_P)))

    y = _final_norm(x, final_norm_g.reshape(1, d), tm_mid)
    y_prompt = y[:mp].reshape(bp, lp, d)
    y_sample = jnp.swapaxes(y[mp:].reshape(ls, bs, d), 0, 1)
    p_shift, p_wkv, p_pool, p_re, p_im = (jnp.stack([o[j] for o in outs_p]) for j in range(5))
    s_shift, s_wkv, s_pool, s_re, s_im = (jnp.stack([o[j] for o in outs_s]) for j in range(5))
    return (y_prompt, y_sample, p_wkv, p_shift, p_pool, p_re, p_im,
            s_wkv, s_shift, s_pool, s_re, s_im)
```

```python
import functools

import jax
import jax.numpy as jnp
from jax import lax
from jax.experimental import pallas as pl
from jax.experimental.pallas import tpu as pltpu

F32, BF16 = jnp.float32, jnp.bfloat16

LANES = 128
SUBLANES = 8
VMEM_LIMIT = 56 * 1024 * 1024

HEAD = 64
HEADS = 16
RW = HEAD * HEADS
R_DECAY, R_AAA, R_GATE = 64, 64, 160
RW_PROJ = 3 * RW + R_DECAY + R_AAA + R_GATE
RW_PAD = 3584
LORA_IN = RW_PAD - 3 * RW
POOL_W = 512
WINDOWS = (2, 4, 8, 16)
POOL_G = POOL_W // len(WINDOWS)
POOL_BUF = max(WINDOWS) - 1
S5_W = 512
S5_G, S5_CH, S5_P = 32, 16, 64
S5_N = S5_G * S5_P
MAIN_W = RW_PAD + POOL_W + S5_W
N_EXP, TOP_K = 32, 4
SW_LIMIT, SW_ALPHA = 7.0, 1.702
NORM_EPS, GN_EPS = 1e-5, 64e-5
MOD_ROWS = 128


def _params(sem):
    return pltpu.CompilerParams(dimension_semantics=sem, vmem_limit_bytes=VMEM_LIMIT)


def _dot(a, b):
    return jnp.dot(a, b, preferred_element_type=F32)


def _segsum(x, ones_bd):
    hi = x.astype(BF16)
    lo = (x - hi.astype(F32)).astype(BF16)
    return _dot(hi, ones_bd) + _dot(lo, ones_bd)


def _sublane_transpose8(xs):
    xs = list(xs)
    sub = lax.broadcasted_iota(jnp.int32, xs[0].shape, 0)
    for d in (4, 2, 1):
        keep = (sub & d) == 0
        for k in range(SUBLANES):
            if k & d:
                continue
            lo, hi = xs[k], xs[k + d]
            xs[k] = jnp.where(keep, lo, pltpu.roll(hi, d, axis=0))
            xs[k + d] = jnp.where(keep, pltpu.roll(lo, SUBLANES - d, axis=0), hi)
    return xs


def _tile_rows(t, rows):
    return jnp.broadcast_to(t[None], (rows // MOD_ROWS,) + t.shape).reshape(rows, t.shape[-1])


def _ada_kernel(c_ref, w_ref, b_ref, o_ref):
    c = c_ref[...]
    a = (c * jax.nn.sigmoid(c)).astype(BF16)
    o_ref[0] = _dot(a, w_ref[0].astype(BF16)) + b_ref[0]


def _adaln(c_all, w_ada, b_ada):
    nl, d, n = w_ada.shape
    r = c_all.shape[0]
    tn = 1024
    return pl.pallas_call(
        _ada_kernel,
        grid=(nl, n // tn),
        in_specs=[pl.BlockSpec((r, d), lambda l, j: (0, 0)),
                  pl.BlockSpec((1, d, tn), lambda l, j: (l, 0, j)),
                  pl.BlockSpec((1, 1, tn), lambda l, j: (l, 0, j))],
        out_specs=pl.BlockSpec((1, r, tn), lambda l, j: (l, 0, j)),
        out_shape=jax.ShapeDtypeStruct((nl, r, n), F32),
        compiler_params=_params(("parallel", "parallel")),
        name="adaln",
    )(c_all, w_ada, b_ada.reshape(nl, 1, n))


def _norm_mod(x, g, scale, shift):
    y = x * lax.rsqrt(jnp.mean(x * x, axis=-1, keepdims=True) + NORM_EPS) * g
    rows = x.shape[0]
    return y * (1.0 + _tile_rows(scale, rows)) + _tile_rows(shift, rows)


def _inproj_kernel(x_ref, g_ref, sc_ref, sh_ref, w_ref, o_ref, h_scr):
    @pl.when(pl.program_id(1) == 0)
    def _():
        h_scr[...] = _norm_mod(x_ref[...], g_ref[...], sc_ref[...], sh_ref[...]).astype(BF16)

    o_ref[...] = _dot(h_scr[...], w_ref[...])


def _inproj(x, g, scale_t, shift_t, w, mod_idx, tm):
    m, d = x.shape
    n = w.shape[1]
    tn = 512
    return pl.pallas_call(
        _inproj_kernel,
        grid=(m // tm, n // tn),
        in_specs=[pl.BlockSpec((tm, d), lambda i, j: (i, 0)),
                  pl.BlockSpec((1, d), lambda i, j: (0, 0)),
                  pl.BlockSpec((MOD_ROWS, d), lambda i, j: (mod_idx(i), 0)),
                  pl.BlockSpec((MOD_ROWS, d), lambda i, j: (mod_idx(i), 0)),
                  pl.BlockSpec((d, tn), lambda i, j: (0, j))],
        out_specs=pl.BlockSpec((tm, tn), lambda i, j: (i, j)),
        out_shape=jax.ShapeDtypeStruct((m, n), F32),
        scratch_shapes=[pltpu.VMEM((tm, d), BF16)],
        compiler_params=_params(("parallel", "arbitrary")),
        name="inproj",
    )(x, g, scale_t, shift_t, w)


def _softplus(x):
    return jnp.maximum(x, 0.0) + jnp.log1p(jnp.exp(-jnp.abs(x)))


RW_TT = 128
HALF = LANES // 2


def _rwkv_pre_math(z, zp, mu_ref, w0_ref, a0_ref, kkw_ref, ka_ref, rk_ref, wl_ref, ones_ref):
    zs = z + (zp - z) * mu_ref[...]
    r, k, v = zs[:, :RW], zs[:, RW:2 * RW], zs[:, 2 * RW:3 * RW]
    l0 = zs[:, 3 * RW:3 * RW + LANES]
    lane = lax.broadcasted_iota(jnp.int32, l0.shape, 1)
    l0 = jnp.where(lane < R_DECAY, jnp.tanh(l0), l0)
    l1 = jax.nn.sigmoid(zs[:, 3 * RW + LANES:])
    lin = jnp.concatenate([l0, l1], axis=1).astype(BF16)
    lo = _dot(lin, wl_ref[...])
    w_log = -_softplus(-(w0_ref[...] + lo[:, :RW])) - 0.5
    a = jax.nn.sigmoid(a0_ref[...] + lo[:, RW:2 * RW])
    ones_bd = ones_ref[...]
    kk = k * kkw_ref[...]
    kkn = kk / jnp.maximum(jnp.sqrt(_segsum(kk * kk, ones_bd)), 1e-12)
    kh = k * (1.0 + (a - 1.0) * ka_ref[...])
    bonus = _segsum(r * kh * rk_ref[...], ones_bd) * v
    return r, jnp.exp(-jnp.exp(w_log)), kh, v, kkn, kkn * a, lo[:, 2 * RW:], bonus


def _rwkv_pre_prompt_kernel(z_ref, prev_ref, mu_ref, w0_ref, a0_ref, kkw_ref, ka_ref, rk_ref, wl_ref,
                            ones_ref, r_o, w_o, k_o, v_o, kk_o, b_o, g_o, bon_o, zt_scr):
    bi = pl.program_id(1)
    z = z_ref[...]
    row = lax.broadcasted_iota(jnp.int32, z.shape, 0)
    zp = jnp.where(row == 0, prev_ref[0], pltpu.roll(z, 1, axis=0))
    r, w, kh, v, kkn, b, g, bonus = _rwkv_pre_math(z, zp, mu_ref, w0_ref, a0_ref, kkw_ref, ka_ref,
                                                   rk_ref, wl_ref, ones_ref)
    g_o[...] = g
    bon_o[...] = bonus
    for n, x in enumerate((r, w, kh, v, kkn, b)):
        for p in range(RW // LANES):
            row0 = pl.multiple_of((bi * (RW // LANES) + p) * LANES, LANES)
            zt_scr[n, pl.ds(row0, LANES), :] = x[:, p * LANES:(p + 1) * LANES].T

    @pl.when(bi == pl.num_programs(1) - 1)
    def _():
        n_bh = zt_scr.shape[1] // HEAD

        def channel_rows(n, c0):
            cols = [[] for _ in range(SUBLANES)]
            for j in range(n_bh // SUBLANES):
                blk = [zt_scr[n, pl.ds(pl.multiple_of((j * SUBLANES + k) * HEAD + c0, SUBLANES), SUBLANES), :]
                       for k in range(SUBLANES)]
                for m, piece in enumerate(_sublane_transpose8(blk)):
                    cols[m].append(piece)
            return [jnp.concatenate(pieces, axis=0) for pieces in cols]

        def store_rows(o_ref, mats, rows_per_t, r0):
            for jt in range(RW_TT // SUBLANES):
                blk = [mt[jt * SUBLANES:(jt + 1) * SUBLANES] for mt in mats]
                for s, piece in enumerate(_sublane_transpose8(blk)):
                    row = pl.multiple_of((jt * SUBLANES + s) * rows_per_t + r0, SUBLANES)
                    o_ref[pl.ds(row, SUBLANES), :] = piece

        for n, o_ref in enumerate((r_o, w_o, k_o, v_o, kk_o, b_o)):
            if o_ref is v_o:
                def group(gi, c, n=n, o_ref=o_ref):
                    c0 = pl.multiple_of(gi * 2 * SUBLANES, 2 * SUBLANES)
                    a = channel_rows(n, c0) + channel_rows(n, c0 + SUBLANES)
                    mats = [jnp.concatenate([a[2 * q], a[2 * q + 1]], axis=0).T for q in range(SUBLANES)]
                    store_rows(o_ref, mats, HEAD // 2, pl.multiple_of(gi * SUBLANES, SUBLANES))
                    return c

                lax.fori_loop(0, HEAD // (2 * SUBLANES), group, 0)
            else:
                def group(gi, c, n=n, o_ref=o_ref):
                    c0 = pl.multiple_of(gi * SUBLANES, SUBLANES)
                    mats = [jnp.concatenate([a, a], axis=0).T for a in channel_rows(n, c0)]
                    store_rows(o_ref, mats, HEAD, c0)
                    return c

                lax.fori_loop(0, HEAD // SUBLANES, group, 0)


def _rwkv_pre_prompt(z_main, prev, n_b, n_t, mu, w0, a0, kkw, ka, rk, wl, ones_bd):
    nt = n_t // RW_TT
    vec = lambda w: pl.BlockSpec((1, w), lambda i, b: (0, 0))
    keyed = pl.BlockSpec((RW_TT * HEAD, LANES), lambda i, b: (i, 0), pipeline_mode=pl.Buffered(1))
    paired = pl.BlockSpec((RW_TT * (HEAD // 2), LANES), lambda i, b: (i, 0), pipeline_mode=pl.Buffered(1))
    keyed_shape = jax.ShapeDtypeStruct((n_t * HEAD, LANES), F32)
    paired_shape = jax.ShapeDtypeStruct((n_t * (HEAD // 2), LANES), F32)
    nat = pl.BlockSpec((RW_TT, RW), lambda i, b: (b * nt + i, 0))
    return pl.pallas_call(
        _rwkv_pre_prompt_kernel,
        grid=(nt, n_b),
        in_specs=[pl.BlockSpec((RW_TT, RW_PAD), lambda i, b: (b * nt + i, 0)),
                  pl.BlockSpec((1, 1, RW_PAD), lambda i, b: (b * nt + i, 0, 0)),
                  vec(RW_PAD), vec(RW), vec(RW), vec(RW), vec(RW), vec(RW),
                  pl.BlockSpec((LORA_IN, 3 * RW), lambda i, b: (0, 0)),
                  pl.BlockSpec((RW, RW), lambda i, b: (0, 0))],
        out_specs=[keyed] * 3 + [paired] + [keyed] * 2 + [nat] * 2,
        out_shape=[keyed_shape] * 3 + [paired_shape] + [keyed_shape] * 2
        + [jax.ShapeDtypeStruct((n_b * n_t, RW), F32)] * 2,
        scratch_shapes=[pltpu.VMEM((6, n_b * RW, RW_TT), F32)],
        compiler_params=_params(("arbitrary", "arbitrary")),
        name="rwkv_pre_prompt",
    )(z_main, prev, mu, w0, a0, kkw, ka, rk, wl, ones_bd)


def _rwkv_pre_sample_kernel(z_ref, zp_ref, st_ref, mu_ref, w0_ref, a0_ref, kkw_ref, ka_ref, rk_ref, wl_ref,
                            ones_ref, r_o, w_o, k_o, v_o, kk_o, b_o, g_o, bon_o):
    zp = jnp.where(pl.program_id(0) == 0, st_ref[...], zp_ref[...])
    r, w, kh, v, kkn, b, g, bonus = _rwkv_pre_math(z_ref[...], zp, mu_ref, w0_ref, a0_ref, kkw_ref, ka_ref,
                                                   rk_ref, wl_ref, ones_ref)
    g_o[...] = g
    bon_o[...] = bonus
    for x, o_ref in zip((r, w, kh, v, kkn, b), (r_o, w_o, k_o, v_o, kk_o, b_o)):
        for p in range(RW // LANES):
            xt = x[:, p * LANES:(p + 1) * LANES].T
            o_ref[0, :, 2 * p * LANES:(2 * p + 1) * LANES] = xt[:HEAD]
            o_ref[0, :, (2 * p + 1) * LANES:(2 * p + 2) * LANES] = xt[HEAD:]


def _rwkv_pre_sample(z_main, state, row0, n_b, n_t, mu, w0, a0, kkw, ka, rk, wl, ones_bd):
    blk0 = row0 // n_b
    vec = lambda w: pl.BlockSpec((1, w), lambda t: (0, 0))
    keyed = pl.BlockSpec((1, HEAD, HEADS * n_b), lambda t: (t, 0, 0))
    nat = pl.BlockSpec((n_b, RW), lambda t: (t, 0))
    return pl.pallas_call(
        _rwkv_pre_sample_kernel,
        grid=(n_t,),
        in_specs=[pl.BlockSpec((n_b, RW_PAD), lambda t: (blk0 + t, 0)),
                  pl.BlockSpec((n_b, RW_PAD), lambda t: (blk0 + jnp.maximum(t - 1, 0), 0)),
                  pl.BlockSpec((n_b, RW_PAD), lambda t: (0, 0)),
                  vec(RW_PAD), vec(RW), vec(RW), vec(RW), vec(RW), vec(RW),
                  pl.BlockSpec((LORA_IN, 3 * RW), lambda t: (0, 0)),
                  pl.BlockSpec((RW, RW), lambda t: (0, 0))],
        out_specs=[keyed] * 6 + [nat] * 2,
        out_shape=[jax.ShapeDtypeStruct((n_t, HEAD, HEADS * n_b), F32)] * 6
        + [jax.ShapeDtypeStruct((n_t * n_b, RW), F32)] * 2,
        compiler_params=_params(("arbitrary",)),
        name="rwkv_pre_sample",
    )(z_main, z_main, state, mu, w0, a0, kkw, ka, rk, wl, ones_bd)


def _rwkv_scan_kernel(nat_state, r_ref, w_ref, k_ref, kk_ref, b_ref, v_ref, s0_ref, y_ref, sT_ref, s_scr):
    tc = pl.program_id(1)
    n_t = r_ref.shape[0]
    n_i = s_scr.shape[0]

    @pl.when(tc == 0)
    def _():
        if nat_state:
            s_scr[...] = s0_ref[...].T.reshape(s_scr.shape)
        else:
            s_scr[...] = s0_ref[...]

    def step(t, carry):
        r_t, w_t, k_t, kk_t, b_t = r_ref[t], w_ref[t], k_ref[t], kk_ref[t], b_ref[t]

        def rows8(ib, c2):
            i0 = pl.multiple_of(ib * SUBLANES, SUBLANES)
            v8 = v_ref[t, pl.ds(i0, SUBLANES), :]
            ys = []
            for ii in range(SUBLANES):
                s_old = s_scr[i0 + ii]
                sa = jnp.sum(s_old * kk_t, axis=0, keepdims=True)
                s_new = s_old * w_t - sa * b_t + v8[ii:ii + 1] * k_t
                ys.append(jnp.sum(s_new * r_t, axis=0, keepdims=True))
                s_scr[i0 + ii] = s_new
            y_ref[t, pl.ds(i0, SUBLANES), :] = jnp.concatenate(ys, axis=0)
            return c2

        return lax.fori_loop(0, n_i // SUBLANES, rows8, carry)

    lax.fori_loop(0, n_t, step, 0)

    @pl.when(tc == pl.num_programs(1) - 1)
    def _():
        if nat_state:
            sT_ref[...] = s_scr[...].reshape(n_i * s_scr.shape[1], LANES).T
        else:
            sT_ref[...] = s_scr[...]


def _rwkv_scan_prompt(r, w, k, kk, b, v, s0, tchunk):
    n_t = r.shape[0]
    n_i = s0.shape[0]
    op = pl.BlockSpec((tchunk, HEAD, LANES), lambda l, t: (t, 0, 0))
    vy = pl.BlockSpec((tchunk, n_i, LANES), lambda l, t: (t, 0, 0))
    st = pl.BlockSpec((n_i, HEAD, LANES), lambda l, t: (0, 0, 0))
    return pl.pallas_call(
        functools.partial(_rwkv_scan_kernel, False),
        grid=(1, n_t // tchunk),
        in_specs=[op] * 5 + [vy, st],
        out_specs=[vy, st],
        out_shape=[jax.ShapeDtypeStruct((n_t, n_i, LANES), F32),
                   jax.ShapeDtypeStruct((n_i, HEAD, LANES), F32)],
        scratch_shapes=[pltpu.VMEM((n_i, HEAD, LANES), F32)],
        compiler_params=_params(("arbitrary", "arbitrary")),
        name="rwkv_scan_prompt",
    )(r, w, k, kk, b, v, s0)


def _rwkv_scan_sample(r, w, k, kk, b, v, s0, layer):
    n_t, _, n_l = r.shape
    n_b = n_l // HEADS
    op = pl.BlockSpec((n_t, HEAD, LANES), lambda h, t: (0, 0, h))
    st = pl.BlockSpec((n_b, HEAD * HEAD), lambda h, t: (layer, h))
    st_out = pl.BlockSpec((n_b, HEAD * HEAD), lambda h, t: (0, h))
    return pl.pallas_call(
        functools.partial(_rwkv_scan_kernel, True),
        grid=(HEADS, 1),
        in_specs=[op] * 6 + [st],
        out_specs=[op, st_out],
        out_shape=[jax.ShapeDtypeStruct((n_t, HEAD, n_l), F32),
                   jax.ShapeDtypeStruct((n_b, HEADS * HEAD * HEAD), F32)],
        scratch_shapes=[pltpu.VMEM((HEAD, HEAD, LANES), F32)],
        compiler_params=_params(("parallel", "arbitrary")),
        name="rwkv_scan_sample",
    )(r, w, k, kk, b, v, s0)


def _rwkv_post_math(y, bon_ref, g_ref, lg_ref, lb_ref, ones_ref):
    ones_bd = ones_ref[...]
    yc = y - _segsum(y, ones_bd) * (1.0 / HEAD)
    var = _segsum(yc * yc, ones_bd) * (1.0 / HEAD)
    yn = yc * lax.rsqrt(var + GN_EPS) * lg_ref[...] + lb_ref[...]
    return ((yn + bon_ref[...]) * g_ref[...]).astype(BF16)


def _rwkv_post_prompt_kernel(y_ref, bon_ref, g_ref, lg_ref, lb_ref, ones_ref, o_ref, zt_scr):
    bi = pl.program_id(1)

    @pl.when(bi == 0)
    def _():
        n_bh = zt_scr.shape[0] // HEAD

        def group(gi, c):
            q0 = pl.multiple_of(gi * SUBLANES, SUBLANES)
            mats = [[] for _ in range(SUBLANES)]
            for jt in range(RW_TT // SUBLANES):
                blk = [y_ref[pl.ds(pl.multiple_of((jt * SUBLANES + s) * (HEAD // 2) + q0, SUBLANES), SUBLANES), :]
                       for s in range(SUBLANES)]
                for m, piece in enumerate(_sublane_transpose8(blk)):
                    mats[m].append(piece)
            vals = []
            for pieces in mats:
                mt = jnp.concatenate(pieces, axis=0).T
                vals += [mt[:HALF], mt[HALF:]]
            for half in range(2):
                for j in range(n_bh // SUBLANES):
                    blk = [v[j * SUBLANES:(j + 1) * SUBLANES] for v in vals[half * SUBLANES:(half + 1) * SUBLANES]]
                    for k, piece in enumerate(_sublane_transpose8(blk)):
                        row = pl.multiple_of((j * SUBLANES + k) * HEAD + 2 * q0 + half * SUBLANES, SUBLANES)
                        zt_scr[pl.ds(row, SUBLANES), :] = piece
            return c

        lax.fori_loop(0, HEAD // (2 * SUBLANES), group, 0)

    pieces = []
    for p in range(RW // LANES):
        row0 = pl.multiple_of((bi * (RW // LANES) + p) * LANES, LANES)
        pieces.append(zt_scr[pl.ds(row0, LANES), :].T)
    y = jnp.concatenate(pieces, axis=1)
    o_ref[...] = _rwkv_post_math(y, bon_ref, g_ref, lg_ref, lb_ref, ones_ref)


def _rwkv_post_prompt(y, bonus, g, n_b, n_t, lnx_g, lnx_b, ones_bd):
    nt = n_t // RW_TT
    nat = pl.BlockSpec((RW_TT, RW), lambda i, b: (b * nt + i, 0))
    vec = pl.BlockSpec((1, RW), lambda i, b: (0, 0))
    return pl.pallas_call(
        _rwkv_post_prompt_kernel,
        grid=(nt, n_b),
        in_specs=[pl.BlockSpec((RW_TT * (HEAD // 2), LANES), lambda i, b: (i, 0)), nat, nat, vec, vec,
                  pl.BlockSpec((RW, RW), lambda i, b: (0, 0))],
        out_specs=nat,
        out_shape=jax.ShapeDtypeStruct((n_b * n_t, RW), BF16),
        scratch_shapes=[pltpu.VMEM((n_b * RW, RW_TT), F32)],
        compiler_params=_params(("arbitrary", "arbitrary")),
        name="rwkv_post_prompt",
    )(y, bonus, g, lnx_g, lnx_b, ones_bd)


def _rwkv_post_sample_kernel(y_ref, bon_ref, g_ref, lg_ref, lb_ref, ones_ref, o_ref):
    pieces = []
    for p in range(RW // LANES):
        m = jnp.concatenate([y_ref[0, :, 2 * p * LANES:(2 * p + 1) * LANES],
                             y_ref[0, :, (2 * p + 1) * LANES:(2 * p + 2) * LANES]], axis=0)
        pieces.append(m.T)
    y = jnp.concatenate(pieces, axis=1)
    o_ref[...] = _rwkv_post_math(y, bon_ref, g_ref, lg_ref, lb_ref, ones_ref)


def _rwkv_post_sample(y, bonus, g, n_b, n_t, lnx_g, lnx_b, ones_bd):
    nat = pl.BlockSpec((n_b, RW), lambda t: (t, 0))
    vec = pl.BlockSpec((1, RW), lambda t: (0, 0))
    return pl.pallas_call(
        _rwkv_post_sample_kernel,
        grid=(n_t,),
        in_specs=[pl.BlockSpec((1, HEAD, HEADS * n_b), lambda t: (t, 0, 0)), nat, nat, vec, vec,
                  pl.BlockSpec((RW, RW), lambda t: (0, 0))],
        out_specs=nat,
        out_shape=jax.ShapeDtypeStruct((n_t * n_b, RW), BF16),
        compiler_params=_params(("parallel",)),
        name="rwkv_post_sample",
    )(y, bonus, g, lnx_g, lnx_b, ones_bd)


def _pool_prompt_kernel(start_pos, z_ref, pw_ref, ps_ref, y_ref):
    n = z_ref.shape[0]
    row = lax.broadcasted_iota(jnp.int32, (n, POOL_G), 0)
    for gi, win in enumerate(WINDOWS):
        sl = slice(gi * POOL_G, (gi + 1) * POOL_G)
        x = z_ref[:, sl]
        s = x
        k = 1
        while k < win:
            s = s + jnp.where(row >= k, pltpu.roll(s, k, axis=0), 0.0)
            k *= 2
        cnt = jnp.minimum(win, row + (start_pos + 1)).astype(F32)
        d = s / cnt - x
        y_ref[:, sl] = (_dot(d.astype(BF16), pw_ref[gi]) * ps_ref[:, sl]).astype(BF16)


def _pool_prompt(z_main, n_b, n_t, pw, ps):
    cb = RW_PAD // POOL_W
    return pl.pallas_call(
        functools.partial(_pool_prompt_kernel, 0),
        grid=(n_b,),
        in_specs=[pl.BlockSpec((n_t, POOL_W), lambda b: (b, cb)),
                  pl.BlockSpec((len(WINDOWS), POOL_G, POOL_G), lambda b: (0, 0, 0)),
                  pl.BlockSpec((1, POOL_W), lambda b: (0, 0))],
        out_specs=pl.BlockSpec((n_t, POOL_W), lambda b: (b, 0)),
        out_shape=jax.ShapeDtypeStruct((n_b * n_t, POOL_W), BF16),
        compiler_params=_params(("parallel",)),
        name="pool_prompt",
    )(z_main, pw, ps)


def _pool_sample_kernel(start_pos, n_t, full_ref, pw_ref, ps_ref, y_ref):
    n_b = full_ref.shape[1]
    for gi, win in enumerate(WINDOWS):
        sl = slice(gi * POOL_G, (gi + 1) * POOL_G)
        f = full_ref[:, :, sl]
        s = f
        k = 1
        while k < win:
            s = s[k:] + s[:-k]
            k *= 2
        s = s[s.shape[0] - n_t:]
        x = f[POOL_BUF:]
        pos = start_pos + lax.broadcasted_iota(jnp.int32, s.shape, 0)
        cnt = jnp.minimum(win, pos + 1).astype(F32)
        d = (s / cnt - x).reshape(n_t * n_b, POOL_G)
        y_ref[:, sl] = (_dot(d.astype(BF16), pw_ref[gi]) * ps_ref[:, sl]).astype(BF16)


def _pool_sample(full, n_t, start_pos, pw, ps):
    n_f, n_b, _ = full.shape
    return pl.pallas_call(
        functools.partial(_pool_sample_kernel, start_pos, n_t),
        grid=(1,),
        in_specs=[pl.BlockSpec((n_f, n_b, POOL_W), lambda i: (0, 0, 0)),
                  pl.BlockSpec((len(WINDOWS), POOL_G, POOL_G), lambda i: (0, 0, 0)),
                  pl.BlockSpec((1, POOL_W), lambda i: (0, 0))],
        out_specs=pl.BlockSpec((n_t * n_b, POOL_W), lambda i: (0, 0)),
        out_shape=jax.ShapeDtypeStruct((n_t * n_b, POOL_W), BF16),
        compiler_params=_params(("arbitrary",)),
        name="pool_sample",
    )(full, pw, ps)


def _s5_tail(x_all, u, cmat_ref, d_ref, gw_ref, gb_ref):
    y = _dot(x_all.astype(BF16), cmat_ref[...]) + d_ref[...] * u
    y = jax.nn.gelu(y)
    return (y * jax.nn.sigmoid(_dot(y.astype(BF16), gw_ref[...]) + gb_ref[...])).astype(BF16)


S5_LC = 512


def _s5_prompt_kernel(u_ref, bmat_ref, cmat_ref, lam_ref, pw_ref, d_ref, gw_ref, gb_ref,
                      y_ref, xre_o, xim_o, x_scr, c_scr):
    tc = pl.program_id(1)

    @pl.when(tc == 0)
    def _():
        c_scr[...] = jnp.zeros_like(c_scr)

    u = u_ref[...]
    n = u.shape[0]
    x_scr[...] = _dot(u.astype(BF16), bmat_ref[...])
    row = lax.broadcasted_iota(jnp.int32, (SUBLANES, S5_LC), 0)
    for lc in range(S5_N // S5_LC):
        re_sl = pl.ds(lc * S5_LC, S5_LC)
        im_sl = pl.ds(S5_N + lc * S5_LC, S5_LC)
        lam = [(lam_ref[2 * j:2 * j + 1, re_sl], lam_ref[2 * j + 1:2 * j + 2, re_sl]) for j in range(3)]
        p_re, p_im = pw_ref[:, re_sl], pw_ref[:, im_sl]

        def blk(rb, carry, re_sl=re_sl, im_sl=im_sl, lam=lam, p_re=p_re, p_im=p_im):
            c_re, c_im = carry
            rows = pl.ds(pl.multiple_of(rb * SUBLANES, SUBLANES), SUBLANES)
            xr, xi = x_scr[rows, re_sl], x_scr[rows, im_sl]
            for j, (l_re, l_im) in enumerate(lam):
                kshift = 1 << j
                sr = jnp.where(row >= kshift, pltpu.roll(xr, kshift, axis=0), 0.0)
                si = jnp.where(row >= kshift, pltpu.roll(xi, kshift, axis=0), 0.0)
                xr, xi = xr + (l_re * sr - l_im * si), xi + (l_re * si + l_im * sr)
            xr, xi = xr + (p_re * c_re - p_im * c_im), xi + (p_re * c_im + p_im * c_re)
            x_scr[rows, re_sl] = xr
            x_scr[rows, im_sl] = xi
            return xr[SUBLANES - 1:], xi[SUBLANES - 1:]

        c_re, c_im = lax.fori_loop(0, n // SUBLANES, blk, (c_scr[0:1, re_sl], c_scr[0:1, im_sl]))
        c_scr[0:1, re_sl] = c_re
        c_scr[0:1, im_sl] = c_im

    y_ref[...] = _s5_tail(x_scr[...], u, cmat_ref, d_ref, gw_ref, gb_ref)

    @pl.when(tc == pl.num_programs(1) - 1)
    def _():
        xre_o[0] = c_scr[0:1, :S5_N]
        xim_o[0] = c_scr[0:1, S5_N:]


def _s5_prompt(z_main, n_b, n_t, tt, bmat, cmat, lam_t, pw_t, dskip, gw, gb):
    cb = (RW_PAD + POOL_W) // S5_W
    n_tc = n_t // tt
    const = lambda shape: pl.BlockSpec(shape, lambda b, t: (0,) * len(shape))
    return pl.pallas_call(
        _s5_prompt_kernel,
        grid=(n_b, n_tc),
        in_specs=[pl.BlockSpec((tt, S5_W), lambda b, t: (b * n_tc + t, cb)),
                  const((S5_W, 2 * S5_N)), const((2 * S5_N, S5_W)),
                  const((SUBLANES, S5_N)), const((SUBLANES, 2 * S5_N)),
                  const((1, S5_W)), const((S5_W, S5_W)), const((1, S5_W))],
        out_specs=[pl.BlockSpec((tt, S5_W), lambda b, t: (b * n_tc + t, 0)),
                   pl.BlockSpec((1, 1, S5_N), lambda b, t: (b, 0, 0)),
                   pl.BlockSpec((1, 1, S5_N), lambda b, t: (b, 0, 0))],
        out_shape=[jax.ShapeDtypeStruct((n_b * n_t, S5_W), BF16),
                   jax.ShapeDtypeStruct((n_b, 1, S5_N), F32),
                   jax.ShapeDtypeStruct((n_b, 1, S5_N), F32)],
        scratch_shapes=[pltpu.VMEM((tt, 2 * S5_N), F32), pltpu.VMEM((SUBLANES, 2 * S5_N), F32)],
        compiler_params=_params(("parallel", "arbitrary")),
        name="s5_prompt",
    )(z_main, bmat, cmat, lam_t, pw_t, dskip, gw, gb)


def _s5_sample_kernel(n_t, u_ref, x0re_ref, x0im_ref, bmat_ref, cmat_ref, lam_ref, d_ref, gw_ref,
                      gb_ref, y_ref, xre_o, xim_o, x_scr):
    u = u_ref[...]
    n_b = u.shape[0] // n_t
    x_scr[...] = _dot(u.astype(BF16), bmat_ref[...])
    for lc in range(S5_N // S5_LC):
        re_sl = pl.ds(lc * S5_LC, S5_LC)
        im_sl = pl.ds(S5_N + lc * S5_LC, S5_LC)
        l_re, l_im = lam_ref[0:1, re_sl], lam_ref[1:2, re_sl]
        xr, xi = x0re_ref[:, re_sl], x0im_ref[:, re_sl]
        for t in range(n_t):
            rows = pl.ds(t * n_b, n_b)
            xr, xi = (l_re * xr - l_im * xi + x_scr[rows, re_sl],
                      l_re * xi + l_im * xr + x_scr[rows, im_sl])
            x_scr[rows, re_sl] = xr
            x_scr[rows, im_sl] = xi
        xre_o[:, re_sl] = xr
        xim_o[:, re_sl] = xi
    y_ref[...] = _s5_tail(x_scr[...], u, cmat_ref, d_ref, gw_ref, gb_ref)


def _s5_sample(z_main, row0, n_b, n_t, x0re, x0im, bmat, cmat, lam_t, dskip, gw, gb):
    rows = n_b * n_t
    cb = (RW_PAD + POOL_W) // S5_W
    const = lambda shape: pl.BlockSpec(shape, lambda i: (0,) * len(shape))
    return pl.pallas_call(
        functools.partial(_s5_sample_kernel, n_t),
        grid=(1,),
        in_specs=[pl.BlockSpec((rows, S5_W), lambda i: (row0 // rows, cb)),
                  const((n_b, S5_N)), const((n_b, S5_N)),
                  const((S5_W, 2 * S5_N)), const((2 * S5_N, S5_W)), const((SUBLANES, S5_N)),
                  const((1, S5_W)), const((S5_W, S5_W)), const((1, S5_W))],
        out_specs=[const((rows, S5_W)), const((n_b, S5_N)), const((n_b, S5_N))],
        out_shape=[jax.ShapeDtypeStruct((rows, S5_W), BF16),
                   jax.ShapeDtypeStruct((n_b, S5_N), F32),
                   jax.ShapeDtypeStruct((n_b, S5_N), F32)],
        scratch_shapes=[pltpu.VMEM((rows, 2 * S5_N), F32)],
        compiler_params=_params(("arbitrary",)),
        name="s5_sample",
    )(z_main, x0re, x0im, bmat, cmat, lam_t, dskip, gw, gb)


def _merge_kernel(ya_ref, yb_ref, yc_ref, zg_ref, wbr_ref, wout_ref, x_ref, g1_ref, n2_ref,
                  sc2_ref, sh2_ref, rw_ref, rb_ref, xo_ref, h2_ref, lg_ref):
    d = x_ref.shape[1]
    rows = x_ref.shape[0]
    m = jax.nn.sigmoid(zg_ref[:, :d]) * _dot(ya_ref[...], wbr_ref[:RW, :])
    m = m + jax.nn.sigmoid(zg_ref[:, d:2 * d]) * _dot(yb_ref[...], wbr_ref[RW:RW + POOL_W, :])
    m = m + jax.nn.sigmoid(zg_ref[:, 2 * d:]) * _dot(yc_ref[...], wbr_ref[RW + POOL_W:, :])
    xn = x_ref[...] + _tile_rows(g1_ref[...], rows) * _dot(m.astype(BF16), wout_ref[...])
    xo_ref[...] = xn
    h2 = _norm_mod(xn, n2_ref[...], sc2_ref[...], sh2_ref[...])
    h2_ref[...] = h2
    lg_ref[...] = jnp.dot(h2, rw_ref[...], precision=lax.Precision.HIGHEST,
                          preferred_element_type=F32) + rb_ref[...]


def _merge(ya, yb, yc, zg, wbr, wout, x, g1_t, n2g, sc2_t, sh2_t, rw, rb, mod_idx, tm):
    m, d = x.shape
    row = lambda w: pl.BlockSpec((tm, w), lambda i: (i, 0))
    const = lambda shape: pl.BlockSpec(shape, lambda i: (0,) * len(shape), pipeline_mode=pl.Buffered(1))
    mod = pl.BlockSpec((MOD_ROWS, d), lambda i: (mod_idx(i), 0))
    return pl.pallas_call(
        _merge_kernel,
        grid=(m // tm,),
        in_specs=[row(RW), row(POOL_W), row(S5_W), row(3 * d), const(wbr.shape), const(wout.shape),
                  row(d), mod, const((1, d)), mod, mod, const(rw.shape), const((1, LANES))],
        out_specs=[row(d), row(d), row(LANES)],
        out_shape=[jax.ShapeDtypeStruct((m, d), F32), jax.ShapeDtypeStruct((m, d), F32),
                   jax.ShapeDtypeStruct((m, LANES), F32)],
        compiler_params=_params(("parallel",)),
        name="merge",
    )(ya, yb, yc, zg, wbr, wout, x, g1_t, n2g, sc2_t, sh2_t, rw, rb)


MOE_SUB = 256
MOE_R = 6
MOE_TF = 512
MOE_VMEM = 60 * 1024 * 1024


def _moe_kernel(ge_ref, gs_ref, gn_ref, gl_ref, x_ref, wg_ref, wu_ref, bg_ref, bu_ref, wd_ref, bd_ref,
                o_ref, xs_scr, acc_scr):
    g, f, r = pl.program_id(0), pl.program_id(1), pl.program_id(2)
    nf = pl.num_programs(1)

    @pl.when(r < gn_ref[g])
    def _():
        @pl.when(f == 0)
        def _():
            xs_scr[r] = x_ref[...].astype(BF16)

        x = xs_scr[r]
        gate = _dot(x, wg_ref[0].astype(BF16)) + bg_ref[0]
        up = _dot(x, wu_ref[0].astype(BF16)) + bu_ref[0]
        gate = jnp.minimum(gate, SW_LIMIT)
        up = jnp.clip(up, -SW_LIMIT, SW_LIMIT)
        act = (up + 1.0) * (gate * jax.nn.sigmoid(SW_ALPHA * gate))
        part = _dot(act.astype(BF16), wd_ref[0].astype(BF16))

        @pl.when(f == 0)
        def _():
            acc_scr[r] = part + bd_ref[0]

        @pl.when(jnp.logical_and(f > 0, f < nf - 1))
        def _():
            acc_scr[r] += part

        @pl.when(f == nf - 1)
        def _():
            o_ref[...] = acc_scr[r] + part


def _moe(xs, groups, layer, w_gu, b_gu, w_down, b_down):
    rows, d = xs.shape
    n_l, n_e, _, two_ff = w_gu.shape
    dff = two_ff // 2
    nf = dff // MOE_TF
    assert nf >= 2
    e0 = layer * n_e
    n_groups = groups[0].shape[0]

    def x_idx(g, f, r, ge, gs, gn, gl):
        return jnp.where(jnp.logical_and(f == 0, r < gn[g]), gs[g] + r, gl[g]), 0

    def f_eff(g, f, gn):
        return jnp.where(gn[g] > 0, f, nf - 1)

    def o_idx(g, f, r, ge, gs, gn, gl):
        last = f == nf - 1
        return jnp.where(jnp.logical_and(last, r < gn[g]), gs[g] + r, jnp.where(last, gl[g], gs[g])), 0

    grid_spec = pltpu.PrefetchScalarGridSpec(
        num_scalar_prefetch=4,
        grid=(n_groups, nf, MOE_R),
        in_specs=[
            pl.BlockSpec((MOE_SUB, d), x_idx),
            pl.BlockSpec((1, d, MOE_TF), lambda g, f, r, ge, gs, gn, gl: (e0 + ge[g], 0, f_eff(g, f, gn))),
            pl.BlockSpec((1, d, MOE_TF), lambda g, f, r, ge, gs, gn, gl: (e0 + ge[g], 0, nf + f_eff(g, f, gn))),
            pl.BlockSpec((1, 1, MOE_TF), lambda g, f, r, ge, gs, gn, gl: (e0 + ge[g], 0, f_eff(g, f, gn))),
            pl.BlockSpec((1, 1, MOE_TF), lambda g, f, r, ge, gs, gn, gl: (e0 + ge[g], 0, nf + f_eff(g, f, gn))),
            pl.BlockSpec((1, MOE_TF, d), lambda g, f, r, ge, gs, gn, gl: (e0 + ge[g], f_eff(g, f, gn), 0)),
            pl.BlockSpec((1, 1, d), lambda g, f, r, ge, gs, gn, gl: (e0 + ge[g], 0, 0)),
        ],
        out_specs=pl.BlockSpec((MOE_SUB, d), o_idx),
        scratch_shapes=[pltpu.VMEM((MOE_R, MOE_SUB, d), BF16), pltpu.VMEM((MOE_R, MOE_SUB, d), F32)],
    )
    w_gu = w_gu.reshape(n_l * n_e, d, two_ff)
    b_gu = b_gu.reshape(n_l * n_e, 1, two_ff)
    return pl.pallas_call(
        _moe_kernel,
        grid_spec=grid_spec,
        out_shape=jax.ShapeDtypeStruct((rows, d), F32),
        compiler_params=pltpu.CompilerParams(dimension_semantics=("arbitrary",) * 3,
                                             vmem_limit_bytes=MOE_VMEM),
        name="moe",
    )(*groups, xs, w_gu, w_gu, b_gu, b_gu, w_down.reshape(n_l * n_e, dff, d),
      b_down.reshape(n_l * n_e, 1, d))


def _route(logits, n_rows_pad, n_groups_max):
    n_tok = logits.shape[0]
    top_v, top_e = lax.top_k(logits, TOP_K)
    gate = jax.nn.softmax(top_v, axis=-1)
    flat_e = top_e.reshape(-1)
    onehot = (flat_e[:, None] == jnp.arange(N_EXP, dtype=jnp.int32)[None, :]).astype(jnp.int32)
    csum = jnp.cumsum(onehot, axis=0)
    counts = csum[-1]
    rank = jnp.sum((csum - 1) * onehot, axis=1)
    nblk = (counts + MOE_SUB - 1) // MOE_SUB
    blk_end = jnp.cumsum(nblk)
    blk_start = blk_end - nblk
    dest = (blk_start[flat_e] * MOE_SUB + rank).astype(jnp.int32)
    flat_tok = jnp.arange(n_tok * TOP_K, dtype=jnp.int32) // TOP_K
    rows_tok = jnp.zeros((n_rows_pad,), jnp.int32).at[dest].set(flat_tok)

    ngrp = (nblk + MOE_R - 1) // MOE_R
    grp_end = jnp.cumsum(ngrp)
    grp_start = grp_end - ngrp
    gidx = jnp.arange(n_groups_max, dtype=jnp.int32)
    used = gidx < grp_end[-1]
    ge = jnp.minimum(jnp.searchsorted(grp_end, gidx, side='right'), N_EXP - 1).astype(jnp.int32)
    local = gidx - grp_start[ge]
    gs = blk_start[ge] + local * MOE_R
    gn = jnp.clip(nblk[ge] - local * MOE_R, 0, MOE_R)
    gl = gs + gn - 1
    last = grp_end[-1] - 1
    groups = (jnp.where(used, ge, ge[last]), jnp.where(used, gs, gl[last]), jnp.where(used, gn, 0),
              jnp.where(used, gl, gl[last]))
    return gate, dest.reshape(n_tok, TOP_K), rows_tok, tuple(t.astype(jnp.int32) for t in groups)


def _combine_kernel(x_ref, y0_ref, y1_ref, y2_ref, y3_ref, gw_ref, g2_ref, o_ref):
    rows = x_ref.shape[0]
    gw = gw_ref[...]
    acc = gw[:, 0:1] * y0_ref[...]
    for k, y_ref in enumerate((y1_ref, y2_ref, y3_ref), start=1):
        acc = acc + gw[:, k:k + 1] * y_ref[...]
    o_ref[...] = x_ref[...] + _tile_rows(g2_ref[...], rows) * acc


def _combine(x, yg, gw, g2_t, mod_idx, tm):
    m, d = x.shape
    n_i = m // tm
    slab = lambda k: pl.BlockSpec((tm, d), lambda i: (k * n_i + i, 0))
    return pl.pallas_call(
        _combine_kernel,
        grid=(n_i,),
        in_specs=[pl.BlockSpec((tm, d), lambda i: (i, 0)), slab(0), slab(1), slab(2), slab(3),
                  pl.BlockSpec((tm, TOP_K), lambda i: (i, 0)),
                  pl.BlockSpec((MOD_ROWS, d), lambda i: (mod_idx(i), 0))],
        out_specs=pl.BlockSpec((tm, d), lambda i: (i, 0)),
        out_shape=jax.ShapeDtypeStruct((m, d), F32),
        compiler_params=_params(("parallel",)),
        name="combine",
    )(x, yg, yg, yg, yg, gw, g2_t)


def _final_norm_kernel(x_ref, g_ref, o_ref):
    x = x_ref[...]
    o_ref[...] = x * lax.rsqrt(jnp.mean(x * x, axis=-1, keepdims=True) + NORM_EPS) * g_ref[...]


def _final_norm(x, g, tm):
    m, d = x.shape
    return pl.pallas_call(
        _final_norm_kernel,
        grid=(m // tm,),
        in_specs=[pl.BlockSpec((tm, d), lambda i: (i, 0)), pl.BlockSpec((1, d), lambda i: (0, 0))],
        out_specs=pl.BlockSpec((tm, d), lambda i: (i, 0)),
        out_shape=jax.ShapeDtypeStruct((m, d), F32),
        compiler_params=_params(("parallel",)),
        name="final_norm",
    )(x, g)


def _s5_params(a_re, a_im, log_dt, b_re, b_im, c_re, c_im):
    dt = jnp.exp(log_dt)[:, None]
    mag = jnp.exp(a_re * dt)
    lb_re, lb_im = mag * jnp.cos(a_im * dt), mag * jnp.sin(a_im * dt)
    nr, ni = lb_re - 1.0, lb_im
    den = a_re * a_re + a_im * a_im
    f_re = (nr * a_re + ni * a_im) / den
    f_im = (ni * a_re - nr * a_im) / den
    bb_re = f_re[..., None] * b_re - f_im[..., None] * b_im
    bb_im = f_re[..., None] * b_im + f_im[..., None] * b_re
    eye = jnp.eye(S5_G, dtype=F32)
    bd_in = lambda w: jnp.einsum('gpc,gh->gchp', w, eye).reshape(S5_W, S5_N)
    bd_out = lambda w: jnp.einsum('gcp,gh->gphc', w, eye).reshape(S5_N, S5_W)
    bmat = jnp.concatenate([bd_in(bb_re), bd_in(bb_im)], axis=1).astype(BF16)
    cmat = jnp.concatenate([bd_out(c_re), -bd_out(c_im)], axis=0).astype(BF16)
    l_re, l_im = lb_re.reshape(1, S5_N), lb_im.reshape(1, S5_N)
    pows = [(l_re, l_im)]
    for _ in range(SUBLANES - 1):
        p_re, p_im = pows[-1]
        pows.append((p_re * l_re - p_im * l_im, p_re * l_im + p_im * l_re))
    lam_t = jnp.concatenate([pows[0][0], pows[0][1], pows[1][0], pows[1][1], pows[3][0], pows[3][1],
                             jnp.zeros((2, S5_N), F32)], axis=0)
    pw_t = jnp.concatenate([jnp.concatenate([p[0] for p in pows], axis=0),
                            jnp.concatenate([p[1] for p in pows], axis=0)], axis=1)
    return bmat, cmat, lam_t, pw_t


def _pad_cols(w, width):
    return jnp.pad(w, ((0, 0), (0, width - w.shape[1])))


def kernel(x_prompt, x_sample, c_prompt, c_sample, state_wkv, state_shift, state_pool, state_s5_re, state_s5_im, norm1_g, norm2_g, final_norm_g, w_ada, b_ada, w_in, rw_mu, rw_w0, rw_w2, rw_a0, rw_a2, rw_g2, rw_kk, rw_ka, rw_rk, rw_lnx_g, rw_lnx_b, pool_w, pool_scale, s5_a_re, s5_a_im, s5_log_dt, s5_b_re, s5_b_im, s5_c_re, s5_c_im, s5_d, s5_glu_w, s5_glu_b, w_br, w_out, router_w, router_b, moe_w_gu, moe_b_gu, moe_w_down, moe_b_down):
    bp, lp, d = x_prompt.shape
    bs, ls, _ = x_sample.shape
    depth = w_in.shape[0]
    mp, ms = bp * lp, bs * ls
    m = mp + ms
    past_len = 16384
    assert bs == MOD_ROWS and bp * HEADS * 2 == LANES and lp % 1024 == 0 and ms == 1024

    def mod_idx_for(tm):
        n_p = mp // tm
        return lambda i: jnp.where(i < n_p, (i * tm) // lp, bp)

    tm_big, tm_mid = 1024, 256
    ones_bd = jnp.kron(jnp.eye(HEADS, dtype=F32), jnp.ones((HEAD, HEAD), F32)).astype(BF16)

    x = jnp.concatenate([x_prompt.reshape(mp, d), jnp.swapaxes(x_sample, 0, 1).reshape(ms, d)], axis=0)
    c_all = jnp.concatenate([c_prompt, c_sample, jnp.zeros((4, d), F32)], axis=0)
    mod = _adaln(c_all, w_ada, b_ada)

    def table(layer, idx):
        v = mod[layer, :, idx * d:(idx + 1) * d]
        return jnp.concatenate([jnp.repeat(v[:bp], MOD_ROWS, axis=0), v[bp:bp + bs]], axis=0)

    outs_p, outs_s = [], []
    for l in range(depth):
        shift1, scale1, gate1, shift2, scale2, gate2 = (table(l, i) for i in range(6))

        w_l = w_in[l]
        w_main = jnp.concatenate([_pad_cols(w_l[:, :RW_PROJ], RW_PAD),
                                  w_l[:, RW_PROJ:RW_PROJ + POOL_W + S5_W]], axis=1).astype(BF16)
        w_gate = w_l[:, RW_PROJ + POOL_W + S5_W:].astype(BF16)
        n1 = norm1_g[l].reshape(1, d)
        z_main = _inproj(x, n1, scale1, shift1, w_main, mod_idx_for(tm_big), tm_big)
        z_gate = _inproj(x, n1, scale1, shift1, w_gate, mod_idx_for(tm_big), tm_big)

        mu = _pad_cols(rw_mu[l].reshape(1, RW_PROJ), RW_PAD)
        wl = jnp.zeros((LORA_IN, 3 * RW), F32)
        wl = wl.at[:R_DECAY, :RW].set(rw_w2[l])
        wl = wl.at[R_DECAY:R_DECAY + R_AAA, RW:2 * RW].set(rw_a2[l])
        wl = wl.at[R_DECAY + R_AAA:R_DECAY + R_AAA + R_GATE, 2 * RW:].set(rw_g2[l]).astype(BF16)
        vecs = [v.reshape(1, RW) for v in (rw_w0[l], rw_a0[l], rw_kk[l], rw_ka[l], rw_rk[l])]
        lnx_g, lnx_b = rw_lnx_g[l].reshape(1, RW), rw_lnx_b[l].reshape(1, RW)
        n_tp = mp // RW_TT
        last_rows = z_main[RW_TT - 1:mp:RW_TT, :RW_PAD]
        prev_p = jnp.concatenate([jnp.zeros((1, RW_PAD), F32), last_rows[:-1]], axis=0)
        first_of_seq = (jnp.arange(n_tp) % (lp // RW_TT) == 0)[:, None]
        prev_p = jnp.where(first_of_seq, 0.0, prev_p).reshape(n_tp, 1, RW_PAD)
        *ops_p, g_p, bon_p = _rwkv_pre_prompt(z_main, prev_p, bp, lp, mu, *vecs, wl, ones_bd)
        ops_p = [a.reshape(lp, a.shape[0] // lp, LANES) for a in ops_p]
        y_pt, st_p = _rwkv_scan_prompt(*ops_p[:3], *ops_p[4:], ops_p[3],
                                       jnp.zeros((HEAD // 2, HEAD, LANES), F32), 64)
        y_a_p = _rwkv_post_prompt(y_pt.reshape(lp * (HEAD // 2), LANES), bon_p, g_p, bp, lp,
                                  lnx_g, lnx_b, ones_bd)
        new_wkv_p = st_p.reshape(HEAD // 2, HEAD, 2, bp, HEADS).transpose(3, 4, 0, 2, 1)
        new_wkv_p = new_wkv_p.reshape(bp, HEADS, HEAD, HEAD)

        *ops_s, g_s, bon_s = _rwkv_pre_sample(z_main, _pad_cols(state_shift[l], RW_PAD), mp, bs, ls,
                                              mu, *vecs, wl, ones_bd)
        y_st, st_s = _rwkv_scan_sample(*ops_s[:3], *ops_s[4:], ops_s[3],
                                       state_wkv.reshape(depth * bs, HEADS * HEAD * HEAD), l)
        y_a_s = _rwkv_post_sample(y_st, bon_s, g_s, bs, ls, lnx_g, lnx_b, ones_bd)
        new_wkv_s = st_s.reshape(bs, HEADS, HEAD, HEAD)
        y_a = jnp.concatenate([y_a_p, y_a_s], axis=0)
        new_shift_p = z_main[lp - 1:mp:lp, :RW_PROJ]
        new_shift_s = z_main[m - bs:, :RW_PROJ]

        pw = pool_w[l].astype(BF16)
        ps = pool_scale[l].reshape(1, POOL_W)
        y_b_p = _pool_prompt(z_main, bp, lp, pw, ps)
        zb_s = z_main[mp:, RW_PAD:RW_PAD + POOL_W].reshape(ls, bs, POOL_W)
        full_s = jnp.concatenate([jnp.swapaxes(state_pool[l], 0, 1), zb_s], axis=0)
        y_b_s = _pool_sample(full_s, ls, past_len, pw, ps)
        y_b = jnp.concatenate([y_b_p, y_b_s], axis=0)
        new_pool_p = z_main[:mp, RW_PAD:RW_PAD + POOL_W].reshape(bp, lp, POOL_W)[:, lp - POOL_BUF:]
        new_pool_s = jnp.swapaxes(full_s[full_s.shape[0] - POOL_BUF:], 0, 1)

        bmat, cmat, lam_t, pw_t = _s5_params(s5_a_re[l], s5_a_im[l], s5_log_dt[l], s5_b_re[l], s5_b_im[l],
                                             s5_c_re[l], s5_c_im[l])
        dskip, gw, gb = s5_d[l].reshape(1, S5_W), s5_glu_w[l].astype(BF16), s5_glu_b[l].reshape(1, S5_W)
        y_c_p, re_p, im_p = _s5_prompt(z_main, bp, lp, 512, bmat, cmat, lam_t, pw_t, dskip, gw, gb)
        y_c_s, re_s, im_s = _s5_sample(z_main, mp, bs, ls, state_s5_re[l].reshape(bs, S5_N),
                                       state_s5_im[l].reshape(bs, S5_N), bmat, cmat, lam_t, dskip, gw, gb)
        y_c = jnp.concatenate([y_c_p, y_c_s], axis=0)

        rw_pad = _pad_cols(router_w[l], LANES)
        rb_pad = jnp.concatenate([router_b[l], jnp.full((LANES - N_EXP,), -1e30, F32)]).reshape(1, LANES)
        x, h2, logits = _merge(y_a, y_b, y_c, z_gate, w_br[l].astype(BF16), w_out[l].astype(BF16), x, gate1,
                               norm2_g[l].reshape(1, d), scale2, shift2, rw_pad, rb_pad,
                               mod_idx_for(tm_mid), tm_mid)

        n_blk_max = m * TOP_K // MOE_SUB + N_EXP
        n_groups_max = N_EXP + n_blk_max // MOE_R + 1
        gate_w, dest, rows_tok, groups = _route(logits[:, :N_EXP], n_blk_max * MOE_SUB, n_groups_max)
        yb = _moe(h2[rows_tok], groups, l, moe_w_gu, moe_b_gu, moe_w_down, moe_b_down)
        yg = yb[dest.T.reshape(-1)]
        x = _combine(x, yg, gate_w, gate2, mod_idx_for(tm_mid), tm_mid)

        outs_p.append((new_shift_p, new_wkv_p, new_pool_p, re_p.reshape(bp, S5_G, S5_P),
                       im_p.reshape(bp, S5_G, S5_P)))
        outs_s.append((new_shift_s, new_wkv_s, new_pool_s, re_s.reshape(bs, S5_G, S5_P),
                       im_s.reshape(bs, S5_G, S5_P)))

    y = _final_norm(x, final_norm_g.reshape(1, d), tm_mid)
    y_prompt = y[:mp].reshape(bp, lp, d)
    y_sample = jnp.swapaxes(y[mp:].reshape(ls, bs, d), 0, 1)
    p_shift, p_wkv, p_pool, p_re, p_im = (jnp.stack([o[j] for o in outs_p]) for j in range(5))
    s_shift, s_wkv, s_pool, s_re, s_im = (jnp.stack([o[j] for o in outs_s]) for j in range(5))
    return (y_prompt, y_sample, p_wkv, p_shift, p_pool, p_re, p_im,
            s_wkv, s_shift, s_pool, s_re, s_im)
```

```python
import functools

import jax
import jax.numpy as jnp
from jax import lax
from jax.experimental import pallas as pl
from jax.experimental.pallas import tpu as pltpu

F32, BF16 = jnp.float32, jnp.bfloat16

LANES = 128
SUBLANES = 8
VMEM_LIMIT = 56 * 1024 * 1024

HEAD = 64
HEADS = 16
RW = HEAD * HEADS
R_DECAY, R_AAA, R_GATE = 64, 64, 160
RW_PROJ = 3 * RW + R_DECAY + R_AAA + R_GATE
RW_PAD = 3584
LORA_IN = RW_PAD - 3 * RW
POOL_W = 512
WINDOWS = (2, 4, 8, 16)
POOL_G = POOL_W // len(WINDOWS)
POOL_BUF = max(WINDOWS) - 1
S5_W = 512
S5_G, S5_CH, S5_P = 32, 16, 64
S5_N = S5_G * S5_P
MAIN_W = RW_PAD + POOL_W + S5_W
N_EXP, TOP_K = 32, 4
SW_LIMIT, SW_ALPHA = 7.0, 1.702
NORM_EPS, GN_EPS = 1e-5, 64e-5
MOD_ROWS = 128


def _params(sem):
    return pltpu.CompilerParams(dimension_semantics=sem, vmem_limit_bytes=VMEM_LIMIT)


def _dot(a, b):
    return jnp.dot(a, b, preferred_element_type=F32)


def _segsum(x, ones_bd):
    hi = x.astype(BF16)
    lo = (x - hi.astype(F32)).astype(BF16)
    return _dot(hi, ones_bd) + _dot(lo, ones_bd)


def _sublane_transpose8(xs):
    xs = list(xs)
    sub = lax.broadcasted_iota(jnp.int32, xs[0].shape, 0)
    for d in (4, 2, 1):
        keep = (sub & d) == 0
        for k in range(SUBLANES):
            if k & d:
                continue
            lo, hi = xs[k], xs[k + d]
            xs[k] = jnp.where(keep, lo, pltpu.roll(hi, d, axis=0))
            xs[k + d] = jnp.where(keep, pltpu.roll(lo, SUBLANES - d, axis=0), hi)
    return xs


def _tile_rows(t, rows):
    return jnp.broadcast_to(t[None], (rows // MOD_ROWS,) + t.shape).reshape(rows, t.shape[-1])


def _stack_rows_kernel(a_ref, b_ref, o_ref):
    i = pl.program_id(0)

    @pl.when(i < pl.num_programs(0) - 1)
    def _():
        o_ref[...] = a_ref[...]

    @pl.when(i == pl.num_programs(0) - 1)
    def _():
        o_ref[...] = b_ref[...]


def _stack_rows(a, b):
    tm, d = b.shape
    n_a = a.shape[0] // tm
    return pl.pallas_call(
        _stack_rows_kernel,
        grid=(n_a + 1,),
        in_specs=[pl.BlockSpec((tm, d), lambda i: (jnp.minimum(i, n_a - 1), 0)),
                  pl.BlockSpec((tm, d), lambda i: (0, 0))],
        out_specs=pl.BlockSpec((tm, d), lambda i: (i, 0)),
        out_shape=jax.ShapeDtypeStruct((a.shape[0] + tm, d), a.dtype),
        compiler_params=_params(("arbitrary",)),
        name="stack_rows",
    )(a, b)


def _ada_kernel(c_ref, w_ref, b_ref, o_ref):
    c = c_ref[...]
    a = (c * jax.nn.sigmoid(c)).astype(BF16)
    o_ref[0] = _dot(a, w_ref[0].astype(BF16)) + b_ref[0]


def _adaln(c_all, w_ada, b_ada):
    nl, d, n = w_ada.shape
    r = c_all.shape[0]
    tn = 1024
    return pl.pallas_call(
        _ada_kernel,
        grid=(nl, n // tn),
        in_specs=[pl.BlockSpec((r, d), lambda l, j: (0, 0)),
                  pl.BlockSpec((1, d, tn), lambda l, j: (l, 0, j)),
                  pl.BlockSpec((1, 1, tn), lambda l, j: (l, 0, j))],
        out_specs=pl.BlockSpec((1, r, tn), lambda l, j: (l, 0, j)),
        out_shape=jax.ShapeDtypeStruct((nl, r, n), F32),
        compiler_params=_params(("parallel", "parallel")),
        name="adaln",
    )(c_all, w_ada, b_ada.reshape(nl, 1, n))


def _norm_mod(x, g, scale, shift):
    y = x * lax.rsqrt(jnp.mean(x * x, axis=-1, keepdims=True) + NORM_EPS) * g
    rows = x.shape[0]
    return y * (1.0 + _tile_rows(scale, rows)) + _tile_rows(shift, rows)


def _inproj_kernel(x_ref, g_ref, sc_ref, sh_ref, w_ref, o_ref, h_scr):
    @pl.when(pl.program_id(1) == 0)
    def _():
        h_scr[...] = _norm_mod(x_ref[...], g_ref[...], sc_ref[...], sh_ref[...]).astype(BF16)

    o_ref[...] = _dot(h_scr[...], w_ref[...])


def _inproj(x, g, scale_t, shift_t, w, mod_idx, tm):
    m, d = x.shape
    n = w.shape[1]
    tn = 512
    return pl.pallas_call(
        _inproj_kernel,
        grid=(m // tm, n // tn),
        in_specs=[pl.BlockSpec((tm, d), lambda i, j: (i, 0)),
                  pl.BlockSpec((1, d), lambda i, j: (0, 0)),
                  pl.BlockSpec((MOD_ROWS, d), lambda i, j: (mod_idx(i), 0)),
                  pl.BlockSpec((MOD_ROWS, d), lambda i, j: (mod_idx(i), 0)),
                  pl.BlockSpec((d, tn), lambda i, j: (0, j))],
        out_specs=pl.BlockSpec((tm, tn), lambda i, j: (i, j)),
        out_shape=jax.ShapeDtypeStruct((m, n), F32),
        scratch_shapes=[pltpu.VMEM((tm, d), BF16)],
        compiler_params=_params(("parallel", "arbitrary")),
        name="inproj",
    )(x, g, scale_t, shift_t, w)


def _softplus(x):
    return jnp.maximum(x, 0.0) + jnp.log1p(jnp.exp(-jnp.abs(x)))


RW_TT = 128
HALF = LANES // 2


def _rwkv_pre_math(z, zp, mu_ref, w0_ref, a0_ref, kkw_ref, ka_ref, rk_ref, wl_ref, ones_ref):
    zs = z + (zp - z) * mu_ref[...]
    r, k, v = zs[:, :RW], zs[:, RW:2 * RW], zs[:, 2 * RW:3 * RW]
    l0 = zs[:, 3 * RW:3 * RW + LANES]
    lane = lax.broadcasted_iota(jnp.int32, l0.shape, 1)
    l0 = jnp.where(lane < R_DECAY, jnp.tanh(l0), l0)
    l1 = jax.nn.sigmoid(zs[:, 3 * RW + LANES:])
    lin = jnp.concatenate([l0, l1], axis=1).astype(BF16)
    lo = _dot(lin, wl_ref[...])
    w_log = -_softplus(-(w0_ref[...] + lo[:, :RW])) - 0.5
    a = jax.nn.sigmoid(a0_ref[...] + lo[:, RW:2 * RW])
    ones_bd = ones_ref[...]
    kk = k * kkw_ref[...]
    kkn = kk / jnp.maximum(jnp.sqrt(_segsum(kk * kk, ones_bd)), 1e-12)
    kh = k * (1.0 + (a - 1.0) * ka_ref[...])
    bonus = _segsum(r * kh * rk_ref[...], ones_bd) * v
    return r, jnp.exp(-jnp.exp(w_log)), kh, v, kkn, kkn * a, lo[:, 2 * RW:], bonus


def _rwkv_pre_prompt_kernel(z_ref, prev_ref, mu_ref, w0_ref, a0_ref, kkw_ref, ka_ref, rk_ref, wl_ref,
                            ones_ref, r_o, w_o, k_o, v_o, kk_o, b_o, g_o, bon_o, zt_scr):
    bi = pl.program_id(1)
    z = z_ref[...]
    row = lax.broadcasted_iota(jnp.int32, z.shape, 0)
    zp = jnp.where(row == 0, prev_ref[0], pltpu.roll(z, 1, axis=0))
    r, w, kh, v, kkn, b, g, bonus = _rwkv_pre_math(z, zp, mu_ref, w0_ref, a0_ref, kkw_ref, ka_ref,
                                                   rk_ref, wl_ref, ones_ref)
    g_o[...] = g
    bon_o[...] = bonus
    for n, x in enumerate((r, w, kh, v, kkn, b)):
        for p in range(RW // LANES):
            row0 = pl.multiple_of((bi * (RW // LANES) + p) * LANES, LANES)
            zt_scr[n, pl.ds(row0, LANES), :] = x[:, p * LANES:(p + 1) * LANES].T

    @pl.when(bi == pl.num_programs(1) - 1)
    def _():
        n_bh = zt_scr.shape[1] // HEAD

        def channel_rows(n, c0):
            cols = [[] for _ in range(SUBLANES)]
            for j in range(n_bh // SUBLANES):
                blk = [zt_scr[n, pl.ds(pl.multiple_of((j * SUBLANES + k) * HEAD + c0, SUBLANES), SUBLANES), :]
                       for k in range(SUBLANES)]
                for m, piece in enumerate(_sublane_transpose8(blk)):
                    cols[m].append(piece)
            return [jnp.concatenate(pieces, axis=0) for pieces in cols]

        def store_rows(o_ref, mats, rows_per_t, r0):
            for jt in range(RW_TT // SUBLANES):
                blk = [mt[jt * SUBLANES:(jt + 1) * SUBLANES] for mt in mats]
                for s, piece in enumerate(_sublane_transpose8(blk)):
                    row = pl.multiple_of((jt * SUBLANES + s) * rows_per_t + r0, SUBLANES)
                    o_ref[pl.ds(row, SUBLANES), :] = piece

        for n, o_ref in enumerate((r_o, w_o, k_o, v_o, kk_o, b_o)):
            if o_ref is v_o:
                def group(gi, c, n=n, o_ref=o_ref):
                    c0 = pl.multiple_of(gi * 2 * SUBLANES, 2 * SUBLANES)
                    a = channel_rows(n, c0) + channel_rows(n, c0 + SUBLANES)
                    mats = [jnp.concatenate([a[2 * q], a[2 * q + 1]], axis=0).T for q in range(SUBLANES)]
                    store_rows(o_ref, mats, HEAD // 2, pl.multiple_of(gi * SUBLANES, SUBLANES))
                    return c

                lax.fori_loop(0, HEAD // (2 * SUBLANES), group, 0)
            else:
                def group(gi, c, n=n, o_ref=o_ref):
                    c0 = pl.multiple_of(gi * SUBLANES, SUBLANES)
                    mats = [jnp.concatenate([a, a], axis=0).T for a in channel_rows(n, c0)]
                    store_rows(o_ref, mats, HEAD, c0)
                    return c

                lax.fori_loop(0, HEAD // SUBLANES, group, 0)


def _rwkv_pre_prompt(z_main, prev, n_b, n_t, mu, w0, a0, kkw, ka, rk, wl, ones_bd):
    nt = n_t // RW_TT
    vec = lambda w: pl.BlockSpec((1, w), lambda i, b: (0, 0))
    keyed = pl.BlockSpec((RW_TT * HEAD, LANES), lambda i, b: (i, 0), pipeline_mode=pl.Buffered(1))
    paired = pl.BlockSpec((RW_TT * (HEAD // 2), LANES), lambda i, b: (i, 0), pipeline_mode=pl.Buffered(1))
    keyed_shape = jax.ShapeDtypeStruct((n_t * HEAD, LANES), F32)
    paired_shape = jax.ShapeDtypeStruct((n_t * (HEAD // 2), LANES), F32)
    nat = pl.BlockSpec((RW_TT, RW), lambda i, b: (b * nt + i, 0))
    return pl.pallas_call(
        _rwkv_pre_prompt_kernel,
        grid=(nt, n_b),
        in_specs=[pl.BlockSpec((RW_TT, RW_PAD), lambda i, b: (b * nt + i, 0)),
                  pl.BlockSpec((1, 1, RW_PAD), lambda i, b: (b * nt + i, 0, 0)),
                  vec(RW_PAD), vec(RW), vec(RW), vec(RW), vec(RW), vec(RW),
                  pl.BlockSpec((LORA_IN, 3 * RW), lambda i, b: (0, 0)),
                  pl.BlockSpec((RW, RW), lambda i, b: (0, 0))],
        out_specs=[keyed] * 3 + [paired] + [keyed] * 2 + [nat] * 2,
        out_shape=[keyed_shape] * 3 + [paired_shape] + [keyed_shape] * 2
        + [jax.ShapeDtypeStruct((n_b * n_t, RW), F32)] * 2,
        scratch_shapes=[pltpu.VMEM((6, n_b * RW, RW_TT), F32)],
        compiler_params=_params(("arbitrary", "arbitrary")),
        name="rwkv_pre_prompt",
    )(z_main, prev, mu, w0, a0, kkw, ka, rk, wl, ones_bd)


def _rwkv_pre_sample_kernel(z_ref, zp_ref, st_ref, mu_ref, w0_ref, a0_ref, kkw_ref, ka_ref, rk_ref, wl_ref,
                            ones_ref, r_o, w_o, k_o, v_o, kk_o, b_o, g_o, bon_o):
    zp = jnp.where(pl.program_id(0) == 0, st_ref[...], zp_ref[...])
    r, w, kh, v, kkn, b, g, bonus = _rwkv_pre_math(z_ref[...], zp, mu_ref, w0_ref, a0_ref, kkw_ref, ka_ref,
                                                   rk_ref, wl_ref, ones_ref)
    g_o[...] = g
    bon_o[...] = bonus
    for x, o_ref in zip((r, w, kh, v, kkn, b), (r_o, w_o, k_o, v_o, kk_o, b_o)):
        for p in range(RW // LANES):
            xt = x[:, p * LANES:(p + 1) * LANES].T
            o_ref[0, :, 2 * p * LANES:(2 * p + 1) * LANES] = xt[:HEAD]
            o_ref[0, :, (2 * p + 1) * LANES:(2 * p + 2) * LANES] = xt[HEAD:]


def _rwkv_pre_sample(z_main, state, row0, n_b, n_t, mu, w0, a0, kkw, ka, rk, wl, ones_bd):
    blk0 = row0 // n_b
    vec = lambda w: pl.BlockSpec((1, w), lambda t: (0, 0))
    keyed = pl.BlockSpec((1, HEAD, HEADS * n_b), lambda t: (t, 0, 0))
    nat = pl.BlockSpec((n_b, RW), lambda t: (t, 0))
    return pl.pallas_call(
        _rwkv_pre_sample_kernel,
        grid=(n_t,),
        in_specs=[pl.BlockSpec((n_b, RW_PAD), lambda t: (blk0 + t, 0)),
                  pl.BlockSpec((n_b, RW_PAD), lambda t: (blk0 + jnp.maximum(t - 1, 0), 0)),
                  pl.BlockSpec((n_b, RW_PAD), lambda t: (0, 0)),
                  vec(RW_PAD), vec(RW), vec(RW), vec(RW), vec(RW), vec(RW),
                  pl.BlockSpec((LORA_IN, 3 * RW), lambda t: (0, 0)),
                  pl.BlockSpec((RW, RW), lambda t: (0, 0))],
        out_specs=[keyed] * 6 + [nat] * 2,
        out_shape=[jax.ShapeDtypeStruct((n_t, HEAD, HEADS * n_b), F32)] * 6
        + [jax.ShapeDtypeStruct((n_t * n_b, RW), F32)] * 2,
        compiler_params=_params(("arbitrary",)),
        name="rwkv_pre_sample",
    )(z_main, z_main, state, mu, w0, a0, kkw, ka, rk, wl, ones_bd)


def _rwkv_scan_kernel(nat_state, r_ref, w_ref, k_ref, kk_ref, b_ref, v_ref, s0_ref, y_ref, sT_ref, s_scr):
    tc = pl.program_id(1)
    n_t = r_ref.shape[0]
    n_i = s_scr.shape[0]

    @pl.when(tc == 0)
    def _():
        if nat_state:
            s_scr[...] = s0_ref[...].T.reshape(s_scr.shape)
        else:
            s_scr[...] = s0_ref[...]

    def step(t, carry):
        r_t, w_t, k_t, kk_t, b_t = r_ref[t], w_ref[t], k_ref[t], kk_ref[t], b_ref[t]

        def rows8(ib, c2):
            i0 = pl.multiple_of(ib * SUBLANES, SUBLANES)
            v8 = v_ref[t, pl.ds(i0, SUBLANES), :]
            ys = []
            for ii in range(SUBLANES):
                s_old = s_scr[i0 + ii]
                sa = jnp.sum(s_old * kk_t, axis=0, keepdims=True)
                s_new = s_old * w_t - sa * b_t + v8[ii:ii + 1] * k_t
                ys.append(jnp.sum(s_new * r_t, axis=0, keepdims=True))
                s_scr[i0 + ii] = s_new
            y_ref[t, pl.ds(i0, SUBLANES), :] = jnp.concatenate(ys, axis=0)
            return c2

        return lax.fori_loop(0, n_i // SUBLANES, rows8, carry)

    lax.fori_loop(0, n_t, step, 0)

    @pl.when(tc == pl.num_programs(1) - 1)
    def _():
        if nat_state:
            sT_ref[...] = s_scr[...].reshape(n_i * s_scr.shape[1], LANES).T
        else:
            sT_ref[...] = s_scr[...]


def _rwkv_scan_prompt(r, w, k, kk, b, v, s0, tchunk):
    n_t = r.shape[0]
    n_i = s0.shape[0]
    op = pl.BlockSpec((tchunk, HEAD, LANES), lambda l, t: (t, 0, 0))
    vy = pl.BlockSpec((tchunk, n_i, LANES), lambda l, t: (t, 0, 0))
    st = pl.BlockSpec((n_i, HEAD, LANES), lambda l, t: (0, 0, 0))
    return pl.pallas_call(
        functools.partial(_rwkv_scan_kernel, False),
        grid=(1, n_t // tchunk),
        in_specs=[op] * 5 + [vy, st],
        out_specs=[vy, st],
        out_shape=[jax.ShapeDtypeStruct((n_t, n_i, LANES), F32),
                   jax.ShapeDtypeStruct((n_i, HEAD, LANES), F32)],
        scratch_shapes=[pltpu.VMEM((n_i, HEAD, LANES), F32)],
        compiler_params=_params(("arbitrary", "arbitrary")),
        name="rwkv_scan_prompt",
    )(r, w, k, kk, b, v, s0)


def _rwkv_scan_sample(r, w, k, kk, b, v, s0, layer):
    n_t, _, n_l = r.shape
    n_b = n_l // HEADS
    op = pl.BlockSpec((n_t, HEAD, LANES), lambda h, t: (0, 0, h))
    st = pl.BlockSpec((n_b, HEAD * HEAD), lambda h, t: (layer, h))
    st_out = pl.BlockSpec((n_b, HEAD * HEAD), lambda h, t: (0, h))
    return pl.pallas_call(
        functools.partial(_rwkv_scan_kernel, True),
        grid=(HEADS, 1),
        in_specs=[op] * 6 + [st],
        out_specs=[op, st_out],
        out_shape=[jax.ShapeDtypeStruct((n_t, HEAD, n_l), F32),
                   jax.ShapeDtypeStruct((n_b, HEADS * HEAD * HEAD), F32)],
        scratch_shapes=[pltpu.VMEM((HEAD, HEAD, LANES), F32)],
        compiler_params=_params(("parallel", "arbitrary")),
        name="rwkv_scan_sample",
    )(r, w, k, kk, b, v, s0)


def _rwkv_post_math(y, bon_ref, g_ref, lg_ref, lb_ref, ones_ref):
    ones_bd = ones_ref[...]
    yc = y - _segsum(y, ones_bd) * (1.0 / HEAD)
    var = _segsum(yc * yc, ones_bd) * (1.0 / HEAD)
    yn = yc * lax.rsqrt(var + GN_EPS) * lg_ref[...] + lb_ref[...]
    return ((yn + bon_ref[...]) * g_ref[...]).astype(BF16)


def _rwkv_post_prompt_kernel(y_ref, bon_ref, g_ref, lg_ref, lb_ref, ones_ref, o_ref, zt_scr):
    bi = pl.program_id(1)

    @pl.when(bi == 0)
    def _():
        n_bh = zt_scr.shape[0] // HEAD

        def group(gi, c):
            q0 = pl.multiple_of(gi * SUBLANES, SUBLANES)
            mats = [[] for _ in range(SUBLANES)]
            for jt in range(RW_TT // SUBLANES):
                blk = [y_ref[pl.ds(pl.multiple_of((jt * SUBLANES + s) * (HEAD // 2) + q0, SUBLANES), SUBLANES), :]
                       for s in range(SUBLANES)]
                for m, piece in enumerate(_sublane_transpose8(blk)):
                    mats[m].append(piece)
            vals = []
            for pieces in mats:
                mt = jnp.concatenate(pieces, axis=0).T
                vals += [mt[:HALF], mt[HALF:]]
            for half in range(2):
                for j in range(n_bh // SUBLANES):
                    blk = [v[j * SUBLANES:(j + 1) * SUBLANES] for v in vals[half * SUBLANES:(half + 1) * SUBLANES]]
                    for k, piece in enumerate(_sublane_transpose8(blk)):
                        row = pl.multiple_of((j * SUBLANES + k) * HEAD + 2 * q0 + half * SUBLANES, SUBLANES)
                        zt_scr[pl.ds(row, SUBLANES), :] = piece
            return c

        lax.fori_loop(0, HEAD // (2 * SUBLANES), group, 0)

    pieces = []
    for p in range(RW // LANES):
        row0 = pl.multiple_of((bi * (RW // LANES) + p) * LANES, LANES)
        pieces.append(zt_scr[pl.ds(row0, LANES), :].T)
    y = jnp.concatenate(pieces, axis=1)
    o_ref[...] = _rwkv_post_math(y, bon_ref, g_ref, lg_ref, lb_ref, ones_ref)


def _rwkv_post_prompt(y, bonus, g, n_b, n_t, lnx_g, lnx_b, ones_bd):
    nt = n_t // RW_TT
    nat = pl.BlockSpec((RW_TT, RW), lambda i, b: (b * nt + i, 0))
    vec = pl.BlockSpec((1, RW), lambda i, b: (0, 0))
    return pl.pallas_call(
        _rwkv_post_prompt_kernel,
        grid=(nt, n_b),
        in_specs=[pl.BlockSpec((RW_TT * (HEAD // 2), LANES), lambda i, b: (i, 0)), nat, nat, vec, vec,
                  pl.BlockSpec((RW, RW), lambda i, b: (0, 0))],
        out_specs=nat,
        out_shape=jax.ShapeDtypeStruct((n_b * n_t, RW), BF16),
        scratch_shapes=[pltpu.VMEM((n_b * RW, RW_TT), F32)],
        compiler_params=_params(("arbitrary", "arbitrary")),
        name="rwkv_post_prompt",
    )(y, bonus, g, lnx_g, lnx_b, ones_bd)


def _rwkv_post_sample_kernel(y_ref, bon_ref, g_ref, lg_ref, lb_ref, ones_ref, o_ref):
    pieces = []
    for p in range(RW // LANES):
        m = jnp.concatenate([y_ref[0, :, 2 * p * LANES:(2 * p + 1) * LANES],
                             y_ref[0, :, (2 * p + 1) * LANES:(2 * p + 2) * LANES]], axis=0)
        pieces.append(m.T)
    y = jnp.concatenate(pieces, axis=1)
    o_ref[...] = _rwkv_post_math(y, bon_ref, g_ref, lg_ref, lb_ref, ones_ref)


def _rwkv_post_sample(y, bonus, g, n_b, n_t, lnx_g, lnx_b, ones_bd):
    nat = pl.BlockSpec((n_b, RW), lambda t: (t, 0))
    vec = pl.BlockSpec((1, RW), lambda t: (0, 0))
    return pl.pallas_call(
        _rwkv_post_sample_kernel,
        grid=(n_t,),
        in_specs=[pl.BlockSpec((1, HEAD, HEADS * n_b), lambda t: (t, 0, 0)), nat, nat, vec, vec,
                  pl.BlockSpec((RW, RW), lambda t: (0, 0))],
        out_specs=nat,
        out_shape=jax.ShapeDtypeStruct((n_t * n_b, RW), BF16),
        compiler_params=_params(("parallel",)),
        name="rwkv_post_sample",
    )(y, bonus, g, lnx_g, lnx_b, ones_bd)


def _pool_prompt_kernel(start_pos, z_ref, pw_ref, ps_ref, y_ref):
    n = z_ref.shape[0]
    row = lax.broadcasted_iota(jnp.int32, (n, POOL_G), 0)
    for gi, win in enumerate(WINDOWS):
        sl = slice(gi * POOL_G, (gi + 1) * POOL_G)
        x = z_ref[:, sl]
        s = x
        k = 1
        while k < win:
            s = s + jnp.where(row >= k, pltpu.roll(s, k, axis=0), 0.0)
            k *= 2
        cnt = jnp.minimum(win, row + (start_pos + 1)).astype(F32)
        d = s / cnt - x
        y_ref[:, sl] = (_dot(d.astype(BF16), pw_ref[gi]) * ps_ref[:, sl]).astype(BF16)


def _pool_prompt(z_main, n_b, n_t, pw, ps):
    cb = RW_PAD // POOL_W
    return pl.pallas_call(
        functools.partial(_pool_prompt_kernel, 0),
        grid=(n_b,),
        in_specs=[pl.BlockSpec((n_t, POOL_W), lambda b: (b, cb)),
                  pl.BlockSpec((len(WINDOWS), POOL_G, POOL_G), lambda b: (0, 0, 0)),
                  pl.BlockSpec((1, POOL_W), lambda b: (0, 0))],
        out_specs=pl.BlockSpec((n_t, POOL_W), lambda b: (b, 0)),
        out_shape=jax.ShapeDtypeStruct((n_b * n_t, POOL_W), BF16),
        compiler_params=_params(("parallel",)),
        name="pool_prompt",
    )(z_main, pw, ps)


def _pool_sample_kernel(start_pos, n_t, full_ref, pw_ref, ps_ref, y_ref):
    n_b = full_ref.shape[1]
    for gi, win in enumerate(WINDOWS):
        sl = slice(gi * POOL_G, (gi + 1) * POOL_G)
        f = full_ref[:, :, sl]
        s = f
        k = 1
        while k < win:
            s = s[k:] + s[:-k]
            k *= 2
        s = s[s.shape[0] - n_t:]
        x = f[POOL_BUF:]
        pos = start_pos + lax.broadcasted_iota(jnp.int32, s.shape, 0)
        cnt = jnp.minimum(win, pos + 1).astype(F32)
        d = (s / cnt - x).reshape(n_t * n_b, POOL_G)
        y_ref[:, sl] = (_dot(d.astype(BF16), pw_ref[gi]) * ps_ref[:, sl]).astype(BF16)


def _pool_sample(full, n_t, start_pos, pw, ps):
    n_f, n_b, _ = full.shape
    return pl.pallas_call(
        functools.partial(_pool_sample_kernel, start_pos, n_t),
        grid=(1,),
        in_specs=[pl.BlockSpec((n_f, n_b, POOL_W), lambda i: (0, 0, 0)),
                  pl.BlockSpec((len(WINDOWS), POOL_G, POOL_G), lambda i: (0, 0, 0)),
                  pl.BlockSpec((1, POOL_W), lambda i: (0, 0))],
        out_specs=pl.BlockSpec((n_t * n_b, POOL_W), lambda i: (0, 0)),
        out_shape=jax.ShapeDtypeStruct((n_t * n_b, POOL_W), BF16),
        compiler_params=_params(("arbitrary",)),
        name="pool_sample",
    )(full, pw, ps)


def _s5_tail(x_all, u, cmat_ref, d_ref, gw_ref, gb_ref):
    y = _dot(x_all.astype(BF16), cmat_ref[...]) + d_ref[...] * u
    y = jax.nn.gelu(y)
    return (y * jax.nn.sigmoid(_dot(y.astype(BF16), gw_ref[...]) + gb_ref[...])).astype(BF16)


S5_LC = 512


def _s5_prompt_kernel(u_ref, bmat_ref, cmat_ref, lam_ref, pw_ref, d_ref, gw_ref, gb_ref,
                      y_ref, xre_o, xim_o, x_scr, c_scr):
    tc = pl.program_id(1)

    @pl.when(tc == 0)
    def _():
        c_scr[...] = jnp.zeros_like(c_scr)

    u = u_ref[...]
    n = u.shape[0]
    x_scr[...] = _dot(u.astype(BF16), bmat_ref[...])
    row = lax.broadcasted_iota(jnp.int32, (SUBLANES, S5_LC), 0)
    for lc in range(S5_N // S5_LC):
        re_sl = pl.ds(lc * S5_LC, S5_LC)
        im_sl = pl.ds(S5_N + lc * S5_LC, S5_LC)
        lam = [(lam_ref[2 * j:2 * j + 1, re_sl], lam_ref[2 * j + 1:2 * j + 2, re_sl]) for j in range(3)]
        p_re, p_im = pw_ref[:, re_sl], pw_ref[:, im_sl]

        def blk(rb, carry, re_sl=re_sl, im_sl=im_sl, lam=lam, p_re=p_re, p_im=p_im):
            c_re, c_im = carry
            rows = pl.ds(pl.multiple_of(rb * SUBLANES, SUBLANES), SUBLANES)
            xr, xi = x_scr[rows, re_sl], x_scr[rows, im_sl]
            for j, (l_re, l_im) in enumerate(lam):
                kshift = 1 << j
                sr = jnp.where(row >= kshift, pltpu.roll(xr, kshift, axis=0), 0.0)
                si = jnp.where(row >= kshift, pltpu.roll(xi, kshift, axis=0), 0.0)
                xr, xi = xr + (l_re * sr - l_im * si), xi + (l_re * si + l_im * sr)
            xr, xi = xr + (p_re * c_re - p_im * c_im), xi + (p_re * c_im + p_im * c_re)
            x_scr[rows, re_sl] = xr
            x_scr[rows, im_sl] = xi
            return xr[SUBLANES - 1:], xi[SUBLANES - 1:]

        c_re, c_im = lax.fori_loop(0, n // SUBLANES, blk, (c_scr[0:1, re_sl], c_scr[0:1, im_sl]))
        c_scr[0:1, re_sl] = c_re
        c_scr[0:1, im_sl] = c_im

    y_ref[...] = _s5_tail(x_scr[...], u, cmat_ref, d_ref, gw_ref, gb_ref)

    @pl.when(tc == pl.num_programs(1) - 1)
    def _():
        xre_o[0] = c_scr[0:1, :S5_N]
        xim_o[0] = c_scr[0:1, S5_N:]


def _s5_prompt(z_main, n_b, n_t, tt, bmat, cmat, lam_t, pw_t, dskip, gw, gb):
    cb = (RW_PAD + POOL_W) // S5_W
    n_tc = n_t // tt
    const = lambda shape: pl.BlockSpec(shape, lambda b, t: (0,) * len(shape))
    return pl.pallas_call(
        _s5_prompt_kernel,
        grid=(n_b, n_tc),
        in_specs=[pl.BlockSpec((tt, S5_W), lambda b, t: (b * n_tc + t, cb)),
                  const((S5_W, 2 * S5_N)), const((2 * S5_N, S5_W)),
                  const((SUBLANES, S5_N)), const((SUBLANES, 2 * S5_N)),
                  const((1, S5_W)), const((S5_W, S5_W)), const((1, S5_W))],
        out_specs=[pl.BlockSpec((tt, S5_W), lambda b, t: (b * n_tc + t, 0)),
                   pl.BlockSpec((1, 1, S5_N), lambda b, t: (b, 0, 0)),
                   pl.BlockSpec((1, 1, S5_N), lambda b, t: (b, 0, 0))],
        out_shape=[jax.ShapeDtypeStruct((n_b * n_t, S5_W), BF16),
                   jax.ShapeDtypeStruct((n_b, 1, S5_N), F32),
                   jax.ShapeDtypeStruct((n_b, 1, S5_N), F32)],
        scratch_shapes=[pltpu.VMEM((tt, 2 * S5_N), F32), pltpu.VMEM((SUBLANES, 2 * S5_N), F32)],
        compiler_params=_params(("parallel", "arbitrary")),
        name="s5_prompt",
    )(z_main, bmat, cmat, lam_t, pw_t, dskip, gw, gb)


def _s5_sample_kernel(n_t, u_ref, x0re_ref, x0im_ref, bmat_ref, cmat_ref, lam_ref, d_ref, gw_ref,
                      gb_ref, y_ref, xre_o, xim_o, x_scr):
    u = u_ref[...]
    n_b = u.shape[0] // n_t
    x_scr[...] = _dot(u.astype(BF16), bmat_ref[...])
    for lc in range(S5_N // S5_LC):
        re_sl = pl.ds(lc * S5_LC, S5_LC)
        im_sl = pl.ds(S5_N + lc * S5_LC, S5_LC)
        l_re, l_im = lam_ref[0:1, re_sl], lam_ref[1:2, re_sl]
        xr, xi = x0re_ref[:, re_sl], x0im_ref[:, re_sl]
        for t in range(n_t):
            rows = pl.ds(t * n_b, n_b)
            xr, xi = (l_re * xr - l_im * xi + x_scr[rows, re_sl],
                      l_re * xi + l_im * xr + x_scr[rows, im_sl])
            x_scr[rows, re_sl] = xr
            x_scr[rows, im_sl] = xi
        xre_o[:, re_sl] = xr
        xim_o[:, re_sl] = xi
    y_ref[...] = _s5_tail(x_scr[...], u, cmat_ref, d_ref, gw_ref, gb_ref)


def _s5_sample(z_main, row0, n_b, n_t, x0re, x0im, bmat, cmat, lam_t, dskip, gw, gb):
    rows = n_b * n_t
    cb = (RW_PAD + POOL_W) // S5_W
    const = lambda shape: pl.BlockSpec(shape, lambda i: (0,) * len(shape))
    return pl.pallas_call(
        functools.partial(_s5_sample_kernel, n_t),
        grid=(1,),
        in_specs=[pl.BlockSpec((rows, S5_W), lambda i: (row0 // rows, cb)),
                  const((n_b, S5_N)), const((n_b, S5_N)),
                  const((S5_W, 2 * S5_N)), const((2 * S5_N, S5_W)), const((SUBLANES, S5_N)),
                  const((1, S5_W)), const((S5_W, S5_W)), const((1, S5_W))],
        out_specs=[const((rows, S5_W)), const((n_b, S5_N)), const((n_b, S5_N))],
        out_shape=[jax.ShapeDtypeStruct((rows, S5_W), BF16),
                   jax.ShapeDtypeStruct((n_b, S5_N), F32),
                   jax.ShapeDtypeStruct((n_b, S5_N), F32)],
        scratch_shapes=[pltpu.VMEM((rows, 2 * S5_N), F32)],
        compiler_params=_params(("arbitrary",)),
        name="s5_sample",
    )(z_main, x0re, x0im, bmat, cmat, lam_t, dskip, gw, gb)


def _merge_kernel(ya_ref, yb_ref, yc_ref, zg_ref, wbr_ref, wout_ref, x_ref, g1_ref, n2_ref,
                  sc2_ref, sh2_ref, rw_ref, rb_ref, xo_ref, h2_ref, lg_ref):
    d = x_ref.shape[1]
    rows = x_ref.shape[0]
    m = jax.nn.sigmoid(zg_ref[:, :d]) * _dot(ya_ref[...], wbr_ref[:RW, :])
    m = m + jax.nn.sigmoid(zg_ref[:, d:2 * d]) * _dot(yb_ref[...], wbr_ref[RW:RW + POOL_W, :])
    m = m + jax.nn.sigmoid(zg_ref[:, 2 * d:]) * _dot(yc_ref[...], wbr_ref[RW + POOL_W:, :])
    xn = x_ref[...] + _tile_rows(g1_ref[...], rows) * _dot(m.astype(BF16), wout_ref[...])
    xo_ref[...] = xn
    h2 = _norm_mod(xn, n2_ref[...], sc2_ref[...], sh2_ref[...])
    h2_ref[...] = h2
    lg_ref[...] = jnp.dot(h2, rw_ref[...], precision=lax.Precision.HIGHEST,
                          preferred_element_type=F32) + rb_ref[...]


def _merge(ya, yb, yc, zg, wbr, wout, x, g1_t, n2g, sc2_t, sh2_t, rw, rb, mod_idx, tm):
    m, d = x.shape
    row = lambda w: pl.BlockSpec((tm, w), lambda i: (i, 0))
    const = lambda shape: pl.BlockSpec(shape, lambda i: (0,) * len(shape), pipeline_mode=pl.Buffered(1))
    mod = pl.BlockSpec((MOD_ROWS, d), lambda i: (mod_idx(i), 0))
    return pl.pallas_call(
        _merge_kernel,
        grid=(m // tm,),
        in_specs=[row(RW), row(POOL_W), row(S5_W), row(3 * d), const(wbr.shape), const(wout.shape),
                  row(d), mod, const((1, d)), mod, mod, const(rw.shape), const((1, LANES))],
        out_specs=[row(d), row(d), row(LANES)],
        out_shape=[jax.ShapeDtypeStruct((m, d), F32), jax.ShapeDtypeStruct((m, d), F32),
                   jax.ShapeDtypeStruct((m, LANES), F32)],
        compiler_params=_params(("parallel",)),
        name="merge",
    )(ya, yb, yc, zg, wbr, wout, x, g1_t, n2g, sc2_t, sh2_t, rw, rb)


MOE_BM = 512
MOE_TF = 512


def _moe_kernel(be_ref, nb_ref, x_ref, wg_ref, wu_ref, bg_ref, bu_ref, wd_ref, bd_ref, o_ref):
    s, f = pl.program_id(0), pl.program_id(1)
    used = s < nb_ref[0]

    @pl.when(jnp.logical_and(jnp.logical_not(used), f == 0))
    def _():
        o_ref[...] = jnp.zeros_like(o_ref)

    @pl.when(used)
    def _():
        x = x_ref[...].astype(BF16)
        g = _dot(x, wg_ref[0].astype(BF16)) + bg_ref[0]
        u = _dot(x, wu_ref[0].astype(BF16)) + bu_ref[0]
        g = jnp.minimum(g, SW_LIMIT)
        u = jnp.clip(u, -SW_LIMIT, SW_LIMIT)
        act = (u + 1.0) * (g * jax.nn.sigmoid(SW_ALPHA * g))
        part = _dot(act.astype(BF16), wd_ref[0].astype(BF16))

        @pl.when(f == 0)
        def _():
            o_ref[...] = part + bd_ref[0]

        @pl.when(f > 0)
        def _():
            o_ref[...] += part


def _moe(xs, blk_e, n_used, layer, w_gu, b_gu, w_down, b_down):
    rows, d = xs.shape
    n_l, n_e, _, two_ff = w_gu.shape
    dff = two_ff // 2
    nf = dff // MOE_TF
    n_blk = rows // MOE_BM
    e0 = layer * n_e

    def f_eff(s, f, nb):
        return jnp.where(s < nb[0], f, nf - 1)

    grid_spec = pltpu.PrefetchScalarGridSpec(
        num_scalar_prefetch=2,
        grid=(n_blk, nf),
        in_specs=[
            pl.BlockSpec((MOE_BM, d), lambda s, f, be, nb: (jnp.minimum(s, nb[0] - 1), 0)),
            pl.BlockSpec((1, d, MOE_TF), lambda s, f, be, nb: (e0 + be[s], 0, f_eff(s, f, nb))),
            pl.BlockSpec((1, d, MOE_TF), lambda s, f, be, nb: (e0 + be[s], 0, nf + f_eff(s, f, nb))),
            pl.BlockSpec((1, 1, MOE_TF), lambda s, f, be, nb: (e0 + be[s], 0, f_eff(s, f, nb))),
            pl.BlockSpec((1, 1, MOE_TF), lambda s, f, be, nb: (e0 + be[s], 0, nf + f_eff(s, f, nb))),
            pl.BlockSpec((1, MOE_TF, d), lambda s, f, be, nb: (e0 + be[s], f_eff(s, f, nb), 0)),
            pl.BlockSpec((1, 1, d), lambda s, f, be, nb: (e0 + be[s], 0, 0)),
        ],
        out_specs=pl.BlockSpec((MOE_BM, d), lambda s, f, be, nb: (s, 0)),
    )
    w_gu = w_gu.reshape(n_l * n_e, d, two_ff)
    b_gu = b_gu.reshape(n_l * n_e, 1, two_ff)
    return pl.pallas_call(
        _moe_kernel,
        grid_spec=grid_spec,
        out_shape=jax.ShapeDtypeStruct((rows, d), F32),
        compiler_params=_params(("arbitrary", "arbitrary")),
        name="moe",
    )(blk_e, n_used, xs, w_gu, w_gu, b_gu, b_gu, w_down.reshape(n_l * n_e, dff, d),
      b_down.reshape(n_l * n_e, 1, d))


def _route(logits, n_rows_pad):
    n_tok = logits.shape[0]
    top_v, top_e = lax.top_k(logits, TOP_K)
    gate = jax.nn.softmax(top_v, axis=-1)
    flat_e = top_e.reshape(-1)
    onehot = (flat_e[:, None] == jnp.arange(N_EXP, dtype=jnp.int32)[None, :]).astype(jnp.int32)
    csum = jnp.cumsum(onehot, axis=0)
    counts = csum[-1]
    rank = jnp.sum((csum - 1) * onehot, axis=1)
    padded = (counts + MOE_BM - 1) // MOE_BM * MOE_BM
    pad_end = jnp.cumsum(padded)
    pad_start = pad_end - padded
    dest = (pad_start[flat_e] + rank).astype(jnp.int32)
    flat_tok = jnp.arange(n_tok * TOP_K, dtype=jnp.int32) // TOP_K
    rows_tok = jnp.zeros((n_rows_pad,), jnp.int32).at[dest].set(flat_tok)
    n_rows_used = pad_end[-1].astype(jnp.int32)
    starts = jnp.arange(n_rows_pad // MOE_BM, dtype=jnp.int32) * MOE_BM
    starts = jnp.minimum(starts, n_rows_used - MOE_BM)
    blk_e = jnp.minimum(jnp.searchsorted(pad_end, starts, side='right'), N_EXP - 1).astype(jnp.int32)
    return gate, dest.reshape(n_tok, TOP_K), rows_tok, blk_e, n_rows_used.reshape(1)


def _combine_kernel(x_ref, y0_ref, y1_ref, y2_ref, y3_ref, gw_ref, g2_ref, o_ref):
    rows = x_ref.shape[0]
    gw = gw_ref[...]
    acc = gw[:, 0:1] * y0_ref[...]
    for k, y_ref in enumerate((y1_ref, y2_ref, y3_ref), start=1):
        acc = acc + gw[:, k:k + 1] * y_ref[...]
    o_ref[...] = x_ref[...] + _tile_rows(g2_ref[...], rows) * acc


def _combine(x, yg, gw, g2_t, mod_idx, tm):
    m, d = x.shape
    n_i = m // tm
    slab = lambda k: pl.BlockSpec((tm, d), lambda i: (k * n_i + i, 0))
    return pl.pallas_call(
        _combine_kernel,
        grid=(n_i,),
        in_specs=[pl.BlockSpec((tm, d), lambda i: (i, 0)), slab(0), slab(1), slab(2), slab(3),
                  pl.BlockSpec((tm, TOP_K), lambda i: (i, 0)),
                  pl.BlockSpec((MOD_ROWS, d), lambda i: (mod_idx(i), 0))],
        out_specs=pl.BlockSpec((tm, d), lambda i: (i, 0)),
        out_shape=jax.ShapeDtypeStruct((m, d), F32),
        compiler_params=_params(("parallel",)),
        name="combine",
    )(x, yg, yg, yg, yg, gw, g2_t)


def _final_norm_kernel(x_ref, g_ref, o_ref):
    x = x_ref[...]
    o_ref[...] = x * lax.rsqrt(jnp.mean(x * x, axis=-1, keepdims=True) + NORM_EPS) * g_ref[...]


def _final_norm(x, g, tm):
    m, d = x.shape
    return pl.pallas_call(
        _final_norm_kernel,
        grid=(m // tm,),
        in_specs=[pl.BlockSpec((tm, d), lambda i: (i, 0)), pl.BlockSpec((1, d), lambda i: (0, 0))],
        out_specs=pl.BlockSpec((tm, d), lambda i: (i, 0)),
        out_shape=jax.ShapeDtypeStruct((m, d), F32),
        compiler_params=_params(("parallel",)),
        name="final_norm",
    )(x, g)


def _s5_params(a_re, a_im, log_dt, b_re, b_im, c_re, c_im):
    dt = jnp.exp(log_dt)[:, None]
    mag = jnp.exp(a_re * dt)
    lb_re, lb_im = mag * jnp.cos(a_im * dt), mag * jnp.sin(a_im * dt)
    nr, ni = lb_re - 1.0, lb_im
    den = a_re * a_re + a_im * a_im
    f_re = (nr * a_re + ni * a_im) / den
    f_im = (ni * a_re - nr * a_im) / den
    bb_re = f_re[..., None] * b_re - f_im[..., None] * b_im
    bb_im = f_re[..., None] * b_im + f_im[..., None] * b_re
    eye = jnp.eye(S5_G, dtype=F32)
    bd_in = lambda w: jnp.einsum('gpc,gh->gchp', w, eye).reshape(S5_W, S5_N)
    bd_out = lambda w: jnp.einsum('gcp,gh->gphc', w, eye).reshape(S5_N, S5_W)
    bmat = jnp.concatenate([bd_in(bb_re), bd_in(bb_im)], axis=1).astype(BF16)
    cmat = jnp.concatenate([bd_out(c_re), -bd_out(c_im)], axis=0).astype(BF16)
    l_re, l_im = lb_re.reshape(1, S5_N), lb_im.reshape(1, S5_N)
    pows = [(l_re, l_im)]
    for _ in range(SUBLANES - 1):
        p_re, p_im = pows[-1]
        pows.append((p_re * l_re - p_im * l_im, p_re * l_im + p_im * l_re))
    lam_t = jnp.concatenate([pows[0][0], pows[0][1], pows[1][0], pows[1][1], pows[3][0], pows[3][1],
                             jnp.zeros((2, S5_N), F32)], axis=0)
    pw_t = jnp.concatenate([jnp.concatenate([p[0] for p in pows], axis=0),
                            jnp.concatenate([p[1] for p in pows], axis=0)], axis=1)
    return bmat, cmat, lam_t, pw_t


def _pad_cols(w, width):
    return jnp.pad(w, ((0, 0), (0, width - w.shape[1])))


def kernel(x_prompt, x_sample, c_prompt, c_sample, state_wkv, state_shift, state_pool, state_s5_re, state_s5_im, norm1_g, norm2_g, final_norm_g, w_ada, b_ada, w_in, rw_mu, rw_w0, rw_w2, rw_a0, rw_a2, rw_g2, rw_kk, rw_ka, rw_rk, rw_lnx_g, rw_lnx_b, pool_w, pool_scale, s5_a_re, s5_a_im, s5_log_dt, s5_b_re, s5_b_im, s5_c_re, s5_c_im, s5_d, s5_glu_w, s5_glu_b, w_br, w_out, router_w, router_b, moe_w_gu, moe_b_gu, moe_w_down, moe_b_down):
    bp, lp, d = x_prompt.shape
    bs, ls, _ = x_sample.shape
    depth = w_in.shape[0]
    mp, ms = bp * lp, bs * ls
    m = mp + ms
    past_len = 16384
    assert bs == MOD_ROWS and bp * HEADS * 2 == LANES and lp % 1024 == 0 and ms == 1024

    def mod_idx_for(tm):
        n_p = mp // tm
        return lambda i: jnp.where(i < n_p, (i * tm) // lp, bp)

    tm_big, tm_mid = 1024, 256
    ones_bd = jnp.kron(jnp.eye(HEADS, dtype=F32), jnp.ones((HEAD, HEAD), F32)).astype(BF16)

    x = _stack_rows(x_prompt.reshape(mp, d), jnp.swapaxes(x_sample, 0, 1).reshape(ms, d))
    c_all = jnp.concatenate([c_prompt, c_sample, jnp.zeros((4, d), F32)], axis=0)
    mod = _adaln(c_all, w_ada, b_ada)

    def table(layer, idx):
        v = mod[layer, :, idx * d:(idx + 1) * d]
        return jnp.concatenate([jnp.repeat(v[:bp], MOD_ROWS, axis=0), v[bp:bp + bs]], axis=0)

    outs_p, outs_s = [], []
    for l in range(depth):
        shift1, scale1, gate1, shift2, scale2, gate2 = (table(l, i) for i in range(6))

        w_l = w_in[l]
        w_main = jnp.concatenate([_pad_cols(w_l[:, :RW_PROJ], RW_PAD),
                                  w_l[:, RW_PROJ:RW_PROJ + POOL_W + S5_W]], axis=1).astype(BF16)
        w_gate = w_l[:, RW_PROJ + POOL_W + S5_W:].astype(BF16)
        n1 = norm1_g[l].reshape(1, d)
        z_main = _inproj(x, n1, scale1, shift1, w_main, mod_idx_for(tm_big), tm_big)
        z_gate = _inproj(x, n1, scale1, shift1, w_gate, mod_idx_for(tm_big), tm_big)

        mu = _pad_cols(rw_mu[l].reshape(1, RW_PROJ), RW_PAD)
        wl = jnp.zeros((LORA_IN, 3 * RW), F32)
        wl = wl.at[:R_DECAY, :RW].set(rw_w2[l])
        wl = wl.at[R_DECAY:R_DECAY + R_AAA, RW:2 * RW].set(rw_a2[l])
        wl = wl.at[R_DECAY + R_AAA:R_DECAY + R_AAA + R_GATE, 2 * RW:].set(rw_g2[l]).astype(BF16)
        vecs = [v.reshape(1, RW) for v in (rw_w0[l], rw_a0[l], rw_kk[l], rw_ka[l], rw_rk[l])]
        lnx_g, lnx_b = rw_lnx_g[l].reshape(1, RW), rw_lnx_b[l].reshape(1, RW)
        n_tp = mp // RW_TT
        last_rows = z_main[RW_TT - 1:mp:RW_TT, :RW_PAD]
        prev_p = jnp.concatenate([jnp.zeros((1, RW_PAD), F32), last_rows[:-1]], axis=0)
        first_of_seq = (jnp.arange(n_tp) % (lp // RW_TT) == 0)[:, None]
        prev_p = jnp.where(first_of_seq, 0.0, prev_p).reshape(n_tp, 1, RW_PAD)
        *ops_p, g_p, bon_p = _rwkv_pre_prompt(z_main, prev_p, bp, lp, mu, *vecs, wl, ones_bd)
        ops_p = [a.reshape(lp, a.shape[0] // lp, LANES) for a in ops_p]
        y_pt, st_p = _rwkv_scan_prompt(*ops_p[:3], *ops_p[4:], ops_p[3],
                                       jnp.zeros((HEAD // 2, HEAD, LANES), F32), 64)
        y_a_p = _rwkv_post_prompt(y_pt.reshape(lp * (HEAD // 2), LANES), bon_p, g_p, bp, lp,
                                  lnx_g, lnx_b, ones_bd)
        new_wkv_p = st_p.reshape(HEAD // 2, HEAD, 2, bp, HEADS).transpose(3, 4, 0, 2, 1)
        new_wkv_p = new_wkv_p.reshape(bp, HEADS, HEAD, HEAD)

        *ops_s, g_s, bon_s = _rwkv_pre_sample(z_main, _pad_cols(state_shift[l], RW_PAD), mp, bs, ls,
                                              mu, *vecs, wl, ones_bd)
        y_st, st_s = _rwkv_scan_sample(*ops_s[:3], *ops_s[4:], ops_s[3],
                                       state_wkv.reshape(depth * bs, HEADS * HEAD * HEAD), l)
        y_a_s = _rwkv_post_sample(y_st, bon_s, g_s, bs, ls, lnx_g, lnx_b, ones_bd)
        new_wkv_s = st_s.reshape(bs, HEADS, HEAD, HEAD)
        y_a = jnp.concatenate([y_a_p, y_a_s], axis=0)
        new_shift_p = z_main[lp - 1:mp:lp, :RW_PROJ]
        new_shift_s = z_main[m - bs:, :RW_PROJ]

        pw = pool_w[l].astype(BF16)
        ps = pool_scale[l].reshape(1, POOL_W)
        y_b_p = _pool_prompt(z_main, bp, lp, pw, ps)
        zb_s = z_main[mp:, RW_PAD:RW_PAD + POOL_W].reshape(ls, bs, POOL_W)
        full_s = jnp.concatenate([jnp.swapaxes(state_pool[l], 0, 1), zb_s], axis=0)
        y_b_s = _pool_sample(full_s, ls, past_len, pw, ps)
        y_b = jnp.concatenate([y_b_p, y_b_s], axis=0)
        new_pool_p = z_main[:mp, RW_PAD:RW_PAD + POOL_W].reshape(bp, lp, POOL_W)[:, lp - POOL_BUF:]
        new_pool_s = jnp.swapaxes(full_s[full_s.shape[0] - POOL_BUF:], 0, 1)

        bmat, cmat, lam_t, pw_t = _s5_params(s5_a_re[l], s5_a_im[l], s5_log_dt[l], s5_b_re[l], s5_b_im[l],
                                             s5_c_re[l], s5_c_im[l])
        dskip, gw, gb = s5_d[l].reshape(1, S5_W), s5_glu_w[l].astype(BF16), s5_glu_b[l].reshape(1, S5_W)
        y_c_p, re_p, im_p = _s5_prompt(z_main, bp, lp, 512, bmat, cmat, lam_t, pw_t, dskip, gw, gb)
        y_c_s, re_s, im_s = _s5_sample(z_main, mp, bs, ls, state_s5_re[l].reshape(bs, S5_N),
                                       state_s5_im[l].reshape(bs, S5_N), bmat, cmat, lam_t, dskip, gw, gb)
        y_c = jnp.concatenate([y_c_p, y_c_s], axis=0)

        rw_pad = _pad_cols(router_w[l], LANES)
        rb_pad = jnp.concatenate([router_b[l], jnp.full((LANES - N_EXP,), -1e30, F32)]).reshape(1, LANES)
        x, h2, logits = _merge(y_a, y_b, y_c, z_gate, w_br[l].astype(BF16), w_out[l].astype(BF16), x, gate1,
                               norm2_g[l].reshape(1, d), scale2, shift2, rw_pad, rb_pad,
                               mod_idx_for(tm_mid), tm_mid)

        n_rows_pad = (m * TOP_K // MOE_BM + N_EXP) * MOE_BM
        gate_w, dest, rows_tok, blk_e, n_rows_used = _route(logits[:, :N_EXP], n_rows_pad)
        yb = _moe(h2[rows_tok], blk_e, n_rows_used // MOE_BM, l, moe_w_gu, moe_b_gu, moe_w_down, moe_b_down)
        yg = yb[dest.T.reshape(-1)]
        x = _combine(x, yg, gate_w, gate2, mod_idx_for(tm_mid), tm_mid)

        outs_p.append((new_shift_p, new_wkv_p, new_pool_p, re_p.reshape(bp, S5_G, S5_P),
                       im_p.reshape(bp, S5_G, S5_P)))
        outs_s.append((new_shift_s, new_wkv_s, new_pool_s, re_s.reshape(bs, S5_G, S5_P),
                       im_s.reshape(bs, S5_G, S5_P)))

    y = _final_norm(x, final_norm_g.reshape(1, d), tm_mid)
    y_prompt = y[:mp].reshape(bp, lp, d)
    y_sample = jnp.swapaxes(y[mp:].reshape(ls, bs, d), 0, 1)
    p_shift, p_wkv, p_pool, p_re, p_im = (jnp.stack([o[j] for o in outs_p]) for j in range(5))
    s_shift, s_wkv, s_pool, s_re, s_im = (jnp.stack([o[j] for o in outs_s]) for j in range(5))
    return (y_prompt, y_sample, p_wkv, p_shift, p_pool, p_re, p_im,
            s_wkv, s_shift, s_pool, s_re, s_im)
```

```python
import functools

import jax
import jax.numpy as jnp
from jax import lax
from jax.experimental import pallas as pl
from jax.experimental.pallas import tpu as pltpu

F32, BF16 = jnp.float32, jnp.bfloat16

LANES = 128
SUBLANES = 8
VMEM_LIMIT = 56 * 1024 * 1024

HEAD = 64
HEADS = 16
RW = HEAD * HEADS
R_DECAY, R_AAA, R_GATE = 64, 64, 160
RW_PROJ = 3 * RW + R_DECAY + R_AAA + R_GATE
RW_PAD = 3584
LORA_IN = RW_PAD - 3 * RW
POOL_W = 512
WINDOWS = (2, 4, 8, 16)
POOL_G = POOL_W // len(WINDOWS)
POOL_BUF = max(WINDOWS) - 1
S5_W = 512
S5_G, S5_CH, S5_P = 32, 16, 64
S5_N = S5_G * S5_P
MAIN_W = RW_PAD + POOL_W + S5_W
N_EXP, TOP_K = 32, 4
SW_LIMIT, SW_ALPHA = 7.0, 1.702
NORM_EPS, GN_EPS = 1e-5, 64e-5
MOD_ROWS = 128


def _params(sem):
    return pltpu.CompilerParams(dimension_semantics=sem, vmem_limit_bytes=VMEM_LIMIT)


def _dot(a, b):
    return jnp.dot(a, b, preferred_element_type=F32)


def _segsum(x, ones_bd):
    def split(v):
        hi = v.astype(BF16)
        return hi, (v - hi.astype(F32)).astype(BF16)

    nt = (((1,), (1,)), ((), ()))
    hi, lo = split(x)
    sums = (lax.dot_general(hi, ones_bd, nt, preferred_element_type=F32)
            + lax.dot_general(lo, ones_bd, nt, preferred_element_type=F32))
    hi, lo = split(sums)
    return _dot(hi, ones_bd) + _dot(lo, ones_bd)


def _sublane_transpose8(xs):
    xs = list(xs)
    sub = lax.broadcasted_iota(jnp.int32, xs[0].shape, 0)
    for d in (4, 2, 1):
        keep = (sub & d) == 0
        for k in range(SUBLANES):
            if k & d:
                continue
            lo, hi = xs[k], xs[k + d]
            xs[k] = jnp.where(keep, lo, pltpu.roll(hi, d, axis=0))
            xs[k + d] = jnp.where(keep, pltpu.roll(lo, SUBLANES - d, axis=0), hi)
    return xs


def _tile_rows(t, rows):
    return jnp.broadcast_to(t[None], (rows // MOD_ROWS,) + t.shape).reshape(rows, t.shape[-1])


def _stack_rows_kernel(a_ref, b_ref, o_ref):
    i = pl.program_id(0)

    @pl.when(i < pl.num_programs(0) - 1)
    def _():
        o_ref[...] = a_ref[...]

    @pl.when(i == pl.num_programs(0) - 1)
    def _():
        o_ref[...] = b_ref[...]


def _stack_rows(a, b):
    tm, d = b.shape
    n_a = a.shape[0] // tm
    return pl.pallas_call(
        _stack_rows_kernel,
        grid=(n_a + 1,),
        in_specs=[pl.BlockSpec((tm, d), lambda i: (jnp.minimum(i, n_a - 1), 0)),
                  pl.BlockSpec((tm, d), lambda i: (0, 0))],
        out_specs=pl.BlockSpec((tm, d), lambda i: (i, 0)),
        out_shape=jax.ShapeDtypeStruct((a.shape[0] + tm, d), a.dtype),
        compiler_params=_params(("arbitrary",)),
        name="stack_rows",
    )(a, b)


def _ada_kernel(c_ref, w_ref, b_ref, o_ref):
    c = c_ref[...]
    a = (c * jax.nn.sigmoid(c)).astype(BF16)
    o_ref[0] = _dot(a, w_ref[0].astype(BF16)) + b_ref[0]


def _adaln(c_all, w_ada, b_ada):
    nl, d, n = w_ada.shape
    r = c_all.shape[0]
    tn = 1024
    return pl.pallas_call(
        _ada_kernel,
        grid=(nl, n // tn),
        in_specs=[pl.BlockSpec((r, d), lambda l, j: (0, 0)),
                  pl.BlockSpec((1, d, tn), lambda l, j: (l, 0, j)),
                  pl.BlockSpec((1, 1, tn), lambda l, j: (l, 0, j))],
        out_specs=pl.BlockSpec((1, r, tn), lambda l, j: (l, 0, j)),
        out_shape=jax.ShapeDtypeStruct((nl, r, n), F32),
        compiler_params=_params(("parallel", "parallel")),
        name="adaln",
    )(c_all, w_ada, b_ada.reshape(nl, 1, n))


def _norm_mod(x, g, scale, shift):
    y = x * lax.rsqrt(jnp.mean(x * x, axis=-1, keepdims=True) + NORM_EPS) * g
    rows = x.shape[0]
    return y * (1.0 + _tile_rows(scale, rows)) + _tile_rows(shift, rows)


def _inproj_kernel(x_ref, g_ref, sc_ref, sh_ref, w_ref, o_ref, h_scr):
    @pl.when(pl.program_id(1) == 0)
    def _():
        h_scr[...] = _norm_mod(x_ref[...], g_ref[...], sc_ref[...], sh_ref[...]).astype(BF16)

    o_ref[...] = _dot(h_scr[...], w_ref[0])


def _inproj(x, g, scale_t, shift_t, w, layer, mod_idx, tm):
    m, d = x.shape
    n = w.shape[2]
    tn = 512
    return pl.pallas_call(
        _inproj_kernel,
        grid=(m // tm, n // tn),
        in_specs=[pl.BlockSpec((tm, d), lambda i, j: (i, 0)),
                  pl.BlockSpec((1, d), lambda i, j: (0, 0)),
                  pl.BlockSpec((MOD_ROWS, d), lambda i, j: (mod_idx(i), 0)),
                  pl.BlockSpec((MOD_ROWS, d), lambda i, j: (mod_idx(i), 0)),
                  pl.BlockSpec((1, d, tn), lambda i, j: (layer, 0, j))],
        out_specs=pl.BlockSpec((tm, tn), lambda i, j: (i, j)),
        out_shape=jax.ShapeDtypeStruct((m, n), F32),
        scratch_shapes=[pltpu.VMEM((tm, d), BF16)],
        compiler_params=_params(("parallel", "arbitrary")),
        name="inproj",
    )(x, g, scale_t, shift_t, w)


def _softplus(x):
    return jnp.maximum(x, 0.0) + jnp.log1p(jnp.exp(-jnp.abs(x)))


RW_TT = 128
HALF = LANES // 2


def _rwkv_pre_math(z, zp, mu_ref, w0_ref, a0_ref, kkw_ref, ka_ref, rk_ref, wl_ref, ones_ref):
    zs = z + (zp - z) * mu_ref[...]
    r, k, v = zs[:, :RW], zs[:, RW:2 * RW], zs[:, 2 * RW:3 * RW]
    l0 = zs[:, 3 * RW:3 * RW + LANES]
    lane = lax.broadcasted_iota(jnp.int32, l0.shape, 1)
    l0 = jnp.where(lane < R_DECAY, jnp.tanh(l0), l0)
    l1 = jax.nn.sigmoid(zs[:, 3 * RW + LANES:])
    lin = jnp.concatenate([l0, l1], axis=1).astype(BF16)
    lo = _dot(lin, wl_ref[...])
    w_log = -_softplus(-(w0_ref[...] + lo[:, :RW])) - 0.5
    a = jax.nn.sigmoid(a0_ref[...] + lo[:, RW:2 * RW])
    ones_bd = ones_ref[...]
    kk = k * kkw_ref[...]
    kkn = kk / jnp.maximum(jnp.sqrt(_segsum(kk * kk, ones_bd)), 1e-12)
    kh = k * (1.0 + (a - 1.0) * ka_ref[...])
    bonus = _segsum(r * kh * rk_ref[...], ones_bd) * v
    return r, jnp.exp(-jnp.exp(w_log)), kh, v, kkn, kkn * a, lo[:, 2 * RW:], bonus


def _rwkv_pre_prompt_kernel(z_ref, prev_ref, mu_ref, w0_ref, a0_ref, kkw_ref, ka_ref, rk_ref, wl_ref,
                            ones_ref, r_o, w_o, k_o, v_o, kk_o, b_o, g_o, bon_o, zt_scr):
    bi = pl.program_id(1)
    z = z_ref[...]
    row = lax.broadcasted_iota(jnp.int32, z.shape, 0)
    zp = jnp.where(row == 0, prev_ref[0], pltpu.roll(z, 1, axis=0))
    r, w, kh, v, kkn, b, g, bonus = _rwkv_pre_math(z, zp, mu_ref, w0_ref, a0_ref, kkw_ref, ka_ref,
                                                   rk_ref, wl_ref, ones_ref)
    g_o[...] = g
    bon_o[...] = bonus
    for n, x in enumerate((r, w, kh, v, kkn, b)):
        for p in range(RW // LANES):
            row0 = pl.multiple_of((bi * (RW // LANES) + p) * LANES, LANES)
            zt_scr[n, pl.ds(row0, LANES), :] = x[:, p * LANES:(p + 1) * LANES].T

    @pl.when(bi == pl.num_programs(1) - 1)
    def _():
        n_bh = zt_scr.shape[1] // HEAD

        def channel_rows(n, c0):
            cols = [[] for _ in range(SUBLANES)]
            for j in range(n_bh // SUBLANES):
                blk = [zt_scr[n, pl.ds(pl.multiple_of((j * SUBLANES + k) * HEAD + c0, SUBLANES), SUBLANES), :]
                       for k in range(SUBLANES)]
                for m, piece in enumerate(_sublane_transpose8(blk)):
                    cols[m].append(piece)
            return [jnp.concatenate(pieces, axis=0) for pieces in cols]

        def store_rows(o_ref, mats, rows_per_t, r0):
            for jt in range(RW_TT // SUBLANES):
                blk = [mt[jt * SUBLANES:(jt + 1) * SUBLANES] for mt in mats]
                for s, piece in enumerate(_sublane_transpose8(blk)):
                    row = pl.multiple_of((jt * SUBLANES + s) * rows_per_t + r0, SUBLANES)
                    o_ref[pl.ds(row, SUBLANES), :] = piece

        for n, o_ref in enumerate((r_o, w_o, k_o, v_o, kk_o, b_o)):
            if o_ref is v_o:
                def group(gi, c, n=n, o_ref=o_ref):
                    c0 = pl.multiple_of(gi * 2 * SUBLANES, 2 * SUBLANES)
                    a = channel_rows(n, c0) + channel_rows(n, c0 + SUBLANES)
                    mats = [jnp.concatenate([a[2 * q], a[2 * q + 1]], axis=0).T for q in range(SUBLANES)]
                    store_rows(o_ref, mats, HEAD // 2, pl.multiple_of(gi * SUBLANES, SUBLANES))
                    return c

                lax.fori_loop(0, HEAD // (2 * SUBLANES), group, 0)
            else:
                def group(gi, c, n=n, o_ref=o_ref):
                    c0 = pl.multiple_of(gi * SUBLANES, SUBLANES)
                    mats = [jnp.concatenate([a, a], axis=0).T for a in channel_rows(n, c0)]
                    store_rows(o_ref, mats, HEAD, c0)
                    return c

                lax.fori_loop(0, HEAD // SUBLANES, group, 0)


def _rwkv_pre_prompt(z_main, prev, n_b, n_t, mu, w0, a0, kkw, ka, rk, wl, ones_bd):
    nt = n_t // RW_TT
    vec = lambda w: pl.BlockSpec((1, w), lambda i, b: (0, 0))
    keyed = pl.BlockSpec((RW_TT * HEAD, LANES), lambda i, b: (i, 0), pipeline_mode=pl.Buffered(1))
    paired = pl.BlockSpec((RW_TT * (HEAD // 2), LANES), lambda i, b: (i, 0), pipeline_mode=pl.Buffered(1))
    keyed_shape = jax.ShapeDtypeStruct((n_t * HEAD, LANES), F32)
    paired_shape = jax.ShapeDtypeStruct((n_t * (HEAD // 2), LANES), F32)
    nat = pl.BlockSpec((RW_TT, RW), lambda i, b: (b * nt + i, 0))
    return pl.pallas_call(
        _rwkv_pre_prompt_kernel,
        grid=(nt, n_b),
        in_specs=[pl.BlockSpec((RW_TT, RW_PAD), lambda i, b: (b * nt + i, 0)),
                  pl.BlockSpec((1, 1, RW_PAD), lambda i, b: (b * nt + i, 0, 0)),
                  vec(RW_PAD), vec(RW), vec(RW), vec(RW), vec(RW), vec(RW),
                  pl.BlockSpec((LORA_IN, 3 * RW), lambda i, b: (0, 0)),
                  pl.BlockSpec((LANES, RW), lambda i, b: (0, 0))],
        out_specs=[keyed] * 3 + [paired] + [keyed] * 2 + [nat] * 2,
        out_shape=[keyed_shape] * 3 + [paired_shape] + [keyed_shape] * 2
        + [jax.ShapeDtypeStruct((n_b * n_t, RW), F32)] * 2,
        scratch_shapes=[pltpu.VMEM((6, n_b * RW, RW_TT), F32)],
        compiler_params=_params(("arbitrary", "arbitrary")),
        name="rwkv_pre_prompt",
    )(z_main, prev, mu, w0, a0, kkw, ka, rk, wl, ones_bd)


def _rwkv_pre_sample_kernel(z_ref, zp_ref, st_ref, mu_ref, w0_ref, a0_ref, kkw_ref, ka_ref, rk_ref, wl_ref,
                            ones_ref, r_o, w_o, k_o, v_o, kk_o, b_o, g_o, bon_o):
    zp = jnp.where(pl.program_id(0) == 0, st_ref[...], zp_ref[...])
    r, w, kh, v, kkn, b, g, bonus = _rwkv_pre_math(z_ref[...], zp, mu_ref, w0_ref, a0_ref, kkw_ref, ka_ref,
                                                   rk_ref, wl_ref, ones_ref)
    g_o[...] = g
    bon_o[...] = bonus
    for x, o_ref in zip((r, w, kh, v, kkn, b), (r_o, w_o, k_o, v_o, kk_o, b_o)):
        for p in range(RW // LANES):
            xt = x[:, p * LANES:(p + 1) * LANES].T
            o_ref[0, :, 2 * p * LANES:(2 * p + 1) * LANES] = xt[:HEAD]
            o_ref[0, :, (2 * p + 1) * LANES:(2 * p + 2) * LANES] = xt[HEAD:]


def _rwkv_pre_sample(z_main, state, row0, n_b, n_t, mu, w0, a0, kkw, ka, rk, wl, ones_bd):
    blk0 = row0 // n_b
    vec = lambda w: pl.BlockSpec((1, w), lambda t: (0, 0))
    keyed = pl.BlockSpec((1, HEAD, HEADS * n_b), lambda t: (t, 0, 0))
    nat = pl.BlockSpec((n_b, RW), lambda t: (t, 0))
    return pl.pallas_call(
        _rwkv_pre_sample_kernel,
        grid=(n_t,),
        in_specs=[pl.BlockSpec((n_b, RW_PAD), lambda t: (blk0 + t, 0)),
                  pl.BlockSpec((n_b, RW_PAD), lambda t: (blk0 + jnp.maximum(t - 1, 0), 0)),
                  pl.BlockSpec((n_b, RW_PAD), lambda t: (0, 0)),
                  vec(RW_PAD), vec(RW), vec(RW), vec(RW), vec(RW), vec(RW),
                  pl.BlockSpec((LORA_IN, 3 * RW), lambda t: (0, 0)),
                  pl.BlockSpec((LANES, RW), lambda t: (0, 0))],
        out_specs=[keyed] * 6 + [nat] * 2,
        out_shape=[jax.ShapeDtypeStruct((n_t, HEAD, HEADS * n_b), F32)] * 6
        + [jax.ShapeDtypeStruct((n_t * n_b, RW), F32)] * 2,
        compiler_params=_params(("arbitrary",)),
        name="rwkv_pre_sample",
    )(z_main, z_main, state, mu, w0, a0, kkw, ka, rk, wl, ones_bd)


def _rwkv_scan_kernel(nat_state, r_ref, w_ref, k_ref, kk_ref, b_ref, v_ref, s0_ref, y_ref, sT_ref, s_scr):
    tc = pl.program_id(1)
    n_t = r_ref.shape[0]
    n_i = s_scr.shape[0]

    @pl.when(tc == 0)
    def _():
        if nat_state:
            s_scr[...] = s0_ref[...].T.reshape(s_scr.shape)
        else:
            s_scr[...] = s0_ref[...]

    def step(t, carry):
        r_t, w_t, k_t, kk_t, b_t = r_ref[t], w_ref[t], k_ref[t], kk_ref[t], b_ref[t]

        def rows8(ib, c2):
            i0 = pl.multiple_of(ib * SUBLANES, SUBLANES)
            v8 = v_ref[t, pl.ds(i0, SUBLANES), :]
            ys = []
            for ii in range(SUBLANES):
                s_old = s_scr[i0 + ii]
                sa = jnp.sum(s_old * kk_t, axis=0, keepdims=True)
                s_new = s_old * w_t - sa * b_t + v8[ii:ii + 1] * k_t
                ys.append(jnp.sum(s_new * r_t, axis=0, keepdims=True))
                s_scr[i0 + ii] = s_new
            y_ref[t, pl.ds(i0, SUBLANES), :] = jnp.concatenate(ys, axis=0)
            return c2

        return lax.fori_loop(0, n_i // SUBLANES, rows8, carry)

    lax.fori_loop(0, n_t, step, 0)

    @pl.when(tc == pl.num_programs(1) - 1)
    def _():
        if nat_state:
            sT_ref[...] = s_scr[...].reshape(n_i * s_scr.shape[1], LANES).T
        else:
            sT_ref[...] = s_scr[...]


def _rwkv_scan_prompt(r, w, k, kk, b, v, s0, tchunk):
    n_t = r.shape[0]
    n_i = s0.shape[0]
    op = pl.BlockSpec((tchunk, HEAD, LANES), lambda l, t: (t, 0, 0))
    vy = pl.BlockSpec((tchunk, n_i, LANES), lambda l, t: (t, 0, 0))
    st = pl.BlockSpec((n_i, HEAD, LANES), lambda l, t: (0, 0, 0))
    return pl.pallas_call(
        functools.partial(_rwkv_scan_kernel, False),
        grid=(1, n_t // tchunk),
        in_specs=[op] * 5 + [vy, st],
        out_specs=[vy, st],
        out_shape=[jax.ShapeDtypeStruct((n_t, n_i, LANES), F32),
                   jax.ShapeDtypeStruct((n_i, HEAD, LANES), F32)],
        scratch_shapes=[pltpu.VMEM((n_i, HEAD, LANES), F32)],
        compiler_params=_params(("arbitrary", "arbitrary")),
        name="rwkv_scan_prompt",
    )(r, w, k, kk, b, v, s0)


def _rwkv_scan_sample(r, w, k, kk, b, v, s0, layer):
    n_t, _, n_l = r.shape
    n_b = n_l // HEADS
    op = pl.BlockSpec((n_t, HEAD, LANES), lambda h, t: (0, 0, h))
    st = pl.BlockSpec((n_b, HEAD * HEAD), lambda h, t: (layer, h))
    st_out = pl.BlockSpec((n_b, HEAD * HEAD), lambda h, t: (0, h))
    return pl.pallas_call(
        functools.partial(_rwkv_scan_kernel, True),
        grid=(HEADS, 1),
        in_specs=[op] * 6 + [st],
        out_specs=[op, st_out],
        out_shape=[jax.ShapeDtypeStruct((n_t, HEAD, n_l), F32),
                   jax.ShapeDtypeStruct((n_b, HEADS * HEAD * HEAD), F32)],
        scratch_shapes=[pltpu.VMEM((HEAD, HEAD, LANES), F32)],
        compiler_params=_params(("parallel", "arbitrary")),
        name="rwkv_scan_sample",
    )(r, w, k, kk, b, v, s0)


def _rwkv_post_math(y, bon_ref, g_ref, lg_ref, lb_ref, ones_ref):
    ones_bd = ones_ref[...]
    yc = y - _segsum(y, ones_bd) * (1.0 / HEAD)
    var = _segsum(yc * yc, ones_bd) * (1.0 / HEAD)
    yn = yc * lax.rsqrt(var + GN_EPS) * lg_ref[...] + lb_ref[...]
    return ((yn + bon_ref[...]) * g_ref[...]).astype(BF16)


def _rwkv_post_prompt_kernel(y_ref, bon_ref, g_ref, lg_ref, lb_ref, ones_ref, o_ref, zt_scr):
    bi = pl.program_id(1)

    @pl.when(bi == 0)
    def _():
        n_bh = zt_scr.shape[0] // HEAD

        def group(gi, c):
            q0 = pl.multiple_of(gi * SUBLANES, SUBLANES)
            mats = [[] for _ in range(SUBLANES)]
            for jt in range(RW_TT // SUBLANES):
                blk = [y_ref[pl.ds(pl.multiple_of((jt * SUBLANES + s) * (HEAD // 2) + q0, SUBLANES), SUBLANES), :]
                       for s in range(SUBLANES)]
                for m, piece in enumerate(_sublane_transpose8(blk)):
                    mats[m].append(piece)
            vals = []
            for pieces in mats:
                mt = jnp.concatenate(pieces, axis=0).T
                vals += [mt[:HALF], mt[HALF:]]
            for half in range(2):
                for j in range(n_bh // SUBLANES):
                    blk = [v[j * SUBLANES:(j + 1) * SUBLANES] for v in vals[half * SUBLANES:(half + 1) * SUBLANES]]
                    for k, piece in enumerate(_sublane_transpose8(blk)):
                        row = pl.multiple_of((j * SUBLANES + k) * HEAD + 2 * q0 + half * SUBLANES, SUBLANES)
                        zt_scr[pl.ds(row, SUBLANES), :] = piece
            return c

        lax.fori_loop(0, HEAD // (2 * SUBLANES), group, 0)

    pieces = []
    for p in range(RW // LANES):
        row0 = pl.multiple_of((bi * (RW // LANES) + p) * LANES, LANES)
        pieces.append(zt_scr[pl.ds(row0, LANES), :].T)
    y = jnp.concatenate(pieces, axis=1)
    o_ref[...] = _rwkv_post_math(y, bon_ref, g_ref, lg_ref, lb_ref, ones_ref)


def _rwkv_post_prompt(y, bonus, g, n_b, n_t, lnx_g, lnx_b, ones_bd):
    nt = n_t // RW_TT
    nat = pl.BlockSpec((RW_TT, RW), lambda i, b: (b * nt + i, 0))
    vec = pl.BlockSpec((1, RW), lambda i, b: (0, 0))
    return pl.pallas_call(
        _rwkv_post_prompt_kernel,
        grid=(nt, n_b),
        in_specs=[pl.BlockSpec((RW_TT * (HEAD // 2), LANES), lambda i, b: (i, 0)), nat, nat, vec, vec,
                  pl.BlockSpec((LANES, RW), lambda i, b: (0, 0))],
        out_specs=nat,
        out_shape=jax.ShapeDtypeStruct((n_b * n_t, RW), BF16),
        scratch_shapes=[pltpu.VMEM((n_b * RW, RW_TT), F32)],
        compiler_params=_params(("arbitrary", "arbitrary")),
        name="rwkv_post_prompt",
    )(y, bonus, g, lnx_g, lnx_b, ones_bd)


def _rwkv_post_sample_kernel(y_ref, bon_ref, g_ref, lg_ref, lb_ref, ones_ref, o_ref):
    pieces = []
    for p in range(RW // LANES):
        m = jnp.concatenate([y_ref[0, :, 2 * p * LANES:(2 * p + 1) * LANES],
                             y_ref[0, :, (2 * p + 1) * LANES:(2 * p + 2) * LANES]], axis=0)
        pieces.append(m.T)
    y = jnp.concatenate(pieces, axis=1)
    o_ref[...] = _rwkv_post_math(y, bon_ref, g_ref, lg_ref, lb_ref, ones_ref)


def _rwkv_post_sample(y, bonus, g, n_b, n_t, lnx_g, lnx_b, ones_bd):
    nat = pl.BlockSpec((n_b, RW), lambda t: (t, 0))
    vec = pl.BlockSpec((1, RW), lambda t: (0, 0))
    return pl.pallas_call(
        _rwkv_post_sample_kernel,
        grid=(n_t,),
        in_specs=[pl.BlockSpec((1, HEAD, HEADS * n_b), lambda t: (t, 0, 0)), nat, nat, vec, vec,
                  pl.BlockSpec((LANES, RW), lambda t: (0, 0))],
        out_specs=nat,
        out_shape=jax.ShapeDtypeStruct((n_t * n_b, RW), BF16),
        compiler_params=_params(("parallel",)),
        name="rwkv_post_sample",
    )(y, bonus, g, lnx_g, lnx_b, ones_bd)


def _pool_prompt_kernel(start_pos, z_ref, pw_ref, ps_ref, y_ref):
    n = z_ref.shape[0]
    row = lax.broadcasted_iota(jnp.int32, (n, POOL_G), 0)
    for gi, win in enumerate(WINDOWS):
        sl = slice(gi * POOL_G, (gi + 1) * POOL_G)
        x = z_ref[:, sl]
        s = x
        k = 1
        while k < win:
            s = s + jnp.where(row >= k, pltpu.roll(s, k, axis=0), 0.0)
            k *= 2
        cnt = jnp.minimum(win, row + (start_pos + 1)).astype(F32)
        d = s / cnt - x
        y_ref[:, sl] = (_dot(d.astype(BF16), pw_ref[gi]) * ps_ref[:, sl]).astype(BF16)


def _pool_prompt(z_main, n_b, n_t, pw, ps):
    cb = RW_PAD // POOL_W
    return pl.pallas_call(
        functools.partial(_pool_prompt_kernel, 0),
        grid=(n_b,),
        in_specs=[pl.BlockSpec((n_t, POOL_W), lambda b: (b, cb)),
                  pl.BlockSpec((len(WINDOWS), POOL_G, POOL_G), lambda b: (0, 0, 0)),
                  pl.BlockSpec((1, POOL_W), lambda b: (0, 0))],
        out_specs=pl.BlockSpec((n_t, POOL_W), lambda b: (b, 0)),
        out_shape=jax.ShapeDtypeStruct((n_b * n_t, POOL_W), BF16),
        compiler_params=_params(("parallel",)),
        name="pool_prompt",
    )(z_main, pw, ps)


def _pool_sample_kernel(start_pos, n_t, full_ref, pw_ref, ps_ref, y_ref):
    n_b = full_ref.shape[1]
    for gi, win in enumerate(WINDOWS):
        sl = slice(gi * POOL_G, (gi + 1) * POOL_G)
        f = full_ref[:, :, sl]
        s = f
        k = 1
        while k < win:
            s = s[k:] + s[:-k]
            k *= 2
        s = s[s.shape[0] - n_t:]
        x = f[POOL_BUF:]
        pos = start_pos + lax.broadcasted_iota(jnp.int32, s.shape, 0)
        cnt = jnp.minimum(win, pos + 1).astype(F32)
        d = (s / cnt - x).reshape(n_t * n_b, POOL_G)
        y_ref[:, sl] = (_dot(d.astype(BF16), pw_ref[gi]) * ps_ref[:, sl]).astype(BF16)


def _pool_sample(full, n_t, start_pos, pw, ps):
    n_f, n_b, _ = full.shape
    return pl.pallas_call(
        functools.partial(_pool_sample_kernel, start_pos, n_t),
        grid=(1,),
        in_specs=[pl.BlockSpec((n_f, n_b, POOL_W), lambda i: (0, 0, 0)),
                  pl.BlockSpec((len(WINDOWS), POOL_G, POOL_G), lambda i: (0, 0, 0)),
                  pl.BlockSpec((1, POOL_W), lambda i: (0, 0))],
        out_specs=pl.BlockSpec((n_t * n_b, POOL_W), lambda i: (0, 0)),
        out_shape=jax.ShapeDtypeStruct((n_t * n_b, POOL_W), BF16),
        compiler_params=_params(("arbitrary",)),
        name="pool_sample",
    )(full, pw, ps)


def _s5_tail(x_all, u, cmat_ref, d_ref, gw_ref, gb_ref):
    y = _dot(x_all.astype(BF16), cmat_ref[...]) + d_ref[...] * u
    y = jax.nn.gelu(y)
    return (y * jax.nn.sigmoid(_dot(y.astype(BF16), gw_ref[...]) + gb_ref[...])).astype(BF16)


S5_LC = 512


def _s5_prompt_kernel(u_ref, bmat_ref, cmat_ref, lam_ref, pw_ref, d_ref, gw_ref, gb_ref,
                      y_ref, xre_o, xim_o, x_scr, c_scr):
    tc = pl.program_id(1)

    @pl.when(tc == 0)
    def _():
        c_scr[...] = jnp.zeros_like(c_scr)

    u = u_ref[...]
    n = u.shape[0]
    x_scr[...] = _dot(u.astype(BF16), bmat_ref[...])
    row = lax.broadcasted_iota(jnp.int32, (SUBLANES, S5_LC), 0)
    for lc in range(S5_N // S5_LC):
        re_sl = pl.ds(lc * S5_LC, S5_LC)
        im_sl = pl.ds(S5_N + lc * S5_LC, S5_LC)
        lam = [(lam_ref[2 * j:2 * j + 1, re_sl], lam_ref[2 * j + 1:2 * j + 2, re_sl]) for j in range(3)]
        p_re, p_im = pw_ref[:, re_sl], pw_ref[:, im_sl]

        def blk(rb, carry, re_sl=re_sl, im_sl=im_sl, lam=lam, p_re=p_re, p_im=p_im):
            c_re, c_im = carry
            rows = pl.ds(pl.multiple_of(rb * SUBLANES, SUBLANES), SUBLANES)
            xr, xi = x_scr[rows, re_sl], x_scr[rows, im_sl]
            for j, (l_re, l_im) in enumerate(lam):
                kshift = 1 << j
                sr = jnp.where(row >= kshift, pltpu.roll(xr, kshift, axis=0), 0.0)
                si = jnp.where(row >= kshift, pltpu.roll(xi, kshift, axis=0), 0.0)
                xr, xi = xr + (l_re * sr - l_im * si), xi + (l_re * si + l_im * sr)
            xr, xi = xr + (p_re * c_re - p_im * c_im), xi + (p_re * c_im + p_im * c_re)
            x_scr[rows, re_sl] = xr
            x_scr[rows, im_sl] = xi
            return xr[SUBLANES - 1:], xi[SUBLANES - 1:]

        c_re, c_im = lax.fori_loop(0, n // SUBLANES, blk, (c_scr[0:1, re_sl], c_scr[0:1, im_sl]))
        c_scr[0:1, re_sl] = c_re
        c_scr[0:1, im_sl] = c_im

    y_ref[...] = _s5_tail(x_scr[...], u, cmat_ref, d_ref, gw_ref, gb_ref)

    @pl.when(tc == pl.num_programs(1) - 1)
    def _():
        xre_o[0] = c_scr[0:1, :S5_N]
        xim_o[0] = c_scr[0:1, S5_N:]


def _s5_prompt(z_main, n_b, n_t, tt, bmat, cmat, lam_t, pw_t, dskip, gw, gb):
    cb = (RW_PAD + POOL_W) // S5_W
    n_tc = n_t // tt
    const = lambda shape: pl.BlockSpec(shape, lambda b, t: (0,) * len(shape))
    return pl.pallas_call(
        _s5_prompt_kernel,
        grid=(n_b, n_tc),
        in_specs=[pl.BlockSpec((tt, S5_W), lambda b, t: (b * n_tc + t, cb)),
                  const((S5_W, 2 * S5_N)), const((2 * S5_N, S5_W)),
                  const((SUBLANES, S5_N)), const((SUBLANES, 2 * S5_N)),
                  const((1, S5_W)), const((S5_W, S5_W)), const((1, S5_W))],
        out_specs=[pl.BlockSpec((tt, S5_W), lambda b, t: (b * n_tc + t, 0)),
                   pl.BlockSpec((1, 1, S5_N), lambda b, t: (b, 0, 0)),
                   pl.BlockSpec((1, 1, S5_N), lambda b, t: (b, 0, 0))],
        out_shape=[jax.ShapeDtypeStruct((n_b * n_t, S5_W), BF16),
                   jax.ShapeDtypeStruct((n_b, 1, S5_N), F32),
                   jax.ShapeDtypeStruct((n_b, 1, S5_N), F32)],
        scratch_shapes=[pltpu.VMEM((tt, 2 * S5_N), F32), pltpu.VMEM((SUBLANES, 2 * S5_N), F32)],
        compiler_params=_params(("parallel", "arbitrary")),
        name="s5_prompt",
    )(z_main, bmat, cmat, lam_t, pw_t, dskip, gw, gb)


def _s5_sample_kernel(n_t, u_ref, x0re_ref, x0im_ref, bmat_ref, cmat_ref, lam_ref, d_ref, gw_ref,
                      gb_ref, y_ref, xre_o, xim_o, x_scr):
    u = u_ref[...]
    n_b = u.shape[0] // n_t
    x_scr[...] = _dot(u.astype(BF16), bmat_ref[...])
    for lc in range(S5_N // S5_LC):
        re_sl = pl.ds(lc * S5_LC, S5_LC)
        im_sl = pl.ds(S5_N + lc * S5_LC, S5_LC)
        l_re, l_im = lam_ref[0:1, re_sl], lam_ref[1:2, re_sl]
        xr, xi = x0re_ref[:, re_sl], x0im_ref[:, re_sl]
        for t in range(n_t):
            rows = pl.ds(t * n_b, n_b)
            xr, xi = (l_re * xr - l_im * xi + x_scr[rows, re_sl],
                      l_re * xi + l_im * xr + x_scr[rows, im_sl])
            x_scr[rows, re_sl] = xr
            x_scr[rows, im_sl] = xi
        xre_o[:, re_sl] = xr
        xim_o[:, re_sl] = xi
    y_ref[...] = _s5_tail(x_scr[...], u, cmat_ref, d_ref, gw_ref, gb_ref)


def _s5_sample(z_main, row0, n_b, n_t, x0re, x0im, bmat, cmat, lam_t, dskip, gw, gb):
    rows = n_b * n_t
    cb = (RW_PAD + POOL_W) // S5_W
    const = lambda shape: pl.BlockSpec(shape, lambda i: (0,) * len(shape))
    return pl.pallas_call(
        functools.partial(_s5_sample_kernel, n_t),
        grid=(1,),
        in_specs=[pl.BlockSpec((rows, S5_W), lambda i: (row0 // rows, cb)),
                  const((n_b, S5_N)), const((n_b, S5_N)),
                  const((S5_W, 2 * S5_N)), const((2 * S5_N, S5_W)), const((SUBLANES, S5_N)),
                  const((1, S5_W)), const((S5_W, S5_W)), const((1, S5_W))],
        out_specs=[const((rows, S5_W)), const((n_b, S5_N)), const((n_b, S5_N))],
        out_shape=[jax.ShapeDtypeStruct((rows, S5_W), BF16),
                   jax.ShapeDtypeStruct((n_b, S5_N), F32),
                   jax.ShapeDtypeStruct((n_b, S5_N), F32)],
        scratch_shapes=[pltpu.VMEM((rows, 2 * S5_N), F32)],
        compiler_params=_params(("arbitrary",)),
        name="s5_sample",
    )(z_main, x0re, x0im, bmat, cmat, lam_t, dskip, gw, gb)


def _merge_kernel(ya_ref, yb_ref, yc_ref, zg_ref, wbr_ref, wout_ref, x_ref, g1_ref, n2_ref,
                  sc2_ref, sh2_ref, rw_ref, rb_ref, xo_ref, h2_ref, lg_ref):
    d = x_ref.shape[1]
    rows = x_ref.shape[0]
    m = jax.nn.sigmoid(zg_ref[:, :d]) * _dot(ya_ref[...], wbr_ref[:RW, :])
    m = m + jax.nn.sigmoid(zg_ref[:, d:2 * d]) * _dot(yb_ref[...], wbr_ref[RW:RW + POOL_W, :])
    m = m + jax.nn.sigmoid(zg_ref[:, 2 * d:]) * _dot(yc_ref[...], wbr_ref[RW + POOL_W:, :])
    xn = x_ref[...] + _tile_rows(g1_ref[...], rows) * _dot(m.astype(BF16), wout_ref[...])
    xo_ref[...] = xn
    h2 = _norm_mod(xn, n2_ref[...], sc2_ref[...], sh2_ref[...])
    h2_ref[...] = h2
    lg_ref[...] = jnp.dot(h2, rw_ref[...], precision=lax.Precision.HIGHEST,
                          preferred_element_type=F32) + rb_ref[...]


def _merge(ya, yb, yc, zg, wbr, wout, x, g1_t, n2g, sc2_t, sh2_t, rw, rb, mod_idx, tm):
    m, d = x.shape
    row = lambda w: pl.BlockSpec((tm, w), lambda i: (i, 0))
    const = lambda shape: pl.BlockSpec(shape, lambda i: (0,) * len(shape), pipeline_mode=pl.Buffered(1))
    mod = pl.BlockSpec((MOD_ROWS, d), lambda i: (mod_idx(i), 0))
    return pl.pallas_call(
        _merge_kernel,
        grid=(m // tm,),
        in_specs=[row(RW), row(POOL_W), row(S5_W), row(3 * d), const(wbr.shape), const(wout.shape),
                  row(d), mod, const((1, d)), mod, mod, const(rw.shape), const((1, LANES))],
        out_specs=[row(d), row(d), row(LANES)],
        out_shape=[jax.ShapeDtypeStruct((m, d), F32), jax.ShapeDtypeStruct((m, d), F32),
                   jax.ShapeDtypeStruct((m, LANES), F32)],
        compiler_params=_params(("parallel",)),
        name="merge",
    )(ya, yb, yc, zg, wbr, wout, x, g1_t, n2g, sc2_t, sh2_t, rw, rb)


MOE_BM = 512
MOE_TF = 512


def _moe_kernel(be_ref, nb_ref, x_ref, wg_ref, wu_ref, bg_ref, bu_ref, wd_ref, bd_ref, o_ref):
    s, f = pl.program_id(0), pl.program_id(1)
    used = s < nb_ref[0]

    @pl.when(jnp.logical_and(jnp.logical_not(used), f == 0))
    def _():
        o_ref[...] = jnp.zeros_like(o_ref)

    @pl.when(used)
    def _():
        x = x_ref[...].astype(BF16)
        g = _dot(x, wg_ref[0].astype(BF16)) + bg_ref[0]
        u = _dot(x, wu_ref[0].astype(BF16)) + bu_ref[0]
        g = jnp.minimum(g, SW_LIMIT)
        u = jnp.clip(u, -SW_LIMIT, SW_LIMIT)
        act = (u + 1.0) * (g * jax.nn.sigmoid(SW_ALPHA * g))
        part = _dot(act.astype(BF16), wd_ref[0].astype(BF16))

        @pl.when(f == 0)
        def _():
            o_ref[...] = part + bd_ref[0]

        @pl.when(f > 0)
        def _():
            o_ref[...] += part


def _moe(xs, blk_e, n_used, layer, w_gu, b_gu, w_down, b_down):
    rows, d = xs.shape
    n_l, n_e, _, two_ff = w_gu.shape
    dff = two_ff // 2
    nf = dff // MOE_TF
    n_blk = rows // MOE_BM
    e0 = layer * n_e

    def f_eff(s, f, nb):
        return jnp.where(s < nb[0], f, nf - 1)

    grid_spec = pltpu.PrefetchScalarGridSpec(
        num_scalar_prefetch=2,
        grid=(n_blk, nf),
        in_specs=[
            pl.BlockSpec((MOE_BM, d), lambda s, f, be, nb: (jnp.minimum(s, nb[0] - 1), 0)),
            pl.BlockSpec((1, d, MOE_TF), lambda s, f, be, nb: (e0 + be[s], 0, f_eff(s, f, nb))),
            pl.BlockSpec((1, d, MOE_TF), lambda s, f, be, nb: (e0 + be[s], 0, nf + f_eff(s, f, nb))),
            pl.BlockSpec((1, 1, MOE_TF), lambda s, f, be, nb: (e0 + be[s], 0, f_eff(s, f, nb))),
            pl.BlockSpec((1, 1, MOE_TF), lambda s, f, be, nb: (e0 + be[s], 0, nf + f_eff(s, f, nb))),
            pl.BlockSpec((1, MOE_TF, d), lambda s, f, be, nb: (e0 + be[s], f_eff(s, f, nb), 0)),
            pl.BlockSpec((1, 1, d), lambda s, f, be, nb: (e0 + be[s], 0, 0)),
        ],
        out_specs=pl.BlockSpec((MOE_BM, d), lambda s, f, be, nb: (s, 0)),
    )
    w_gu = w_gu.reshape(n_l * n_e, d, two_ff)
    b_gu = b_gu.reshape(n_l * n_e, 1, two_ff)
    return pl.pallas_call(
        _moe_kernel,
        grid_spec=grid_spec,
        out_shape=jax.ShapeDtypeStruct((rows, d), F32),
        compiler_params=_params(("arbitrary", "arbitrary")),
        name="moe",
    )(blk_e, n_used, xs, w_gu, w_gu, b_gu, b_gu, w_down.reshape(n_l * n_e, dff, d),
      b_down.reshape(n_l * n_e, 1, d))


def _route(logits, n_rows_pad):
    n_tok = logits.shape[0]
    top_v, top_e = lax.top_k(logits, TOP_K)
    gate = jax.nn.softmax(top_v, axis=-1)
    flat_e = top_e.reshape(-1)
    onehot = (flat_e[:, None] == jnp.arange(N_EXP, dtype=jnp.int32)[None, :]).astype(jnp.int32)
    csum = jnp.cumsum(onehot, axis=0)
    counts = csum[-1]
    rank = jnp.sum((csum - 1) * onehot, axis=1)
    padded = (counts + MOE_BM - 1) // MOE_BM * MOE_BM
    pad_end = jnp.cumsum(padded)
    pad_start = pad_end - padded
    dest = (pad_start[flat_e] + rank).astype(jnp.int32)
    flat_tok = jnp.arange(n_tok * TOP_K, dtype=jnp.int32) // TOP_K
    rows_tok = jnp.zeros((n_rows_pad,), jnp.int32).at[dest].set(flat_tok)
    n_rows_used = pad_end[-1].astype(jnp.int32)
    starts = jnp.arange(n_rows_pad // MOE_BM, dtype=jnp.int32) * MOE_BM
    starts = jnp.minimum(starts, n_rows_used - MOE_BM)
    blk_e = jnp.minimum(jnp.searchsorted(pad_end, starts, side='right'), N_EXP - 1).astype(jnp.int32)
    return gate, dest.reshape(n_tok, TOP_K), rows_tok, blk_e, n_rows_used.reshape(1)


def _combine_kernel(x_ref, y0_ref, y1_ref, y2_ref, y3_ref, gw_ref, g2_ref, o_ref):
    rows = x_ref.shape[0]
    gw = gw_ref[...]
    acc = gw[:, 0:1] * y0_ref[...]
    for k, y_ref in enumerate((y1_ref, y2_ref, y3_ref), start=1):
        acc = acc + gw[:, k:k + 1] * y_ref[...]
    o_ref[...] = x_ref[...] + _tile_rows(g2_ref[...], rows) * acc


def _combine(x, yg, gw, g2_t, mod_idx, tm):
    m, d = x.shape
    n_i = m // tm
    slab = lambda k: pl.BlockSpec((tm, d), lambda i: (k * n_i + i, 0))
    return pl.pallas_call(
        _combine_kernel,
        grid=(n_i,),
        in_specs=[pl.BlockSpec((tm, d), lambda i: (i, 0)), slab(0), slab(1), slab(2), slab(3),
                  pl.BlockSpec((tm, TOP_K), lambda i: (i, 0)),
                  pl.BlockSpec((MOD_ROWS, d), lambda i: (mod_idx(i), 0))],
        out_specs=pl.BlockSpec((tm, d), lambda i: (i, 0)),
        out_shape=jax.ShapeDtypeStruct((m, d), F32),
        compiler_params=_params(("parallel",)),
        name="combine",
    )(x, yg, yg, yg, yg, gw, g2_t)


def _final_norm_kernel(x_ref, g_ref, o_ref):
    x = x_ref[...]
    o_ref[...] = x * lax.rsqrt(jnp.mean(x * x, axis=-1, keepdims=True) + NORM_EPS) * g_ref[...]


def _final_norm(x, g, tm):
    m, d = x.shape
    return pl.pallas_call(
        _final_norm_kernel,
        grid=(m // tm,),
        in_specs=[pl.BlockSpec((tm, d), lambda i: (i, 0)), pl.BlockSpec((1, d), lambda i: (0, 0))],
        out_specs=pl.BlockSpec((tm, d), lambda i: (i, 0)),
        out_shape=jax.ShapeDtypeStruct((m, d), F32),
        compiler_params=_params(("parallel",)),
        name="final_norm",
    )(x, g)


def _s5_params(a_re, a_im, log_dt, b_re, b_im, c_re, c_im):
    dt = jnp.exp(log_dt)[:, None]
    mag = jnp.exp(a_re * dt)
    lb_re, lb_im = mag * jnp.cos(a_im * dt), mag * jnp.sin(a_im * dt)
    nr, ni = lb_re - 1.0, lb_im
    den = a_re * a_re + a_im * a_im
    f_re = (nr * a_re + ni * a_im) / den
    f_im = (ni * a_re - nr * a_im) / den
    bb_re = f_re[..., None] * b_re - f_im[..., None] * b_im
    bb_im = f_re[..., None] * b_im + f_im[..., None] * b_re
    eye = jnp.eye(S5_G, dtype=F32)
    bd_in = lambda w: jnp.einsum('gpc,gh->gchp', w, eye).reshape(S5_W, S5_N)
    bd_out = lambda w: jnp.einsum('gcp,gh->gphc', w, eye).reshape(S5_N, S5_W)
    bmat = jnp.concatenate([bd_in(bb_re), bd_in(bb_im)], axis=1).astype(BF16)
    cmat = jnp.concatenate([bd_out(c_re), -bd_out(c_im)], axis=0).astype(BF16)
    l_re, l_im = lb_re.reshape(1, S5_N), lb_im.reshape(1, S5_N)
    pows = [(l_re, l_im)]
    for _ in range(SUBLANES - 1):
        p_re, p_im = pows[-1]
        pows.append((p_re * l_re - p_im * l_im, p_re * l_im + p_im * l_re))
    lam_t = jnp.concatenate([pows[0][0], pows[0][1], pows[1][0], pows[1][1], pows[3][0], pows[3][1],
                             jnp.zeros((2, S5_N), F32)], axis=0)
    pw_t = jnp.concatenate([jnp.concatenate([p[0] for p in pows], axis=0),
                            jnp.concatenate([p[1] for p in pows], axis=0)], axis=1)
    return bmat, cmat, lam_t, pw_t


def _pad_cols(w, width):
    return jnp.pad(w, ((0, 0), (0, width - w.shape[1])))


def kernel(x_prompt, x_sample, c_prompt, c_sample, state_wkv, state_shift, state_pool, state_s5_re, state_s5_im, norm1_g, norm2_g, final_norm_g, w_ada, b_ada, w_in, rw_mu, rw_w0, rw_w2, rw_a0, rw_a2, rw_g2, rw_kk, rw_ka, rw_rk, rw_lnx_g, rw_lnx_b, pool_w, pool_scale, s5_a_re, s5_a_im, s5_log_dt, s5_b_re, s5_b_im, s5_c_re, s5_c_im, s5_d, s5_glu_w, s5_glu_b, w_br, w_out, router_w, router_b, moe_w_gu, moe_b_gu, moe_w_down, moe_b_down):
    bp, lp, d = x_prompt.shape
    bs, ls, _ = x_sample.shape
    depth = w_in.shape[0]
    mp, ms = bp * lp, bs * ls
    m = mp + ms
    past_len = 16384
    assert bs == MOD_ROWS and bp * HEADS * 2 == LANES and lp % 1024 == 0 and ms == 1024

    def mod_idx_for(tm):
        n_p = mp // tm
        return lambda i: jnp.where(i < n_p, (i * tm) // lp, bp)

    tm_big, tm_mid = 1024, 256
    ones_bd = jnp.pad(jnp.kron(jnp.eye(HEADS, dtype=F32), jnp.ones((1, HEAD), F32)),
                      ((0, LANES - HEADS), (0, 0))).astype(BF16)

    x = _stack_rows(x_prompt.reshape(mp, d), jnp.swapaxes(x_sample, 0, 1).reshape(ms, d))
    c_all = jnp.concatenate([c_prompt, c_sample, jnp.zeros((4, d), F32)], axis=0)
    mod = _adaln(c_all, w_ada, b_ada)

    mod6 = mod.reshape(depth, mod.shape[1], 6, d)
    tables = jnp.concatenate([jnp.repeat(mod6[:, :bp], MOD_ROWS, axis=1), mod6[:, bp:bp + bs]], axis=1)
    tables = tables.transpose(0, 2, 1, 3)
    w_main_all = jnp.concatenate([w_in[:, :, :RW_PROJ], jnp.zeros((depth, d, RW_PAD - RW_PROJ), F32),
                                  w_in[:, :, RW_PROJ:RW_PROJ + POOL_W + S5_W]], axis=2).astype(BF16)
    w_gate_all = w_in[:, :, RW_PROJ + POOL_W + S5_W:].astype(BF16)
    wl_all = jnp.zeros((depth, LORA_IN, 3 * RW), F32)
    wl_all = wl_all.at[:, :R_DECAY, :RW].set(rw_w2)
    wl_all = wl_all.at[:, R_DECAY:R_DECAY + R_AAA, RW:2 * RW].set(rw_a2)
    wl_all = wl_all.at[:, R_DECAY + R_AAA:R_DECAY + R_AAA + R_GATE, 2 * RW:].set(rw_g2).astype(BF16)
    s5_all = jax.vmap(_s5_params)(s5_a_re, s5_a_im, s5_log_dt, s5_b_re, s5_b_im, s5_c_re, s5_c_im)

    outs_p, outs_s = [], []
    for l in range(depth):
        shift1, scale1, gate1, shift2, scale2, gate2 = (tables[l, i] for i in range(6))

        n1 = norm1_g[l].reshape(1, d)
        z_main = _inproj(x, n1, scale1, shift1, w_main_all, l, mod_idx_for(tm_big), tm_big)
        z_gate = _inproj(x, n1, scale1, shift1, w_gate_all, l, mod_idx_for(tm_big), tm_big)

        mu = _pad_cols(rw_mu[l].reshape(1, RW_PROJ), RW_PAD)
        wl = wl_all[l]
        vecs = [v.reshape(1, RW) for v in (rw_w0[l], rw_a0[l], rw_kk[l], rw_ka[l], rw_rk[l])]
        lnx_g, lnx_b = rw_lnx_g[l].reshape(1, RW), rw_lnx_b[l].reshape(1, RW)
        n_tp = mp // RW_TT
        last_rows = z_main[RW_TT - 1:mp:RW_TT, :RW_PAD]
        prev_p = jnp.concatenate([jnp.zeros((1, RW_PAD), F32), last_rows[:-1]], axis=0)
        first_of_seq = (jnp.arange(n_tp) % (lp // RW_TT) == 0)[:, None]
        prev_p = jnp.where(first_of_seq, 0.0, prev_p).reshape(n_tp, 1, RW_PAD)
        *ops_p, g_p, bon_p = _rwkv_pre_prompt(z_main, prev_p, bp, lp, mu, *vecs, wl, ones_bd)
        ops_p = [a.reshape(lp, a.shape[0] // lp, LANES) for a in ops_p]
        y_pt, st_p = _rwkv_scan_prompt(*ops_p[:3], *ops_p[4:], ops_p[3],
                                       jnp.zeros((HEAD // 2, HEAD, LANES), F32), 64)
        y_a_p = _rwkv_post_prompt(y_pt.reshape(lp * (HEAD // 2), LANES), bon_p, g_p, bp, lp,
                                  lnx_g, lnx_b, ones_bd)
        new_wkv_p = st_p.reshape(HEAD // 2, HEAD, 2, bp, HEADS).transpose(3, 4, 0, 2, 1)
        new_wkv_p = new_wkv_p.reshape(bp, HEADS, HEAD, HEAD)

        *ops_s, g_s, bon_s = _rwkv_pre_sample(z_main, _pad_cols(state_shift[l], RW_PAD), mp, bs, ls,
                                              mu, *vecs, wl, ones_bd)
        y_st, st_s = _rwkv_scan_sample(*ops_s[:3], *ops_s[4:], ops_s[3],
                                       state_wkv.reshape(depth * bs, HEADS * HEAD * HEAD), l)
        y_a_s = _rwkv_post_sample(y_st, bon_s, g_s, bs, ls, lnx_g, lnx_b, ones_bd)
        new_wkv_s = st_s.reshape(bs, HEADS, HEAD, HEAD)
        y_a = jnp.concatenate([y_a_p, y_a_s], axis=0)
        new_shift_p = z_main[lp - 1:mp:lp, :RW_PROJ]
        new_shift_s = z_main[m - bs:, :RW_PROJ]

        pw = pool_w[l].astype(BF16)
        ps = pool_scale[l].reshape(1, POOL_W)
        y_b_p = _pool_prompt(z_main, bp, lp, pw, ps)
        zb_s = z_main[mp:, RW_PAD:RW_PAD + POOL_W].reshape(ls, bs, POOL_W)
        full_s = jnp.concatenate([jnp.swapaxes(state_pool[l], 0, 1), zb_s], axis=0)
        y_b_s = _pool_sample(full_s, ls, past_len, pw, ps)
        y_b = jnp.concatenate([y_b_p, y_b_s], axis=0)
        new_pool_p = z_main[:mp, RW_PAD:RW_PAD + POOL_W].reshape(bp, lp, POOL_W)[:, lp - POOL_BUF:]
        new_pool_s = jnp.swapaxes(full_s[full_s.shape[0] - POOL_BUF:], 0, 1)

        bmat, cmat, lam_t, pw_t = (t[l] for t in s5_all)
        dskip, gw, gb = s5_d[l].reshape(1, S5_W), s5_glu_w[l].astype(BF16), s5_glu_b[l].reshape(1, S5_W)
        y_c_p, re_p, im_p = _s5_prompt(z_main, bp, lp, 512, bmat, cmat, lam_t, pw_t, dskip, gw, gb)
        y_c_s, re_s, im_s = _s5_sample(z_main, mp, bs, ls, state_s5_re[l].reshape(bs, S5_N),
                                       state_s5_im[l].reshape(bs, S5_N), bmat, cmat, lam_t, dskip, gw, gb)
        y_c = jnp.concatenate([y_c_p, y_c_s], axis=0)

        rw_pad = _pad_cols(router_w[l], LANES)
        rb_pad = jnp.concatenate([router_b[l], jnp.full((LANES - N_EXP,), -1e30, F32)]).reshape(1, LANES)
        x, h2, logits = _merge(y_a, y_b, y_c, z_gate, w_br[l].astype(BF16), w_out[l].astype(BF16), x, gate1,
                               norm2_g[l].reshape(1, d), scale2, shift2, rw_pad, rb_pad,
                               mod_idx_for(tm_mid), tm_mid)

        n_rows_pad = (m * TOP_K // MOE_BM + N_EXP) * MOE_BM
        gate_w, dest, rows_tok, blk_e, n_rows_used = _route(logits[:, :N_EXP], n_rows_pad)
        yb = _moe(h2[rows_tok], blk_e, n_rows_used // MOE_BM, l, moe_w_gu, moe_b_gu, moe_w_down, moe_b_down)
        yg = yb[dest.T.reshape(-1)]
        x = _combine(x, yg, gate_w, gate2, mod_idx_for(tm_mid), tm_mid)

        outs_p.append((new_shift_p, new_wkv_p, new_pool_p, re_p.reshape(bp, S5_G, S5_P),
                       im_p.reshape(bp, S5_G, S5_P)))
        outs_s.append((new_shift_s, new_wkv_s, new_pool_s, re_s.reshape(bs, S5_G, S5_P),
                       im_s.reshape(bs, S5_G, S5_P)))

    y = _final_norm(x, final_norm_g.reshape(1, d), tm_mid)
    y_prompt = y[:mp].reshape(bp, lp, d)
    y_sample = jnp.swapaxes(y[mp:].reshape(ls, bs, d), 0, 1)
    p_shift, p_wkv, p_pool, p_re, p_im = (jnp.stack([o[j] for o in outs_p]) for j in range(5))
    s_shift, s_wkv, s_pool, s_re, s_im = (jnp.stack([o[j] for o in outs_s]) for j in range(5))
    return (y_prompt, y_sample, p_wkv, p_shift, p_pool, p_re, p_im,
            s_wkv, s_shift, s_pool, s_re, s_im)
```

```python
import functools

import jax
import jax.numpy as jnp
from jax import lax
from jax.experimental import pallas as pl
from jax.experimental.pallas import tpu as pltpu

F32, BF16 = jnp.float32, jnp.bfloat16

LANES = 128
SUBLANES = 8
VMEM_LIMIT = 56 * 1024 * 1024

HEAD = 64
HEADS = 16
RW = HEAD * HEADS
R_DECAY, R_AAA, R_GATE = 64, 64, 160
RW_PROJ = 3 * RW + R_DECAY + R_AAA + R_GATE
RW_PAD = 3584
LORA_IN = RW_PAD - 3 * RW
POOL_W = 512
WINDOWS = (2, 4, 8, 16)
POOL_G = POOL_W // len(WINDOWS)
POOL_BUF = max(WINDOWS) - 1
S5_W = 512
S5_G, S5_CH, S5_P = 32, 16, 64
S5_N = S5_G * S5_P
MAIN_W = RW_PAD + POOL_W + S5_W
N_EXP, TOP_K = 32, 4
SW_LIMIT, SW_ALPHA = 7.0, 1.702
NORM_EPS, GN_EPS = 1e-5, 64e-5
MOD_ROWS = 128


def _params(sem):
    return pltpu.CompilerParams(dimension_semantics=sem, vmem_limit_bytes=VMEM_LIMIT)


def _dot(a, b):
    return jnp.dot(a, b, preferred_element_type=F32)


def _segsum(x, ones_bd):
    def split(v):
        hi = v.astype(BF16)
        return hi, (v - hi.astype(F32)).astype(BF16)

    nt = (((1,), (1,)), ((), ()))
    hi, lo = split(x)
    sums = (lax.dot_general(hi, ones_bd, nt, preferred_element_type=F32)
            + lax.dot_general(lo, ones_bd, nt, preferred_element_type=F32))
    hi, lo = split(sums)
    return _dot(hi, ones_bd) + _dot(lo, ones_bd)


def _sublane_transpose8(xs):
    xs = list(xs)
    sub = lax.broadcasted_iota(jnp.int32, xs[0].shape, 0)
    for d in (4, 2, 1):
        keep = (sub & d) == 0
        for k in range(SUBLANES):
            if k & d:
                continue
            lo, hi = xs[k], xs[k + d]
            xs[k] = jnp.where(keep, lo, pltpu.roll(hi, d, axis=0))
            xs[k + d] = jnp.where(keep, pltpu.roll(lo, SUBLANES - d, axis=0), hi)
    return xs


def _tile_rows(t, rows):
    return jnp.broadcast_to(t[None], (rows // MOD_ROWS,) + t.shape).reshape(rows, t.shape[-1])


def _stack_rows_kernel(a_ref, b_ref, o_ref):
    i = pl.program_id(0)

    @pl.when(i < pl.num_programs(0) - 1)
    def _():
        o_ref[...] = a_ref[...]

    @pl.when(i == pl.num_programs(0) - 1)
    def _():
        o_ref[...] = b_ref[...]


def _stack_rows(a, b):
    tm, d = b.shape
    n_a = a.shape[0] // tm
    return pl.pallas_call(
        _stack_rows_kernel,
        grid=(n_a + 1,),
        in_specs=[pl.BlockSpec((tm, d), lambda i: (jnp.minimum(i, n_a - 1), 0)),
                  pl.BlockSpec((tm, d), lambda i: (0, 0))],
        out_specs=pl.BlockSpec((tm, d), lambda i: (i, 0)),
        out_shape=jax.ShapeDtypeStruct((a.shape[0] + tm, d), a.dtype),
        compiler_params=_params(("arbitrary",)),
        name="stack_rows",
    )(a, b)


def _ada_kernel(c_ref, w_ref, b_ref, o_ref):
    c = c_ref[...]
    a = (c * jax.nn.sigmoid(c)).astype(BF16)
    o_ref[0] = _dot(a, w_ref[0].astype(BF16)) + b_ref[0]


def _adaln(c_all, w_ada, b_ada):
    nl, d, n = w_ada.shape
    r = c_all.shape[0]
    tn = 1024
    return pl.pallas_call(
        _ada_kernel,
        grid=(nl, n // tn),
        in_specs=[pl.BlockSpec((r, d), lambda l, j: (0, 0)),
                  pl.BlockSpec((1, d, tn), lambda l, j: (l, 0, j)),
                  pl.BlockSpec((1, 1, tn), lambda l, j: (l, 0, j))],
        out_specs=pl.BlockSpec((1, r, tn), lambda l, j: (l, 0, j)),
        out_shape=jax.ShapeDtypeStruct((nl, r, n), F32),
        compiler_params=_params(("parallel", "parallel")),
        name="adaln",
    )(c_all, w_ada, b_ada.reshape(nl, 1, n))


def _norm_mod(x, g, scale, shift):
    y = x * lax.rsqrt(jnp.mean(x * x, axis=-1, keepdims=True) + NORM_EPS) * g
    rows = x.shape[0]
    return y * (1.0 + _tile_rows(scale, rows)) + _tile_rows(shift, rows)


def _inproj_kernel(x_ref, g_ref, sc_ref, sh_ref, w_ref, o_ref, h_scr):
    @pl.when(pl.program_id(1) == 0)
    def _():
        h_scr[...] = _norm_mod(x_ref[...], g_ref[...], sc_ref[...], sh_ref[...]).astype(BF16)

    o_ref[...] = _dot(h_scr[...], w_ref[0])


def _inproj(x, g, scale_t, shift_t, w, layer, mod_idx, tm):
    m, d = x.shape
    n = w.shape[2]
    tn = 512
    return pl.pallas_call(
        _inproj_kernel,
        grid=(m // tm, n // tn),
        in_specs=[pl.BlockSpec((tm, d), lambda i, j: (i, 0)),
                  pl.BlockSpec((1, d), lambda i, j: (0, 0)),
                  pl.BlockSpec((MOD_ROWS, d), lambda i, j: (mod_idx(i), 0)),
                  pl.BlockSpec((MOD_ROWS, d), lambda i, j: (mod_idx(i), 0)),
                  pl.BlockSpec((1, d, tn), lambda i, j: (layer, 0, j))],
        out_specs=pl.BlockSpec((tm, tn), lambda i, j: (i, j)),
        out_shape=jax.ShapeDtypeStruct((m, n), F32),
        scratch_shapes=[pltpu.VMEM((tm, d), BF16)],
        compiler_params=_params(("parallel", "arbitrary")),
        name="inproj",
    )(x, g, scale_t, shift_t, w)


def _softplus(x):
    return jnp.maximum(x, 0.0) + jnp.log1p(jnp.exp(-jnp.abs(x)))


RW_TT = 128
HALF = LANES // 2


def _rwkv_pre_math(z, zp, mu_ref, w0_ref, a0_ref, kkw_ref, ka_ref, rk_ref, wl_ref, ones_ref):
    zs = z + (zp - z) * mu_ref[...]
    r, k, v = zs[:, :RW], zs[:, RW:2 * RW], zs[:, 2 * RW:3 * RW]
    l0 = zs[:, 3 * RW:3 * RW + LANES]
    lane = lax.broadcasted_iota(jnp.int32, l0.shape, 1)
    l0 = jnp.where(lane < R_DECAY, jnp.tanh(l0), l0)
    l1 = jax.nn.sigmoid(zs[:, 3 * RW + LANES:])
    lin = jnp.concatenate([l0, l1], axis=1).astype(BF16)
    lo = _dot(lin, wl_ref[...])
    w_log = -_softplus(-(w0_ref[...] + lo[:, :RW])) - 0.5
    a = jax.nn.sigmoid(a0_ref[...] + lo[:, RW:2 * RW])
    ones_bd = ones_ref[...]
    kk = k * kkw_ref[...]
    kkn = kk / jnp.maximum(jnp.sqrt(_segsum(kk * kk, ones_bd)), 1e-12)
    kh = k * (1.0 + (a - 1.0) * ka_ref[...])
    bonus = _segsum(r * kh * rk_ref[...], ones_bd) * v
    return r, jnp.exp(-jnp.exp(w_log)), kh, v, kkn, kkn * a, lo[:, 2 * RW:], bonus


def _rwkv_pre_prompt_kernel(z_ref, prev_ref, mu_ref, w0_ref, a0_ref, kkw_ref, ka_ref, rk_ref, wl_ref,
                            ones_ref, r_o, w_o, k_o, v_o, kk_o, b_o, g_o, bon_o, zt_scr):
    bi = pl.program_id(1)
    z = z_ref[...]
    row = lax.broadcasted_iota(jnp.int32, z.shape, 0)
    zp = jnp.where(row == 0, prev_ref[0], pltpu.roll(z, 1, axis=0))
    r, w, kh, v, kkn, b, g, bonus = _rwkv_pre_math(z, zp, mu_ref, w0_ref, a0_ref, kkw_ref, ka_ref,
                                                   rk_ref, wl_ref, ones_ref)
    g_o[...] = g
    bon_o[...] = bonus
    for n, x in enumerate((r, w, kh, v, kkn, b)):
        for p in range(RW // LANES):
            row0 = pl.multiple_of((bi * (RW // LANES) + p) * LANES, LANES)
            zt_scr[n, pl.ds(row0, LANES), :] = x[:, p * LANES:(p + 1) * LANES].T

    @pl.when(bi == pl.num_programs(1) - 1)
    def _():
        n_bh = zt_scr.shape[1] // HEAD

        def channel_rows(n, c0):
            cols = [[] for _ in range(SUBLANES)]
            for j in range(n_bh // SUBLANES):
                blk = [zt_scr[n, pl.ds(pl.multiple_of((j * SUBLANES + k) * HEAD + c0, SUBLANES), SUBLANES), :]
                       for k in range(SUBLANES)]
                for m, piece in enumerate(_sublane_transpose8(blk)):
                    cols[m].append(piece)
            return [jnp.concatenate(pieces, axis=0) for pieces in cols]

        def store_rows(o_ref, mats, rows_per_t, r0):
            for jt in range(RW_TT // SUBLANES):
                blk = [mt[jt * SUBLANES:(jt + 1) * SUBLANES] for mt in mats]
                for s, piece in enumerate(_sublane_transpose8(blk)):
                    row = pl.multiple_of((jt * SUBLANES + s) * rows_per_t + r0, SUBLANES)
                    o_ref[pl.ds(row, SUBLANES), :] = piece

        for n, o_ref in enumerate((r_o, w_o, k_o, v_o, kk_o, b_o)):
            if o_ref is v_o:
                def group(gi, c, n=n, o_ref=o_ref):
                    c0 = pl.multiple_of(gi * 2 * SUBLANES, 2 * SUBLANES)
                    a = channel_rows(n, c0) + channel_rows(n, c0 + SUBLANES)
                    mats = [jnp.concatenate([a[2 * q], a[2 * q + 1]], axis=0).T for q in range(SUBLANES)]
                    store_rows(o_ref, mats, HEAD // 2, pl.multiple_of(gi * SUBLANES, SUBLANES))
                    return c

                lax.fori_loop(0, HEAD // (2 * SUBLANES), group, 0)
            else:
                def group(gi, c, n=n, o_ref=o_ref):
                    c0 = pl.multiple_of(gi * SUBLANES, SUBLANES)
                    mats = [jnp.concatenate([a, a], axis=0).T for a in channel_rows(n, c0)]
                    store_rows(o_ref, mats, HEAD, c0)
                    return c

                lax.fori_loop(0, HEAD // SUBLANES, group, 0)


def _rwkv_pre_prompt(z_main, prev, n_b, n_t, mu, w0, a0, kkw, ka, rk, wl, ones_bd):
    nt = n_t // RW_TT
    vec = lambda w: pl.BlockSpec((1, w), lambda i, b: (0, 0))
    keyed = pl.BlockSpec((RW_TT * HEAD, LANES), lambda i, b: (i, 0), pipeline_mode=pl.Buffered(1))
    paired = pl.BlockSpec((RW_TT * (HEAD // 2), LANES), lambda i, b: (i, 0), pipeline_mode=pl.Buffered(1))
    keyed_shape = jax.ShapeDtypeStruct((n_t * HEAD, LANES), F32)
    paired_shape = jax.ShapeDtypeStruct((n_t * (HEAD // 2), LANES), F32)
    nat = pl.BlockSpec((RW_TT, RW), lambda i, b: (b * nt + i, 0))
    return pl.pallas_call(
        _rwkv_pre_prompt_kernel,
        grid=(nt, n_b),
        in_specs=[pl.BlockSpec((RW_TT, RW_PAD), lambda i, b: (b * nt + i, 0)),
                  pl.BlockSpec((1, 1, RW_PAD), lambda i, b: (b * nt + i, 0, 0)),
                  vec(RW_PAD), vec(RW), vec(RW), vec(RW), vec(RW), vec(RW),
                  pl.BlockSpec((LORA_IN, 3 * RW), lambda i, b: (0, 0)),
                  pl.BlockSpec((LANES, RW), lambda i, b: (0, 0))],
        out_specs=[keyed] * 3 + [paired] + [keyed] * 2 + [nat] * 2,
        out_shape=[keyed_shape] * 3 + [paired_shape] + [keyed_shape] * 2
        + [jax.ShapeDtypeStruct((n_b * n_t, RW), F32)] * 2,
        scratch_shapes=[pltpu.VMEM((6, n_b * RW, RW_TT), F32)],
        compiler_params=_params(("arbitrary", "arbitrary")),
        name="rwkv_pre_prompt",
    )(z_main, prev, mu, w0, a0, kkw, ka, rk, wl, ones_bd)


def _rwkv_pre_sample_kernel(z_ref, zp_ref, st_ref, mu_ref, w0_ref, a0_ref, kkw_ref, ka_ref, rk_ref, wl_ref,
                            ones_ref, r_o, w_o, k_o, v_o, kk_o, b_o, g_o, bon_o):
    zp = jnp.where(pl.program_id(0) == 0, st_ref[...], zp_ref[...])
    r, w, kh, v, kkn, b, g, bonus = _rwkv_pre_math(z_ref[...], zp, mu_ref, w0_ref, a0_ref, kkw_ref, ka_ref,
                                                   rk_ref, wl_ref, ones_ref)
    g_o[...] = g
    bon_o[...] = bonus
    for x, o_ref in zip((r, w, kh, v, kkn, b), (r_o, w_o, k_o, v_o, kk_o, b_o)):
        for p in range(RW // LANES):
            xt = x[:, p * LANES:(p + 1) * LANES].T
            o_ref[0, :, 2 * p * LANES:(2 * p + 1) * LANES] = xt[:HEAD]
            o_ref[0, :, (2 * p + 1) * LANES:(2 * p + 2) * LANES] = xt[HEAD:]


def _rwkv_pre_sample(z_main, state, row0, n_b, n_t, mu, w0, a0, kkw, ka, rk, wl, ones_bd):
    blk0 = row0 // n_b
    vec = lambda w: pl.BlockSpec((1, w), lambda t: (0, 0))
    keyed = pl.BlockSpec((1, HEAD, HEADS * n_b), lambda t: (t, 0, 0))
    nat = pl.BlockSpec((n_b, RW), lambda t: (t, 0))
    return pl.pallas_call(
        _rwkv_pre_sample_kernel,
        grid=(n_t,),
        in_specs=[pl.BlockSpec((n_b, RW_PAD), lambda t: (blk0 + t, 0)),
                  pl.BlockSpec((n_b, RW_PAD), lambda t: (blk0 + jnp.maximum(t - 1, 0), 0)),
                  pl.BlockSpec((n_b, RW_PAD), lambda t: (0, 0)),
                  vec(RW_PAD), vec(RW), vec(RW), vec(RW), vec(RW), vec(RW),
                  pl.BlockSpec((LORA_IN, 3 * RW), lambda t: (0, 0)),
                  pl.BlockSpec((LANES, RW), lambda t: (0, 0))],
        out_specs=[keyed] * 6 + [nat] * 2,
        out_shape=[jax.ShapeDtypeStruct((n_t, HEAD, HEADS * n_b), F32)] * 6
        + [jax.ShapeDtypeStruct((n_t * n_b, RW), F32)] * 2,
        compiler_params=_params(("arbitrary",)),
        name="rwkv_pre_sample",
    )(z_main, z_main, state, mu, w0, a0, kkw, ka, rk, wl, ones_bd)


def _rwkv_scan_kernel(nat_state, r_ref, w_ref, k_ref, kk_ref, b_ref, v_ref, s0_ref, y_ref, sT_ref, s_scr):
    tc = pl.program_id(1)
    n_t = r_ref.shape[0]
    n_i = s_scr.shape[0]

    @pl.when(tc == 0)
    def _():
        if nat_state:
            s_scr[...] = s0_ref[...].T.reshape(s_scr.shape)
        else:
            s_scr[...] = s0_ref[...]

    def step(t, carry):
        r_t, w_t, k_t, kk_t, b_t = r_ref[t], w_ref[t], k_ref[t], kk_ref[t], b_ref[t]

        def rows8(ib, c2):
            i0 = pl.multiple_of(ib * SUBLANES, SUBLANES)
            v8 = v_ref[t, pl.ds(i0, SUBLANES), :]
            ys = []
            for ii in range(SUBLANES):
                s_old = s_scr[i0 + ii]
                sa = jnp.sum(s_old * kk_t, axis=0, keepdims=True)
                s_new = s_old * w_t - sa * b_t + v8[ii:ii + 1] * k_t
                ys.append(jnp.sum(s_new * r_t, axis=0, keepdims=True))
                s_scr[i0 + ii] = s_new
            y_ref[t, pl.ds(i0, SUBLANES), :] = jnp.concatenate(ys, axis=0)
            return c2

        return lax.fori_loop(0, n_i // SUBLANES, rows8, carry)

    lax.fori_loop(0, n_t, step, 0)

    @pl.when(tc == pl.num_programs(1) - 1)
    def _():
        if nat_state:
            sT_ref[...] = s_scr[...].reshape(n_i * s_scr.shape[1], LANES).T
        else:
            sT_ref[...] = s_scr[...]


def _rwkv_scan_prompt(r, w, k, kk, b, v, s0, tchunk):
    n_t = r.shape[0]
    n_i = s0.shape[0]
    op = pl.BlockSpec((tchunk, HEAD, LANES), lambda l, t: (t, 0, 0))
    vy = pl.BlockSpec((tchunk, n_i, LANES), lambda l, t: (t, 0, 0))
    st = pl.BlockSpec((n_i, HEAD, LANES), lambda l, t: (0, 0, 0))
    return pl.pallas_call(
        functools.partial(_rwkv_scan_kernel, False),
        grid=(1, n_t // tchunk),
        in_specs=[op] * 5 + [vy, st],
        out_specs=[vy, st],
        out_shape=[jax.ShapeDtypeStruct((n_t, n_i, LANES), F32),
                   jax.ShapeDtypeStruct((n_i, HEAD, LANES), F32)],
        scratch_shapes=[pltpu.VMEM((n_i, HEAD, LANES), F32)],
        compiler_params=_params(("arbitrary", "arbitrary")),
        name="rwkv_scan_prompt",
    )(r, w, k, kk, b, v, s0)


def _rwkv_scan_sample(r, w, k, kk, b, v, s0, layer):
    n_t, _, n_l = r.shape
    n_b = n_l // HEADS
    op = pl.BlockSpec((n_t, HEAD, LANES), lambda h, t: (0, 0, h))
    st = pl.BlockSpec((n_b, HEAD * HEAD), lambda h, t: (layer, h))
    st_out = pl.BlockSpec((n_b, HEAD * HEAD), lambda h, t: (0, h))
    return pl.pallas_call(
        functools.partial(_rwkv_scan_kernel, True),
        grid=(HEADS, 1),
        in_specs=[op] * 6 + [st],
        out_specs=[op, st_out],
        out_shape=[jax.ShapeDtypeStruct((n_t, HEAD, n_l), F32),
                   jax.ShapeDtypeStruct((n_b, HEADS * HEAD * HEAD), F32)],
        scratch_shapes=[pltpu.VMEM((HEAD, HEAD, LANES), F32)],
        compiler_params=_params(("parallel", "arbitrary")),
        name="rwkv_scan_sample",
    )(r, w, k, kk, b, v, s0)


def _rwkv_post_math(y, bon_ref, g_ref, lg_ref, lb_ref, ones_ref):
    ones_bd = ones_ref[...]
    yc = y - _segsum(y, ones_bd) * (1.0 / HEAD)
    var = _segsum(yc * yc, ones_bd) * (1.0 / HEAD)
    yn = yc * lax.rsqrt(var + GN_EPS) * lg_ref[...] + lb_ref[...]
    return ((yn + bon_ref[...]) * g_ref[...]).astype(BF16)


def _rwkv_post_prompt_kernel(y_ref, bon_ref, g_ref, lg_ref, lb_ref, ones_ref, o_ref, zt_scr):
    bi = pl.program_id(1)

    @pl.when(bi == 0)
    def _():
        n_bh = zt_scr.shape[0] // HEAD

        def group(gi, c):
            q0 = pl.multiple_of(gi * SUBLANES, SUBLANES)
            mats = [[] for _ in range(SUBLANES)]
            for jt in range(RW_TT // SUBLANES):
                blk = [y_ref[pl.ds(pl.multiple_of((jt * SUBLANES + s) * (HEAD // 2) + q0, SUBLANES), SUBLANES), :]
                       for s in range(SUBLANES)]
                for m, piece in enumerate(_sublane_transpose8(blk)):
                    mats[m].append(piece)
            vals = []
            for pieces in mats:
                mt = jnp.concatenate(pieces, axis=0).T
                vals += [mt[:HALF], mt[HALF:]]
            for half in range(2):
                for j in range(n_bh // SUBLANES):
                    blk = [v[j * SUBLANES:(j + 1) * SUBLANES] for v in vals[half * SUBLANES:(half + 1) * SUBLANES]]
                    for k, piece in enumerate(_sublane_transpose8(blk)):
                        row = pl.multiple_of((j * SUBLANES + k) * HEAD + 2 * q0 + half * SUBLANES, SUBLANES)
                        zt_scr[pl.ds(row, SUBLANES), :] = piece
            return c

        lax.fori_loop(0, HEAD // (2 * SUBLANES), group, 0)

    pieces = []
    for p in range(RW // LANES):
        row0 = pl.multiple_of((bi * (RW // LANES) + p) * LANES, LANES)
        pieces.append(zt_scr[pl.ds(row0, LANES), :].T)
    y = jnp.concatenate(pieces, axis=1)
    o_ref[...] = _rwkv_post_math(y, bon_ref, g_ref, lg_ref, lb_ref, ones_ref)


def _rwkv_post_prompt(y, bonus, g, n_b, n_t, lnx_g, lnx_b, ones_bd):
    nt = n_t // RW_TT
    nat = pl.BlockSpec((RW_TT, RW), lambda i, b: (b * nt + i, 0))
    vec = pl.BlockSpec((1, RW), lambda i, b: (0, 0))
    return pl.pallas_call(
        _rwkv_post_prompt_kernel,
        grid=(nt, n_b),
        in_specs=[pl.BlockSpec((RW_TT * (HEAD // 2), LANES), lambda i, b: (i, 0)), nat, nat, vec, vec,
                  pl.BlockSpec((LANES, RW), lambda i, b: (0, 0))],
        out_specs=nat,
        out_shape=jax.ShapeDtypeStruct((n_b * n_t, RW), BF16),
        scratch_shapes=[pltpu.VMEM((n_b * RW, RW_TT), F32)],
        compiler_params=_params(("arbitrary", "arbitrary")),
        name="rwkv_post_prompt",
    )(y, bonus, g, lnx_g, lnx_b, ones_bd)


def _rwkv_post_sample_kernel(y_ref, bon_ref, g_ref, lg_ref, lb_ref, ones_ref, o_ref):
    pieces = []
    for p in range(RW // LANES):
        m = jnp.concatenate([y_ref[0, :, 2 * p * LANES:(2 * p + 1) * LANES],
                             y_ref[0, :, (2 * p + 1) * LANES:(2 * p + 2) * LANES]], axis=0)
        pieces.append(m.T)
    y = jnp.concatenate(pieces, axis=1)
    o_ref[...] = _rwkv_post_math(y, bon_ref, g_ref, lg_ref, lb_ref, ones_ref)


def _rwkv_post_sample(y, bonus, g, n_b, n_t, lnx_g, lnx_b, ones_bd):
    nat = pl.BlockSpec((n_b, RW), lambda t: (t, 0))
    vec = pl.BlockSpec((1, RW), lambda t: (0, 0))
    return pl.pallas_call(
        _rwkv_post_sample_kernel,
        grid=(n_t,),
        in_specs=[pl.BlockSpec((1, HEAD, HEADS * n_b), lambda t: (t, 0, 0)), nat, nat, vec, vec,
                  pl.BlockSpec((LANES, RW), lambda t: (0, 0))],
        out_specs=nat,
        out_shape=jax.ShapeDtypeStruct((n_t * n_b, RW), BF16),
        compiler_params=_params(("parallel",)),
        name="rwkv_post_sample",
    )(y, bonus, g, lnx_g, lnx_b, ones_bd)


def _pool_prompt_kernel(start_pos, z_ref, pw_ref, ps_ref, y_ref):
    n = z_ref.shape[0]
    row = lax.broadcasted_iota(jnp.int32, (n, POOL_G), 0)
    for gi, win in enumerate(WINDOWS):
        sl = slice(gi * POOL_G, (gi + 1) * POOL_G)
        x = z_ref[:, sl]
        s = x
        k = 1
        while k < win:
            s = s + jnp.where(row >= k, pltpu.roll(s, k, axis=0), 0.0)
            k *= 2
        cnt = jnp.minimum(win, row + (start_pos + 1)).astype(F32)
        d = s / cnt - x
        y_ref[:, sl] = (_dot(d.astype(BF16), pw_ref[gi]) * ps_ref[:, sl]).astype(BF16)


def _pool_prompt(z_main, n_b, n_t, pw, ps):
    cb = RW_PAD // POOL_W
    return pl.pallas_call(
        functools.partial(_pool_prompt_kernel, 0),
        grid=(n_b,),
        in_specs=[pl.BlockSpec((n_t, POOL_W), lambda b: (b, cb)),
                  pl.BlockSpec((len(WINDOWS), POOL_G, POOL_G), lambda b: (0, 0, 0)),
                  pl.BlockSpec((1, POOL_W), lambda b: (0, 0))],
        out_specs=pl.BlockSpec((n_t, POOL_W), lambda b: (b, 0)),
        out_shape=jax.ShapeDtypeStruct((n_b * n_t, POOL_W), BF16),
        compiler_params=_params(("parallel",)),
        name="pool_prompt",
    )(z_main, pw, ps)


def _pool_sample_kernel(start_pos, n_t, full_ref, pw_ref, ps_ref, y_ref):
    n_b = full_ref.shape[1]
    for gi, win in enumerate(WINDOWS):
        sl = slice(gi * POOL_G, (gi + 1) * POOL_G)
        f = full_ref[:, :, sl]
        s = f
        k = 1
        while k < win:
            s = s[k:] + s[:-k]
            k *= 2
        s = s[s.shape[0] - n_t:]
        x = f[POOL_BUF:]
        pos = start_pos + lax.broadcasted_iota(jnp.int32, s.shape, 0)
        cnt = jnp.minimum(win, pos + 1).astype(F32)
        d = (s / cnt - x).reshape(n_t * n_b, POOL_G)
        y_ref[:, sl] = (_dot(d.astype(BF16), pw_ref[gi]) * ps_ref[:, sl]).astype(BF16)


def _pool_sample(full, n_t, start_pos, pw, ps):
    n_f, n_b, _ = full.shape
    return pl.pallas_call(
        functools.partial(_pool_sample_kernel, start_pos, n_t),
        grid=(1,),
        in_specs=[pl.BlockSpec((n_f, n_b, POOL_W), lambda i: (0, 0, 0)),
                  pl.BlockSpec((len(WINDOWS), POOL_G, POOL_G), lambda i: (0, 0, 0)),
                  pl.BlockSpec((1, POOL_W), lambda i: (0, 0))],
        out_specs=pl.BlockSpec((n_t * n_b, POOL_W), lambda i: (0, 0)),
        out_shape=jax.ShapeDtypeStruct((n_t * n_b, POOL_W), BF16),
        compiler_params=_params(("arbitrary",)),
        name="pool_sample",
    )(full, pw, ps)


def _s5_tail(x_all, u, cmat_ref, d_ref, gw_ref, gb_ref):
    y = _dot(x_all.astype(BF16), cmat_ref[...]) + d_ref[...] * u
    y = jax.nn.gelu(y)
    return (y * jax.nn.sigmoid(_dot(y.astype(BF16), gw_ref[...]) + gb_ref[...])).astype(BF16)


S5_LC = 512


def _s5_prompt_kernel(u_ref, bmat_ref, cmat_ref, lam_ref, pw_ref, d_ref, gw_ref, gb_ref,
                      y_ref, xre_o, xim_o, x_scr, c_scr):
    tc = pl.program_id(1)

    @pl.when(tc == 0)
    def _():
        c_scr[...] = jnp.zeros_like(c_scr)

    u = u_ref[...]
    n = u.shape[0]
    x_scr[...] = _dot(u.astype(BF16), bmat_ref[...])
    row = lax.broadcasted_iota(jnp.int32, (SUBLANES, S5_LC), 0)
    for lc in range(S5_N // S5_LC):
        re_sl = pl.ds(lc * S5_LC, S5_LC)
        im_sl = pl.ds(S5_N + lc * S5_LC, S5_LC)
        lam = [(lam_ref[2 * j:2 * j + 1, re_sl], lam_ref[2 * j + 1:2 * j + 2, re_sl]) for j in range(3)]
        p_re, p_im = pw_ref[:, re_sl], pw_ref[:, im_sl]

        def blk(rb, carry, re_sl=re_sl, im_sl=im_sl, lam=lam, p_re=p_re, p_im=p_im):
            c_re, c_im = carry
            rows = pl.ds(pl.multiple_of(rb * SUBLANES, SUBLANES), SUBLANES)
            xr, xi = x_scr[rows, re_sl], x_scr[rows, im_sl]
            for j, (l_re, l_im) in enumerate(lam):
                kshift = 1 << j
                sr = jnp.where(row >= kshift, pltpu.roll(xr, kshift, axis=0), 0.0)
                si = jnp.where(row >= kshift, pltpu.roll(xi, kshift, axis=0), 0.0)
                xr, xi = xr + (l_re * sr - l_im * si), xi + (l_re * si + l_im * sr)
            xr, xi = xr + (p_re * c_re - p_im * c_im), xi + (p_re * c_im + p_im * c_re)
            x_scr[rows, re_sl] = xr
            x_scr[rows, im_sl] = xi
            return xr[SUBLANES - 1:], xi[SUBLANES - 1:]

        c_re, c_im = lax.fori_loop(0, n // SUBLANES, blk, (c_scr[0:1, re_sl], c_scr[0:1, im_sl]))
        c_scr[0:1, re_sl] = c_re
        c_scr[0:1, im_sl] = c_im

    y_ref[...] = _s5_tail(x_scr[...], u, cmat_ref, d_ref, gw_ref, gb_ref)

    @pl.when(tc == pl.num_programs(1) - 1)
    def _():
        xre_o[0] = c_scr[0:1, :S5_N]
        xim_o[0] = c_scr[0:1, S5_N:]


def _s5_prompt(z_main, n_b, n_t, tt, bmat, cmat, lam_t, pw_t, dskip, gw, gb):
    cb = (RW_PAD + POOL_W) // S5_W
    n_tc = n_t // tt
    const = lambda shape: pl.BlockSpec(shape, lambda b, t: (0,) * len(shape))
    return pl.pallas_call(
        _s5_prompt_kernel,
        grid=(n_b, n_tc),
        in_specs=[pl.BlockSpec((tt, S5_W), lambda b, t: (b * n_tc + t, cb)),
                  const((S5_W, 2 * S5_N)), const((2 * S5_N, S5_W)),
                  const((SUBLANES, S5_N)), const((SUBLANES, 2 * S5_N)),
                  const((1, S5_W)), const((S5_W, S5_W)), const((1, S5_W))],
        out_specs=[pl.BlockSpec((tt, S5_W), lambda b, t: (b * n_tc + t, 0)),
                   pl.BlockSpec((1, 1, S5_N), lambda b, t: (b, 0, 0)),
                   pl.BlockSpec((1, 1, S5_N), lambda b, t: (b, 0, 0))],
        out_shape=[jax.ShapeDtypeStruct((n_b * n_t, S5_W), BF16),
                   jax.ShapeDtypeStruct((n_b, 1, S5_N), F32),
                   jax.ShapeDtypeStruct((n_b, 1, S5_N), F32)],
        scratch_shapes=[pltpu.VMEM((tt, 2 * S5_N), F32), pltpu.VMEM((SUBLANES, 2 * S5_N), F32)],
        compiler_params=_params(("parallel", "arbitrary")),
        name="s5_prompt",
    )(z_main, bmat, cmat, lam_t, pw_t, dskip, gw, gb)


def _s5_sample_kernel(n_t, u_ref, x0re_ref, x0im_ref, bmat_ref, cmat_ref, lam_ref, d_ref, gw_ref,
                      gb_ref, y_ref, xre_o, xim_o, x_scr):
    u = u_ref[...]
    n_b = u.shape[0] // n_t
    x_scr[...] = _dot(u.astype(BF16), bmat_ref[...])
    for lc in range(S5_N // S5_LC):
        re_sl = pl.ds(lc * S5_LC, S5_LC)
        im_sl = pl.ds(S5_N + lc * S5_LC, S5_LC)
        l_re, l_im = lam_ref[0:1, re_sl], lam_ref[1:2, re_sl]
        xr, xi = x0re_ref[:, re_sl], x0im_ref[:, re_sl]
        for t in range(n_t):
            rows = pl.ds(t * n_b, n_b)
            xr, xi = (l_re * xr - l_im * xi + x_scr[rows, re_sl],
                      l_re * xi + l_im * xr + x_scr[rows, im_sl])
            x_scr[rows, re_sl] = xr
            x_scr[rows, im_sl] = xi
        xre_o[:, re_sl] = xr
        xim_o[:, re_sl] = xi
    y_ref[...] = _s5_tail(x_scr[...], u, cmat_ref, d_ref, gw_ref, gb_ref)


def _s5_sample(z_main, row0, n_b, n_t, x0re, x0im, bmat, cmat, lam_t, dskip, gw, gb):
    rows = n_b * n_t
    cb = (RW_PAD + POOL_W) // S5_W
    const = lambda shape: pl.BlockSpec(shape, lambda i: (0,) * len(shape))
    return pl.pallas_call(
        functools.partial(_s5_sample_kernel, n_t),
        grid=(1,),
        in_specs=[pl.BlockSpec((rows, S5_W), lambda i: (row0 // rows, cb)),
                  const((n_b, S5_N)), const((n_b, S5_N)),
                  const((S5_W, 2 * S5_N)), const((2 * S5_N, S5_W)), const((SUBLANES, S5_N)),
                  const((1, S5_W)), const((S5_W, S5_W)), const((1, S5_W))],
        out_specs=[const((rows, S5_W)), const((n_b, S5_N)), const((n_b, S5_N))],
        out_shape=[jax.ShapeDtypeStruct((rows, S5_W), BF16),
                   jax.ShapeDtypeStruct((n_b, S5_N), F32),
                   jax.ShapeDtypeStruct((n_b, S5_N), F32)],
        scratch_shapes=[pltpu.VMEM((rows, 2 * S5_N), F32)],
        compiler_params=_params(("arbitrary",)),
        name="s5_sample",
    )(z_main, x0re, x0im, bmat, cmat, lam_t, dskip, gw, gb)


def _branch_merge_kernel(ya_ref, yb_ref, yc_ref, zg_ref, wbr_ref, o_ref):
    d = o_ref.shape[1]
    m = jax.nn.sigmoid(zg_ref[:, :d]) * _dot(ya_ref[...], wbr_ref[:RW, :])
    m = m + jax.nn.sigmoid(zg_ref[:, d:2 * d]) * _dot(yb_ref[...], wbr_ref[RW:RW + POOL_W, :])
    m = m + jax.nn.sigmoid(zg_ref[:, 2 * d:]) * _dot(yc_ref[...], wbr_ref[RW + POOL_W:, :])
    o_ref[...] = m.astype(BF16)


def _outproj_kernel(m_ref, wout_ref, x_ref, g1_ref, n2_ref, sc2_ref, sh2_ref, rw_ref, rb_ref,
                    xo_ref, h2_ref, lg_ref):
    rows = x_ref.shape[0]
    xn = x_ref[...] + _tile_rows(g1_ref[...], rows) * _dot(m_ref[...], wout_ref[...])
    xo_ref[...] = xn
    h2 = _norm_mod(xn, n2_ref[...], sc2_ref[...], sh2_ref[...])
    h2_ref[...] = h2
    lg_ref[...] = jnp.dot(h2, rw_ref[...], precision=lax.Precision.HIGHEST,
                          preferred_element_type=F32) + rb_ref[...]


def _merge(ya, yb, yc, zg, wbr, wout, x, g1_t, n2g, sc2_t, sh2_t, rw, rb, mod_idx, tm):
    m, d = x.shape
    row = lambda w: pl.BlockSpec((tm, w), lambda i: (i, 0))
    const = lambda shape: pl.BlockSpec(shape, lambda i: (0,) * len(shape), pipeline_mode=pl.Buffered(1))
    mod = pl.BlockSpec((MOD_ROWS, d), lambda i: (mod_idx(i), 0))
    merged = pl.pallas_call(
        _branch_merge_kernel,
        grid=(m // tm,),
        in_specs=[row(RW), row(POOL_W), row(S5_W), row(3 * d), const(wbr.shape)],
        out_specs=row(d),
        out_shape=jax.ShapeDtypeStruct((m, d), BF16),
        compiler_params=_params(("parallel",)),
        name="branch_merge",
    )(ya, yb, yc, zg, wbr)
    return pl.pallas_call(
        _outproj_kernel,
        grid=(m // tm,),
        in_specs=[row(d), const(wout.shape), row(d), mod, const((1, d)), mod, mod, const(rw.shape),
                  const((1, LANES))],
        out_specs=[row(d), row(d), row(LANES)],
        out_shape=[jax.ShapeDtypeStruct((m, d), F32), jax.ShapeDtypeStruct((m, d), F32),
                   jax.ShapeDtypeStruct((m, LANES), F32)],
        compiler_params=_params(("parallel",)),
        name="outproj",
    )(merged, wout, x, g1_t, n2g, sc2_t, sh2_t, rw, rb)


MOE_BM = 512
MOE_TF = 512


def _moe_kernel(be_ref, nb_ref, x_ref, wg_ref, wu_ref, bg_ref, bu_ref, wd_ref, bd_ref, o_ref):
    s, f = pl.program_id(0), pl.program_id(1)
    used = s < nb_ref[0]

    @pl.when(jnp.logical_and(jnp.logical_not(used), f == 0))
    def _():
        o_ref[...] = jnp.zeros_like(o_ref)

    @pl.when(used)
    def _():
        x = x_ref[...].astype(BF16)
        g = _dot(x, wg_ref[0].astype(BF16)) + bg_ref[0]
        u = _dot(x, wu_ref[0].astype(BF16)) + bu_ref[0]
        g = jnp.minimum(g, SW_LIMIT)
        u = jnp.clip(u, -SW_LIMIT, SW_LIMIT)
        act = (u + 1.0) * (g * jax.nn.sigmoid(SW_ALPHA * g))
        part = _dot(act.astype(BF16), wd_ref[0].astype(BF16))

        @pl.when(f == 0)
        def _():
            o_ref[...] = part + bd_ref[0]

        @pl.when(f > 0)
        def _():
            o_ref[...] += part


def _moe(xs, blk_e, n_used, layer, w_gu, b_gu, w_down, b_down):
    rows, d = xs.shape
    n_l, n_e, _, two_ff = w_gu.shape
    dff = two_ff // 2
    nf = dff // MOE_TF
    n_blk = rows // MOE_BM
    e0 = layer * n_e

    def f_eff(s, f, nb):
        return jnp.where(s < nb[0], f, nf - 1)

    grid_spec = pltpu.PrefetchScalarGridSpec(
        num_scalar_prefetch=2,
        grid=(n_blk, nf),
        in_specs=[
            pl.BlockSpec((MOE_BM, d), lambda s, f, be, nb: (jnp.minimum(s, nb[0] - 1), 0)),
            pl.BlockSpec((1, d, MOE_TF), lambda s, f, be, nb: (e0 + be[s], 0, f_eff(s, f, nb))),
            pl.BlockSpec((1, d, MOE_TF), lambda s, f, be, nb: (e0 + be[s], 0, nf + f_eff(s, f, nb))),
            pl.BlockSpec((1, 1, MOE_TF), lambda s, f, be, nb: (e0 + be[s], 0, f_eff(s, f, nb))),
            pl.BlockSpec((1, 1, MOE_TF), lambda s, f, be, nb: (e0 + be[s], 0, nf + f_eff(s, f, nb))),
            pl.BlockSpec((1, MOE_TF, d), lambda s, f, be, nb: (e0 + be[s], f_eff(s, f, nb), 0)),
            pl.BlockSpec((1, 1, d), lambda s, f, be, nb: (e0 + be[s], 0, 0)),
        ],
        out_specs=pl.BlockSpec((MOE_BM, d), lambda s, f, be, nb: (s, 0)),
    )
    w_gu = w_gu.reshape(n_l * n_e, d, two_ff)
    b_gu = b_gu.reshape(n_l * n_e, 1, two_ff)
    return pl.pallas_call(
        _moe_kernel,
        grid_spec=grid_spec,
        out_shape=jax.ShapeDtypeStruct((rows, d), F32),
        compiler_params=_params(("arbitrary", "arbitrary")),
        name="moe",
    )(blk_e, n_used, xs, w_gu, w_gu, b_gu, b_gu, w_down.reshape(n_l * n_e, dff, d),
      b_down.reshape(n_l * n_e, 1, d))


def _route(logits, n_rows_pad):
    n_tok = logits.shape[0]
    top_v, top_e = lax.top_k(logits, TOP_K)
    gate = jax.nn.softmax(top_v, axis=-1)
    flat_e = top_e.reshape(-1)
    onehot = (flat_e[:, None] == jnp.arange(N_EXP, dtype=jnp.int32)[None, :]).astype(jnp.int32)
    csum = jnp.cumsum(onehot, axis=0)
    counts = csum[-1]
    rank = jnp.sum((csum - 1) * onehot, axis=1)
    padded = (counts + MOE_BM - 1) // MOE_BM * MOE_BM
    pad_end = jnp.cumsum(padded)
    pad_start = pad_end - padded
    dest = (pad_start[flat_e] + rank).astype(jnp.int32)
    flat_tok = jnp.arange(n_tok * TOP_K, dtype=jnp.int32) // TOP_K
    rows_tok = jnp.zeros((n_rows_pad,), jnp.int32).at[dest].set(flat_tok)
    n_rows_used = pad_end[-1].astype(jnp.int32)
    starts = jnp.arange(n_rows_pad // MOE_BM, dtype=jnp.int32) * MOE_BM
    starts = jnp.minimum(starts, n_rows_used - MOE_BM)
    blk_e = jnp.minimum(jnp.searchsorted(pad_end, starts, side='right'), N_EXP - 1).astype(jnp.int32)
    return gate, dest.reshape(n_tok, TOP_K), rows_tok, blk_e, n_rows_used.reshape(1)


def _combine_kernel(x_ref, y0_ref, y1_ref, y2_ref, y3_ref, gw_ref, g2_ref, o_ref):
    rows = x_ref.shape[0]
    gw = gw_ref[...]
    acc = gw[:, 0:1] * y0_ref[...]
    for k, y_ref in enumerate((y1_ref, y2_ref, y3_ref), start=1):
        acc = acc + gw[:, k:k + 1] * y_ref[...]
    o_ref[...] = x_ref[...] + _tile_rows(g2_ref[...], rows) * acc


def _combine(x, yg, gw, g2_t, mod_idx, tm):
    m, d = x.shape
    n_i = m // tm
    slab = lambda k: pl.BlockSpec((tm, d), lambda i: (k * n_i + i, 0))
    return pl.pallas_call(
        _combine_kernel,
        grid=(n_i,),
        in_specs=[pl.BlockSpec((tm, d), lambda i: (i, 0)), slab(0), slab(1), slab(2), slab(3),
                  pl.BlockSpec((tm, TOP_K), lambda i: (i, 0)),
                  pl.BlockSpec((MOD_ROWS, d), lambda i: (mod_idx(i), 0))],
        out_specs=pl.BlockSpec((tm, d), lambda i: (i, 0)),
        out_shape=jax.ShapeDtypeStruct((m, d), F32),
        compiler_params=_params(("parallel",)),
        name="combine",
    )(x, yg, yg, yg, yg, gw, g2_t)


def _final_norm_kernel(x_ref, g_ref, o_ref):
    x = x_ref[...]
    o_ref[...] = x * lax.rsqrt(jnp.mean(x * x, axis=-1, keepdims=True) + NORM_EPS) * g_ref[...]


def _final_norm(x, g, tm):
    m, d = x.shape
    return pl.pallas_call(
        _final_norm_kernel,
        grid=(m // tm,),
        in_specs=[pl.BlockSpec((tm, d), lambda i: (i, 0)), pl.BlockSpec((1, d), lambda i: (0, 0))],
        out_specs=pl.BlockSpec((tm, d), lambda i: (i, 0)),
        out_shape=jax.ShapeDtypeStruct((m, d), F32),
        compiler_params=_params(("parallel",)),
        name="final_norm",
    )(x, g)


def _s5_params(a_re, a_im, log_dt, b_re, b_im, c_re, c_im):
    dt = jnp.exp(log_dt)[:, None]
    mag = jnp.exp(a_re * dt)
    lb_re, lb_im = mag * jnp.cos(a_im * dt), mag * jnp.sin(a_im * dt)
    nr, ni = lb_re - 1.0, lb_im
    den = a_re * a_re + a_im * a_im
    f_re = (nr * a_re + ni * a_im) / den
    f_im = (ni * a_re - nr * a_im) / den
    bb_re = f_re[..., None] * b_re - f_im[..., None] * b_im
    bb_im = f_re[..., None] * b_im + f_im[..., None] * b_re
    eye = jnp.eye(S5_G, dtype=F32)
    bd_in = lambda w: jnp.einsum('gpc,gh->gchp', w, eye).reshape(S5_W, S5_N)
    bd_out = lambda w: jnp.einsum('gcp,gh->gphc', w, eye).reshape(S5_N, S5_W)
    bmat = jnp.concatenate([bd_in(bb_re), bd_in(bb_im)], axis=1).astype(BF16)
    cmat = jnp.concatenate([bd_out(c_re), -bd_out(c_im)], axis=0).astype(BF16)
    l_re, l_im = lb_re.reshape(1, S5_N), lb_im.reshape(1, S5_N)
    pows = [(l_re, l_im)]
    for _ in range(SUBLANES - 1):
        p_re, p_im = pows[-1]
        pows.append((p_re * l_re - p_im * l_im, p_re * l_im + p_im * l_re))
    lam_t = jnp.concatenate([pows[0][0], pows[0][1], pows[1][0], pows[1][1], pows[3][0], pows[3][1],
                             jnp.zeros((2, S5_N), F32)], axis=0)
    pw_t = jnp.concatenate([jnp.concatenate([p[0] for p in pows], axis=0),
                            jnp.concatenate([p[1] for p in pows], axis=0)], axis=1)
    return bmat, cmat, lam_t, pw_t


def _pad_cols(w, width):
    return jnp.pad(w, ((0, 0), (0, width - w.shape[1])))


def kernel(x_prompt, x_sample, c_prompt, c_sample, state_wkv, state_shift, state_pool, state_s5_re, state_s5_im, norm1_g, norm2_g, final_norm_g, w_ada, b_ada, w_in, rw_mu, rw_w0, rw_w2, rw_a0, rw_a2, rw_g2, rw_kk, rw_ka, rw_rk, rw_lnx_g, rw_lnx_b, pool_w, pool_scale, s5_a_re, s5_a_im, s5_log_dt, s5_b_re, s5_b_im, s5_c_re, s5_c_im, s5_d, s5_glu_w, s5_glu_b, w_br, w_out, router_w, router_b, moe_w_gu, moe_b_gu, moe_w_down, moe_b_down):
    bp, lp, d = x_prompt.shape
    bs, ls, _ = x_sample.shape
    depth = w_in.shape[0]
    mp, ms = bp * lp, bs * ls
    m = mp + ms
    past_len = 16384
    assert bs == MOD_ROWS and bp * HEADS * 2 == LANES and lp % 1024 == 0 and ms == 1024

    def mod_idx_for(tm):
        n_p = mp // tm
        return lambda i: jnp.where(i < n_p, (i * tm) // lp, bp)

    tm_big, tm_merge, tm_mid = 1024, 512, 256
    ones_bd = jnp.pad(jnp.kron(jnp.eye(HEADS, dtype=F32), jnp.ones((1, HEAD), F32)),
                      ((0, LANES - HEADS), (0, 0))).astype(BF16)

    x = _stack_rows(x_prompt.reshape(mp, d), jnp.swapaxes(x_sample, 0, 1).reshape(ms, d))
    c_all = jnp.concatenate([c_prompt, c_sample, jnp.zeros((4, d), F32)], axis=0)
    mod = _adaln(c_all, w_ada, b_ada)

    mod6 = mod.reshape(depth, mod.shape[1], 6, d)
    tables = jnp.concatenate([jnp.repeat(mod6[:, :bp], MOD_ROWS, axis=1), mod6[:, bp:bp + bs]], axis=1)
    tables = tables.transpose(0, 2, 1, 3)
    w_main_all = jnp.concatenate([w_in[:, :, :RW_PROJ], jnp.zeros((depth, d, RW_PAD - RW_PROJ), F32),
                                  w_in[:, :, RW_PROJ:RW_PROJ + POOL_W + S5_W]], axis=2).astype(BF16)
    w_gate_all = w_in[:, :, RW_PROJ + POOL_W + S5_W:].astype(BF16)
    wl_all = jnp.zeros((depth, LORA_IN, 3 * RW), F32)
    wl_all = wl_all.at[:, :R_DECAY, :RW].set(rw_w2)
    wl_all = wl_all.at[:, R_DECAY:R_DECAY + R_AAA, RW:2 * RW].set(rw_a2)
    wl_all = wl_all.at[:, R_DECAY + R_AAA:R_DECAY + R_AAA + R_GATE, 2 * RW:].set(rw_g2).astype(BF16)
    s5_all = jax.vmap(_s5_params)(s5_a_re, s5_a_im, s5_log_dt, s5_b_re, s5_b_im, s5_c_re, s5_c_im)

    outs_p, outs_s = [], []
    for l in range(depth):
        shift1, scale1, gate1, shift2, scale2, gate2 = (tables[l, i] for i in range(6))

        n1 = norm1_g[l].reshape(1, d)
        z_main = _inproj(x, n1, scale1, shift1, w_main_all, l, mod_idx_for(tm_big), tm_big)
        z_gate = _inproj(x, n1, scale1, shift1, w_gate_all, l, mod_idx_for(tm_big), tm_big)

        mu = _pad_cols(rw_mu[l].reshape(1, RW_PROJ), RW_PAD)
        wl = wl_all[l]
        vecs = [v.reshape(1, RW) for v in (rw_w0[l], rw_a0[l], rw_kk[l], rw_ka[l], rw_rk[l])]
        lnx_g, lnx_b = rw_lnx_g[l].reshape(1, RW), rw_lnx_b[l].reshape(1, RW)
        n_tp = mp // RW_TT
        last_rows = z_main[RW_TT - 1:mp:RW_TT, :RW_PAD]
        prev_p = jnp.concatenate([jnp.zeros((1, RW_PAD), F32), last_rows[:-1]], axis=0)
        first_of_seq = (jnp.arange(n_tp) % (lp // RW_TT) == 0)[:, None]
        prev_p = jnp.where(first_of_seq, 0.0, prev_p).reshape(n_tp, 1, RW_PAD)
        *ops_p, g_p, bon_p = _rwkv_pre_prompt(z_main, prev_p, bp, lp, mu, *vecs, wl, ones_bd)
        ops_p = [a.reshape(lp, a.shape[0] // lp, LANES) for a in ops_p]
        y_pt, st_p = _rwkv_scan_prompt(*ops_p[:3], *ops_p[4:], ops_p[3],
                                       jnp.zeros((HEAD // 2, HEAD, LANES), F32), 64)
        y_a_p = _rwkv_post_prompt(y_pt.reshape(lp * (HEAD // 2), LANES), bon_p, g_p, bp, lp,
                                  lnx_g, lnx_b, ones_bd)
        new_wkv_p = st_p.reshape(HEAD // 2, HEAD, 2, bp, HEADS).transpose(3, 4, 0, 2, 1)
        new_wkv_p = new_wkv_p.reshape(bp, HEADS, HEAD, HEAD)

        *ops_s, g_s, bon_s = _rwkv_pre_sample(z_main, _pad_cols(state_shift[l], RW_PAD), mp, bs, ls,
                                              mu, *vecs, wl, ones_bd)
        y_st, st_s = _rwkv_scan_sample(*ops_s[:3], *ops_s[4:], ops_s[3],
                                       state_wkv.reshape(depth * bs, HEADS * HEAD * HEAD), l)
        y_a_s = _rwkv_post_sample(y_st, bon_s, g_s, bs, ls, lnx_g, lnx_b, ones_bd)
        new_wkv_s = st_s.reshape(bs, HEADS, HEAD, HEAD)
        y_a = jnp.concatenate([y_a_p, y_a_s], axis=0)
        new_shift_p = z_main[lp - 1:mp:lp, :RW_PROJ]
        new_shift_s = z_main[m - bs:, :RW_PROJ]

        pw = pool_w[l].astype(BF16)
        ps = pool_scale[l].reshape(1, POOL_W)
        y_b_p = _pool_prompt(z_main, bp, lp, pw, ps)
        zb_s = z_main[mp:, RW_PAD:RW_PAD + POOL_W].reshape(ls, bs, POOL_W)
        full_s = jnp.concatenate([jnp.swapaxes(state_pool[l], 0, 1), zb_s], axis=0)
        y_b_s = _pool_sample(full_s, ls, past_len, pw, ps)
        y_b = jnp.concatenate([y_b_p, y_b_s], axis=0)
        new_pool_p = z_main[:mp, RW_PAD:RW_PAD + POOL_W].reshape(bp, lp, POOL_W)[:, lp - POOL_BUF:]
        new_pool_s = jnp.swapaxes(full_s[full_s.shape[0] - POOL_BUF:], 0, 1)

        bmat, cmat, lam_t, pw_t = (t[l] for t in s5_all)
        dskip, gw, gb = s5_d[l].reshape(1, S5_W), s5_glu_w[l].astype(BF16), s5_glu_b[l].reshape(1, S5_W)
        y_c_p, re_p, im_p = _s5_prompt(z_main, bp, lp, 1024, bmat, cmat, lam_t, pw_t, dskip, gw, gb)
        y_c_s, re_s, im_s = _s5_sample(z_main, mp, bs, ls, state_s5_re[l].reshape(bs, S5_N),
                                       state_s5_im[l].reshape(bs, S5_N), bmat, cmat, lam_t, dskip, gw, gb)
        y_c = jnp.concatenate([y_c_p, y_c_s], axis=0)

        rw_pad = _pad_cols(router_w[l], LANES)
        rb_pad = jnp.concatenate([router_b[l], jnp.full((LANES - N_EXP,), -1e30, F32)]).reshape(1, LANES)
        x, h2, logits = _merge(y_a, y_b, y_c, z_gate, w_br[l].astype(BF16), w_out[l].astype(BF16), x, gate1,
                               norm2_g[l].reshape(1, d), scale2, shift2, rw_pad, rb_pad,
                               mod_idx_for(tm_merge), tm_merge)

        n_rows_pad = (m * TOP_K // MOE_BM + N_EXP) * MOE_BM
        gate_w, dest, rows_tok, blk_e, n_rows_used = _route(logits[:, :N_EXP], n_rows_pad)
        yb = _moe(h2[rows_tok], blk_e, n_rows_used // MOE_BM, l, moe_w_gu, moe_b_gu, moe_w_down, moe_b_down)
        yg = yb[dest.T.reshape(-1)]
        x = _combine(x, yg, gate_w, gate2, mod_idx_for(tm_mid), tm_mid)

        outs_p.append((new_shift_p, new_wkv_p, new_pool_p, re_p.reshape(bp, S5_G, S5_P),
                       im_p.reshape(bp, S5_G, S5_P)))
        outs_s.append((new_shift_s, new_wkv_s, new_pool_s, re_s.reshape(bs, S5_G, S5_P),
                       im_s.reshape(bs, S5_G, S5_P)))

    y = _final_norm(x, final_norm_g.reshape(1, d), tm_mid)
    y_prompt = y[:mp].reshape(bp, lp, d)
    y_sample = jnp.swapaxes(y[mp:].reshape(ls, bs, d), 0, 1)
    p_shift, p_wkv, p_pool, p_re, p_im = (jnp.stack([o[j] for o in outs_p]) for j in range(5))
    s_shift, s_wkv, s_pool, s_re, s_im = (jnp.stack([o[j] for o in outs_s]) for j in range(5))
    return (y_prompt, y_sample, p_wkv, p_shift, p_pool, p_re, p_im,
            s_wkv, s_shift, s_pool, s_re, s_im)
```

```python
import functools

import jax
import jax.numpy as jnp
from jax import lax
from jax.experimental import pallas as pl
from jax.experimental.pallas import tpu as pltpu

F32, BF16 = jnp.float32, jnp.bfloat16

LANES = 128
SUBLANES = 8
VMEM_LIMIT = 56 * 1024 * 1024

HEAD = 64
HEADS = 16
RW = HEAD * HEADS
R_DECAY, R_AAA, R_GATE = 64, 64, 160
RW_PROJ = 3 * RW + R_DECAY + R_AAA + R_GATE
RW_PAD = 3584
LORA_IN = RW_PAD - 3 * RW
POOL_W = 512
WINDOWS = (2, 4, 8, 16)
POOL_G = POOL_W // len(WINDOWS)
POOL_BUF = max(WINDOWS) - 1
S5_W = 512
S5_G, S5_CH, S5_P = 32, 16, 64
S5_N = S5_G * S5_P
MAIN_W = RW_PAD + POOL_W + S5_W
N_EXP, TOP_K = 32, 4
SW_LIMIT, SW_ALPHA = 7.0, 1.702
NORM_EPS, GN_EPS = 1e-5, 64e-5
MOD_ROWS = 128


def _params(sem):
    return pltpu.CompilerParams(dimension_semantics=sem, vmem_limit_bytes=VMEM_LIMIT)


def _dot(a, b):
    return jnp.dot(a, b, preferred_element_type=F32)


def _dot3(a, b):
    a_hi, b_hi = a.astype(BF16), b.astype(BF16)
    a_lo = (a - a_hi.astype(F32)).astype(BF16)
    b_lo = (b - b_hi.astype(F32)).astype(BF16)
    return _dot(a_hi, b_hi) + (_dot(a_hi, b_lo) + _dot(a_lo, b_hi))


def _segsum(x, ones_bd):
    def split(v):
        hi = v.astype(BF16)
        return hi, (v - hi.astype(F32)).astype(BF16)

    nt = (((1,), (1,)), ((), ()))
    hi, lo = split(x)
    sums = (lax.dot_general(hi, ones_bd, nt, preferred_element_type=F32)
            + lax.dot_general(lo, ones_bd, nt, preferred_element_type=F32))
    hi, lo = split(sums)
    return _dot(hi, ones_bd) + _dot(lo, ones_bd)


def _sublane_transpose8(xs):
    xs = list(xs)
    sub = lax.broadcasted_iota(jnp.int32, xs[0].shape, 0)
    for d in (4, 2, 1):
        keep = (sub & d) == 0
        for k in range(SUBLANES):
            if k & d:
                continue
            lo, hi = xs[k], xs[k + d]
            xs[k] = jnp.where(keep, lo, pltpu.roll(hi, d, axis=0))
            xs[k + d] = jnp.where(keep, pltpu.roll(lo, SUBLANES - d, axis=0), hi)
    return xs


def _tile_rows(t, rows):
    return jnp.broadcast_to(t[None], (rows // MOD_ROWS,) + t.shape).reshape(rows, t.shape[-1])


def _stack_rows_kernel(a_ref, b_ref, o_ref):
    i = pl.program_id(0)

    @pl.when(i < pl.num_programs(0) - 1)
    def _():
        o_ref[...] = a_ref[...]

    @pl.when(i == pl.num_programs(0) - 1)
    def _():
        o_ref[...] = b_ref[...]


def _stack_rows(a, b):
    tm, d = b.shape
    n_a = a.shape[0] // tm
    return pl.pallas_call(
        _stack_rows_kernel,
        grid=(n_a + 1,),
        in_specs=[pl.BlockSpec((tm, d), lambda i: (jnp.minimum(i, n_a - 1), 0)),
                  pl.BlockSpec((tm, d), lambda i: (0, 0))],
        out_specs=pl.BlockSpec((tm, d), lambda i: (i, 0)),
        out_shape=jax.ShapeDtypeStruct((a.shape[0] + tm, d), a.dtype),
        compiler_params=_params(("arbitrary",)),
        name="stack_rows",
    )(a, b)


def _ada_kernel(c_ref, w_ref, b_ref, o_ref):
    c = c_ref[...]
    a = (c * jax.nn.sigmoid(c)).astype(BF16)
    o_ref[0] = _dot(a, w_ref[0].astype(BF16)) + b_ref[0]


def _adaln(c_all, w_ada, b_ada):
    nl, d, n = w_ada.shape
    r = c_all.shape[0]
    tn = 1024
    return pl.pallas_call(
        _ada_kernel,
        grid=(nl, n // tn),
        in_specs=[pl.BlockSpec((r, d), lambda l, j: (0, 0)),
                  pl.BlockSpec((1, d, tn), lambda l, j: (l, 0, j)),
                  pl.BlockSpec((1, 1, tn), lambda l, j: (l, 0, j))],
        out_specs=pl.BlockSpec((1, r, tn), lambda l, j: (l, 0, j)),
        out_shape=jax.ShapeDtypeStruct((nl, r, n), F32),
        compiler_params=_params(("parallel", "parallel")),
        name="adaln",
    )(c_all, w_ada, b_ada.reshape(nl, 1, n))


def _norm_mod(x, g, scale, shift):
    y = x * lax.rsqrt(jnp.mean(x * x, axis=-1, keepdims=True) + NORM_EPS) * g
    rows = x.shape[0]
    return y * (1.0 + _tile_rows(scale, rows)) + _tile_rows(shift, rows)


def _inproj_kernel(x_ref, g_ref, sc_ref, sh_ref, w_ref, o_ref, h_scr):
    @pl.when(pl.program_id(1) == 0)
    def _():
        h_scr[...] = _norm_mod(x_ref[...], g_ref[...], sc_ref[...], sh_ref[...]).astype(BF16)

    o_ref[...] = _dot(h_scr[...], w_ref[0])


def _inproj(x, g, scale_t, shift_t, w, layer, mod_idx, tm):
    m, d = x.shape
    n = w.shape[2]
    tn = 512
    return pl.pallas_call(
        _inproj_kernel,
        grid=(m // tm, n // tn),
        in_specs=[pl.BlockSpec((tm, d), lambda i, j: (i, 0)),
                  pl.BlockSpec((1, d), lambda i, j: (0, 0)),
                  pl.BlockSpec((MOD_ROWS, d), lambda i, j: (mod_idx(i), 0)),
                  pl.BlockSpec((MOD_ROWS, d), lambda i, j: (mod_idx(i), 0)),
                  pl.BlockSpec((1, d, tn), lambda i, j: (layer, 0, j))],
        out_specs=pl.BlockSpec((tm, tn), lambda i, j: (i, j)),
        out_shape=jax.ShapeDtypeStruct((m, n), F32),
        scratch_shapes=[pltpu.VMEM((tm, d), BF16)],
        compiler_params=_params(("parallel", "arbitrary")),
        name="inproj",
    )(x, g, scale_t, shift_t, w)


def _softplus(x):
    return jnp.maximum(x, 0.0) + jnp.log1p(jnp.exp(-jnp.abs(x)))


RW_TT = 128
HALF = LANES // 2


def _rwkv_pre_math(z, zp, mu_ref, w0_ref, a0_ref, kkw_ref, ka_ref, rk_ref, wl_ref, ones_ref):
    zs = z + (zp - z) * mu_ref[...]
    r, k, v = zs[:, :RW], zs[:, RW:2 * RW], zs[:, 2 * RW:3 * RW]
    l0 = zs[:, 3 * RW:3 * RW + LANES]
    lane = lax.broadcasted_iota(jnp.int32, l0.shape, 1)
    l0 = jnp.where(lane < R_DECAY, jnp.tanh(l0), l0)
    l1 = jax.nn.sigmoid(zs[:, 3 * RW + LANES:])
    lin = jnp.concatenate([l0, l1], axis=1).astype(BF16)
    lo = _dot(lin, wl_ref[...])
    w_log = -_softplus(-(w0_ref[...] + lo[:, :RW])) - 0.5
    a = jax.nn.sigmoid(a0_ref[...] + lo[:, RW:2 * RW])
    ones_bd = ones_ref[...]
    kk = k * kkw_ref[...]
    kkn = kk / jnp.maximum(jnp.sqrt(_segsum(kk * kk, ones_bd)), 1e-12)
    kh = k * (1.0 + (a - 1.0) * ka_ref[...])
    bonus = _segsum(r * kh * rk_ref[...], ones_bd) * v
    return r, jnp.exp(-jnp.exp(w_log)), kh, v, kkn, kkn * a, lo[:, 2 * RW:], bonus


def _rwkv_pre_prompt_kernel(z_ref, prev_ref, mu_ref, w0_ref, a0_ref, kkw_ref, ka_ref, rk_ref, wl_ref,
                            ones_ref, r_o, w_o, k_o, v_o, kk_o, b_o, g_o, bon_o, zt_scr):
    bi = pl.program_id(1)
    z = z_ref[...]
    row = lax.broadcasted_iota(jnp.int32, z.shape, 0)
    prev_row = jnp.where(pl.program_id(0) == 0, 0.0, prev_ref[SUBLANES - 1:SUBLANES, :])
    zp = jnp.where(row == 0, prev_row, pltpu.roll(z, 1, axis=0))
    r, w, kh, v, kkn, b, g, bonus = _rwkv_pre_math(z, zp, mu_ref, w0_ref, a0_ref, kkw_ref, ka_ref,
                                                   rk_ref, wl_ref, ones_ref)
    g_o[...] = g
    bon_o[...] = bonus
    for n, x in enumerate((r, w, kh, v, kkn, b)):
        for p in range(RW // LANES):
            row0 = pl.multiple_of((bi * (RW // LANES) + p) * LANES, LANES)
            zt_scr[n, pl.ds(row0, LANES), :] = x[:, p * LANES:(p + 1) * LANES].T

    @pl.when(bi == pl.num_programs(1) - 1)
    def _():
        n_bh = zt_scr.shape[1] // HEAD

        def channel_rows(n, c0):
            cols = [[] for _ in range(SUBLANES)]
            for j in range(n_bh // SUBLANES):
                blk = [zt_scr[n, pl.ds(pl.multiple_of((j * SUBLANES + k) * HEAD + c0, SUBLANES), SUBLANES), :]
                       for k in range(SUBLANES)]
                for m, piece in enumerate(_sublane_transpose8(blk)):
                    cols[m].append(piece)
            return [jnp.concatenate(pieces, axis=0) for pieces in cols]

        def store_rows(o_ref, mats, rows_per_t, r0):
            for jt in range(RW_TT // SUBLANES):
                blk = [mt[jt * SUBLANES:(jt + 1) * SUBLANES] for mt in mats]
                for s, piece in enumerate(_sublane_transpose8(blk)):
                    row = pl.multiple_of((jt * SUBLANES + s) * rows_per_t + r0, SUBLANES)
                    o_ref[pl.ds(row, SUBLANES), :] = piece

        for n, o_ref in enumerate((r_o, w_o, k_o, v_o, kk_o, b_o)):
            if o_ref is v_o:
                def group(gi, c, n=n, o_ref=o_ref):
                    c0 = pl.multiple_of(gi * 2 * SUBLANES, 2 * SUBLANES)
                    a = channel_rows(n, c0) + channel_rows(n, c0 + SUBLANES)
                    mats = [jnp.concatenate([a[2 * q], a[2 * q + 1]], axis=0).T for q in range(SUBLANES)]
                    store_rows(o_ref, mats, HEAD // 2, pl.multiple_of(gi * SUBLANES, SUBLANES))
                    return c

                lax.fori_loop(0, HEAD // (2 * SUBLANES), group, 0)
            else:
                def group(gi, c, n=n, o_ref=o_ref):
                    c0 = pl.multiple_of(gi * SUBLANES, SUBLANES)
                    mats = [jnp.concatenate([a, a], axis=0).T for a in channel_rows(n, c0)]
                    store_rows(o_ref, mats, HEAD, c0)
                    return c

                lax.fori_loop(0, HEAD // SUBLANES, group, 0)


def _rwkv_pre_prompt(z_main, n_b, n_t, mu, w0, a0, kkw, ka, rk, wl, ones_bd):
    nt = n_t // RW_TT
    vec = lambda w: pl.BlockSpec((1, w), lambda i, b: (0, 0))
    keyed = pl.BlockSpec((RW_TT * HEAD, LANES), lambda i, b: (i, 0), pipeline_mode=pl.Buffered(1))
    paired = pl.BlockSpec((RW_TT * (HEAD // 2), LANES), lambda i, b: (i, 0), pipeline_mode=pl.Buffered(1))
    keyed_shape = jax.ShapeDtypeStruct((n_t * HEAD, LANES), F32)
    paired_shape = jax.ShapeDtypeStruct((n_t * (HEAD // 2), LANES), F32)
    nat = pl.BlockSpec((RW_TT, RW), lambda i, b: (b * nt + i, 0))
    return pl.pallas_call(
        _rwkv_pre_prompt_kernel,
        grid=(nt, n_b),
        in_specs=[pl.BlockSpec((RW_TT, RW_PAD), lambda i, b: (b * nt + i, 0)),
                  pl.BlockSpec((SUBLANES, RW_PAD),
                               lambda i, b: (jnp.maximum((b * nt + i) * (RW_TT // SUBLANES) - 1, 0), 0)),
                  vec(RW_PAD), vec(RW), vec(RW), vec(RW), vec(RW), vec(RW),
                  pl.BlockSpec((LORA_IN, 3 * RW), lambda i, b: (0, 0)),
                  pl.BlockSpec((LANES, RW), lambda i, b: (0, 0))],
        out_specs=[keyed] * 3 + [paired] + [keyed] * 2 + [nat] * 2,
        out_shape=[keyed_shape] * 3 + [paired_shape] + [keyed_shape] * 2
        + [jax.ShapeDtypeStruct((n_b * n_t, RW), F32)] * 2,
        scratch_shapes=[pltpu.VMEM((6, n_b * RW, RW_TT), F32)],
        compiler_params=_params(("arbitrary", "arbitrary")),
        name="rwkv_pre_prompt",
    )(z_main, z_main, mu, w0, a0, kkw, ka, rk, wl, ones_bd)


def _rwkv_pre_sample_kernel(z_ref, zp_ref, st_ref, mu_ref, w0_ref, a0_ref, kkw_ref, ka_ref, rk_ref, wl_ref,
                            ones_ref, r_o, w_o, k_o, v_o, kk_o, b_o, g_o, bon_o):
    zp = jnp.where(pl.program_id(0) == 0, st_ref[...], zp_ref[...])
    r, w, kh, v, kkn, b, g, bonus = _rwkv_pre_math(z_ref[...], zp, mu_ref, w0_ref, a0_ref, kkw_ref, ka_ref,
                                                   rk_ref, wl_ref, ones_ref)
    g_o[...] = g
    bon_o[...] = bonus
    for x, o_ref in zip((r, w, kh, v, kkn, b), (r_o, w_o, k_o, v_o, kk_o, b_o)):
        for p in range(RW // LANES):
            xt = x[:, p * LANES:(p + 1) * LANES].T
            o_ref[0, :, 2 * p * LANES:(2 * p + 1) * LANES] = xt[:HEAD]
            o_ref[0, :, (2 * p + 1) * LANES:(2 * p + 2) * LANES] = xt[HEAD:]


def _rwkv_pre_sample(z_main, state, row0, n_b, n_t, mu, w0, a0, kkw, ka, rk, wl, ones_bd):
    blk0 = row0 // n_b
    vec = lambda w: pl.BlockSpec((1, w), lambda t: (0, 0))
    keyed = pl.BlockSpec((1, HEAD, HEADS * n_b), lambda t: (t, 0, 0))
    nat = pl.BlockSpec((n_b, RW), lambda t: (t, 0))
    return pl.pallas_call(
        _rwkv_pre_sample_kernel,
        grid=(n_t,),
        in_specs=[pl.BlockSpec((n_b, RW_PAD), lambda t: (blk0 + t, 0)),
                  pl.BlockSpec((n_b, RW_PAD), lambda t: (blk0 + jnp.maximum(t - 1, 0), 0)),
                  pl.BlockSpec((n_b, RW_PAD), lambda t: (0, 0)),
                  vec(RW_PAD), vec(RW), vec(RW), vec(RW), vec(RW), vec(RW),
                  pl.BlockSpec((LORA_IN, 3 * RW), lambda t: (0, 0)),
                  pl.BlockSpec((LANES, RW), lambda t: (0, 0))],
        out_specs=[keyed] * 6 + [nat] * 2,
        out_shape=[jax.ShapeDtypeStruct((n_t, HEAD, HEADS * n_b), F32)] * 6
        + [jax.ShapeDtypeStruct((n_t * n_b, RW), F32)] * 2,
        compiler_params=_params(("arbitrary",)),
        name="rwkv_pre_sample",
    )(z_main, z_main, state, mu, w0, a0, kkw, ka, rk, wl, ones_bd)


def _rwkv_scan_kernel(nat_state, r_ref, w_ref, k_ref, kk_ref, b_ref, v_ref, s0_ref, y_ref, sT_ref, s_scr):
    tc = pl.program_id(1)
    n_t = r_ref.shape[0]
    n_i = s_scr.shape[0]

    @pl.when(tc == 0)
    def _():
        if nat_state:
            s_scr[...] = s0_ref[...].T.reshape(s_scr.shape)
        else:
            s_scr[...] = s0_ref[...]

    def step(t, carry):
        r_t, w_t, k_t, kk_t, b_t = r_ref[t], w_ref[t], k_ref[t], kk_ref[t], b_ref[t]

        def rows8(ib, c2):
            i0 = pl.multiple_of(ib * SUBLANES, SUBLANES)
            v8 = v_ref[t, pl.ds(i0, SUBLANES), :]
            ys = []
            for ii in range(SUBLANES):
                s_old = s_scr[i0 + ii]
                sa = jnp.sum(s_old * kk_t, axis=0, keepdims=True)
                s_new = s_old * w_t - sa * b_t + v8[ii:ii + 1] * k_t
                ys.append(jnp.sum(s_new * r_t, axis=0, keepdims=True))
                s_scr[i0 + ii] = s_new
            y_ref[t, pl.ds(i0, SUBLANES), :] = jnp.concatenate(ys, axis=0)
            return c2

        return lax.fori_loop(0, n_i // SUBLANES, rows8, carry)

    lax.fori_loop(0, n_t, step, 0)

    @pl.when(tc == pl.num_programs(1) - 1)
    def _():
        if nat_state:
            sT_ref[...] = s_scr[...].reshape(n_i * s_scr.shape[1], LANES).T
        else:
            sT_ref[...] = s_scr[...]


def _rwkv_scan_prompt(r, w, k, kk, b, v, s0, tchunk):
    n_t = r.shape[0]
    n_i = s0.shape[0]
    op = pl.BlockSpec((tchunk, HEAD, LANES), lambda l, t: (t, 0, 0))
    vy = pl.BlockSpec((tchunk, n_i, LANES), lambda l, t: (t, 0, 0))
    st = pl.BlockSpec((n_i, HEAD, LANES), lambda l, t: (0, 0, 0))
    return pl.pallas_call(
        functools.partial(_rwkv_scan_kernel, False),
        grid=(1, n_t // tchunk),
        in_specs=[op] * 5 + [vy, st],
        out_specs=[vy, st],
        out_shape=[jax.ShapeDtypeStruct((n_t, n_i, LANES), F32),
                   jax.ShapeDtypeStruct((n_i, HEAD, LANES), F32)],
        scratch_shapes=[pltpu.VMEM((n_i, HEAD, LANES), F32)],
        compiler_params=_params(("arbitrary", "arbitrary")),
        name="rwkv_scan_prompt",
    )(r, w, k, kk, b, v, s0)


def _rwkv_scan_sample(r, w, k, kk, b, v, s0, layer):
    n_t, _, n_l = r.shape
    n_b = n_l // HEADS
    op = pl.BlockSpec((n_t, HEAD, LANES), lambda h, t: (0, 0, h))
    st = pl.BlockSpec((n_b, HEAD * HEAD), lambda h, t: (layer, h))
    st_out = pl.BlockSpec((n_b, HEAD * HEAD), lambda h, t: (0, h))
    return pl.pallas_call(
        functools.partial(_rwkv_scan_kernel, True),
        grid=(HEADS, 1),
        in_specs=[op] * 6 + [st],
        out_specs=[op, st_out],
        out_shape=[jax.ShapeDtypeStruct((n_t, HEAD, n_l), F32),
                   jax.ShapeDtypeStruct((n_b, HEADS * HEAD * HEAD), F32)],
        scratch_shapes=[pltpu.VMEM((HEAD, HEAD, LANES), F32)],
        compiler_params=_params(("parallel", "arbitrary")),
        name="rwkv_scan_sample",
    )(r, w, k, kk, b, v, s0)


def _rwkv_post_math(y, bon_ref, g_ref, lg_ref, lb_ref, ones_ref):
    ones_bd = ones_ref[...]
    yc = y - _segsum(y, ones_bd) * (1.0 / HEAD)
    var = _segsum(yc * yc, ones_bd) * (1.0 / HEAD)
    yn = yc * lax.rsqrt(var + GN_EPS) * lg_ref[...] + lb_ref[...]
    return ((yn + bon_ref[...]) * g_ref[...]).astype(BF16)


def _rwkv_post_prompt_kernel(y_ref, bon_ref, g_ref, lg_ref, lb_ref, ones_ref, o_ref, zt_scr):
    bi = pl.program_id(1)

    @pl.when(bi == 0)
    def _():
        n_bh = zt_scr.shape[0] // HEAD

        def group(gi, c):
            q0 = pl.multiple_of(gi * SUBLANES, SUBLANES)
            mats = [[] for _ in range(SUBLANES)]
            for jt in range(RW_TT // SUBLANES):
                blk = [y_ref[pl.ds(pl.multiple_of((jt * SUBLANES + s) * (HEAD // 2) + q0, SUBLANES), SUBLANES), :]
                       for s in range(SUBLANES)]
                for m, piece in enumerate(_sublane_transpose8(blk)):
                    mats[m].append(piece)
            vals = []
            for pieces in mats:
                mt = jnp.concatenate(pieces, axis=0).T
                vals += [mt[:HALF], mt[HALF:]]
            for half in range(2):
                for j in range(n_bh // SUBLANES):
                    blk = [v[j * SUBLANES:(j + 1) * SUBLANES] for v in vals[half * SUBLANES:(half + 1) * SUBLANES]]
                    for k, piece in enumerate(_sublane_transpose8(blk)):
                        row = pl.multiple_of((j * SUBLANES + k) * HEAD + 2 * q0 + half * SUBLANES, SUBLANES)
                        zt_scr[pl.ds(row, SUBLANES), :] = piece
            return c

        lax.fori_loop(0, HEAD // (2 * SUBLANES), group, 0)

    pieces = []
    for p in range(RW // LANES):
        row0 = pl.multiple_of((bi * (RW // LANES) + p) * LANES, LANES)
        pieces.append(zt_scr[pl.ds(row0, LANES), :].T)
    y = jnp.concatenate(pieces, axis=1)
    o_ref[...] = _rwkv_post_math(y, bon_ref, g_ref, lg_ref, lb_ref, ones_ref)


def _rwkv_post_prompt(y, bonus, g, n_b, n_t, lnx_g, lnx_b, ones_bd):
    nt = n_t // RW_TT
    nat = pl.BlockSpec((RW_TT, RW), lambda i, b: (b * nt + i, 0))
    vec = pl.BlockSpec((1, RW), lambda i, b: (0, 0))
    return pl.pallas_call(
        _rwkv_post_prompt_kernel,
        grid=(nt, n_b),
        in_specs=[pl.BlockSpec((RW_TT * (HEAD // 2), LANES), lambda i, b: (i, 0)), nat, nat, vec, vec,
                  pl.BlockSpec((LANES, RW), lambda i, b: (0, 0))],
        out_specs=nat,
        out_shape=jax.ShapeDtypeStruct((n_b * n_t, RW), BF16),
        scratch_shapes=[pltpu.VMEM((n_b * RW, RW_TT), F32)],
        compiler_params=_params(("arbitrary", "arbitrary")),
        name="rwkv_post_prompt",
    )(y, bonus, g, lnx_g, lnx_b, ones_bd)


def _rwkv_post_sample_kernel(y_ref, bon_ref, g_ref, lg_ref, lb_ref, ones_ref, o_ref):
    pieces = []
    for p in range(RW // LANES):
        m = jnp.concatenate([y_ref[0, :, 2 * p * LANES:(2 * p + 1) * LANES],
                             y_ref[0, :, (2 * p + 1) * LANES:(2 * p + 2) * LANES]], axis=0)
        pieces.append(m.T)
    y = jnp.concatenate(pieces, axis=1)
    o_ref[...] = _rwkv_post_math(y, bon_ref, g_ref, lg_ref, lb_ref, ones_ref)


def _rwkv_post_sample(y, bonus, g, n_b, n_t, lnx_g, lnx_b, ones_bd):
    nat = pl.BlockSpec((n_b, RW), lambda t: (t, 0))
    vec = pl.BlockSpec((1, RW), lambda t: (0, 0))
    return pl.pallas_call(
        _rwkv_post_sample_kernel,
        grid=(n_t,),
        in_specs=[pl.BlockSpec((1, HEAD, HEADS * n_b), lambda t: (t, 0, 0)), nat, nat, vec, vec,
                  pl.BlockSpec((LANES, RW), lambda t: (0, 0))],
        out_specs=nat,
        out_shape=jax.ShapeDtypeStruct((n_t * n_b, RW), BF16),
        compiler_params=_params(("parallel",)),
        name="rwkv_post_sample",
    )(y, bonus, g, lnx_g, lnx_b, ones_bd)


def _pool_prompt_kernel(start_pos, z_ref, pw_ref, ps_ref, y_ref):
    n = z_ref.shape[0]
    row = lax.broadcasted_iota(jnp.int32, (n, POOL_G), 0)
    for gi, win in enumerate(WINDOWS):
        sl = slice(gi * POOL_G, (gi + 1) * POOL_G)
        x = z_ref[:, sl]
        s = x
        k = 1
        while k < win:
            s = s + jnp.where(row >= k, pltpu.roll(s, k, axis=0), 0.0)
            k *= 2
        cnt = jnp.minimum(win, row + (start_pos + 1)).astype(F32)
        d = s / cnt - x
        y_ref[:, sl] = (_dot(d.astype(BF16), pw_ref[gi]) * ps_ref[:, sl]).astype(BF16)


def _pool_prompt(z_main, n_b, n_t, pw, ps):
    cb = RW_PAD // POOL_W
    return pl.pallas_call(
        functools.partial(_pool_prompt_kernel, 0),
        grid=(n_b,),
        in_specs=[pl.BlockSpec((n_t, POOL_W), lambda b: (b, cb)),
                  pl.BlockSpec((len(WINDOWS), POOL_G, POOL_G), lambda b: (0, 0, 0)),
                  pl.BlockSpec((1, POOL_W), lambda b: (0, 0))],
        out_specs=pl.BlockSpec((n_t, POOL_W), lambda b: (b, 0)),
        out_shape=jax.ShapeDtypeStruct((n_b * n_t, POOL_W), BF16),
        compiler_params=_params(("parallel",)),
        name="pool_prompt",
    )(z_main, pw, ps)


def _pool_sample_kernel(start_pos, n_t, full_ref, pw_ref, ps_ref, y_ref):
    n_b = full_ref.shape[1]
    for gi, win in enumerate(WINDOWS):
        sl = slice(gi * POOL_G, (gi + 1) * POOL_G)
        f = full_ref[:, :, sl]
        s = f
        k = 1
        while k < win:
            s = s[k:] + s[:-k]
            k *= 2
        s = s[s.shape[0] - n_t:]
        x = f[POOL_BUF:]
        pos = start_pos + lax.broadcasted_iota(jnp.int32, s.shape, 0)
        cnt = jnp.minimum(win, pos + 1).astype(F32)
        d = (s / cnt - x).reshape(n_t * n_b, POOL_G)
        y_ref[:, sl] = (_dot(d.astype(BF16), pw_ref[gi]) * ps_ref[:, sl]).astype(BF16)


def _pool_sample(full, n_t, start_pos, pw, ps):
    n_f, n_b, _ = full.shape
    return pl.pallas_call(
        functools.partial(_pool_sample_kernel, start_pos, n_t),
        grid=(1,),
        in_specs=[pl.BlockSpec((n_f, n_b, POOL_W), lambda i: (0, 0, 0)),
                  pl.BlockSpec((len(WINDOWS), POOL_G, POOL_G), lambda i: (0, 0, 0)),
                  pl.BlockSpec((1, POOL_W), lambda i: (0, 0))],
        out_specs=pl.BlockSpec((n_t * n_b, POOL_W), lambda i: (0, 0)),
        out_shape=jax.ShapeDtypeStruct((n_t * n_b, POOL_W), BF16),
        compiler_params=_params(("arbitrary",)),
        name="pool_sample",
    )(full, pw, ps)


def _s5_tail(x_all, u, cmat_ref, d_ref, gw_ref, gb_ref):
    y = _dot(x_all.astype(BF16), cmat_ref[...]) + d_ref[...] * u
    y = jax.nn.gelu(y)
    return (y * jax.nn.sigmoid(_dot(y.astype(BF16), gw_ref[...]) + gb_ref[...])).astype(BF16)


S5_LC = 512


def _s5_prompt_kernel(u_ref, bmat_ref, cmat_ref, lam_ref, pw_ref, d_ref, gw_ref, gb_ref,
                      y_ref, xre_o, xim_o, x_scr, c_scr):
    tc = pl.program_id(1)

    @pl.when(tc == 0)
    def _():
        c_scr[...] = jnp.zeros_like(c_scr)

    u = u_ref[...]
    n = u.shape[0]
    x_scr[...] = _dot(u.astype(BF16), bmat_ref[...])
    row = lax.broadcasted_iota(jnp.int32, (SUBLANES, S5_LC), 0)
    for lc in range(S5_N // S5_LC):
        re_sl = pl.ds(lc * S5_LC, S5_LC)
        im_sl = pl.ds(S5_N + lc * S5_LC, S5_LC)
        lam = [(lam_ref[2 * j:2 * j + 1, re_sl], lam_ref[2 * j + 1:2 * j + 2, re_sl]) for j in range(3)]
        p_re, p_im = pw_ref[:, re_sl], pw_ref[:, im_sl]

        def blk(rb, carry, re_sl=re_sl, im_sl=im_sl, lam=lam, p_re=p_re, p_im=p_im):
            c_re, c_im = carry
            rows = pl.ds(pl.multiple_of(rb * SUBLANES, SUBLANES), SUBLANES)
            xr, xi = x_scr[rows, re_sl], x_scr[rows, im_sl]
            for j, (l_re, l_im) in enumerate(lam):
                kshift = 1 << j
                sr = jnp.where(row >= kshift, pltpu.roll(xr, kshift, axis=0), 0.0)
                si = jnp.where(row >= kshift, pltpu.roll(xi, kshift, axis=0), 0.0)
                xr, xi = xr + (l_re * sr - l_im * si), xi + (l_re * si + l_im * sr)
            xr, xi = xr + (p_re * c_re - p_im * c_im), xi + (p_re * c_im + p_im * c_re)
            x_scr[rows, re_sl] = xr
            x_scr[rows, im_sl] = xi
            return xr[SUBLANES - 1:], xi[SUBLANES - 1:]

        c_re, c_im = lax.fori_loop(0, n // SUBLANES, blk, (c_scr[0:1, re_sl], c_scr[0:1, im_sl]))
        c_scr[0:1, re_sl] = c_re
        c_scr[0:1, im_sl] = c_im

    y_ref[...] = _s5_tail(x_scr[...], u, cmat_ref, d_ref, gw_ref, gb_ref)

    @pl.when(tc == pl.num_programs(1) - 1)
    def _():
        xre_o[0] = c_scr[0:1, :S5_N]
        xim_o[0] = c_scr[0:1, S5_N:]


def _s5_prompt(z_main, n_b, n_t, tt, bmat, cmat, lam_t, pw_t, dskip, gw, gb):
    cb = (RW_PAD + POOL_W) // S5_W
    n_tc = n_t // tt
    const = lambda shape: pl.BlockSpec(shape, lambda b, t: (0,) * len(shape))
    return pl.pallas_call(
        _s5_prompt_kernel,
        grid=(n_b, n_tc),
        in_specs=[pl.BlockSpec((tt, S5_W), lambda b, t: (b * n_tc + t, cb)),
                  const((S5_W, 2 * S5_N)), const((2 * S5_N, S5_W)),
                  const((SUBLANES, S5_N)), const((SUBLANES, 2 * S5_N)),
                  const((1, S5_W)), const((S5_W, S5_W)), const((1, S5_W))],
        out_specs=[pl.BlockSpec((tt, S5_W), lambda b, t: (b * n_tc + t, 0)),
                   pl.BlockSpec((1, 1, S5_N), lambda b, t: (b, 0, 0)),
                   pl.BlockSpec((1, 1, S5_N), lambda b, t: (b, 0, 0))],
        out_shape=[jax.ShapeDtypeStruct((n_b * n_t, S5_W), BF16),
                   jax.ShapeDtypeStruct((n_b, 1, S5_N), F32),
                   jax.ShapeDtypeStruct((n_b, 1, S5_N), F32)],
        scratch_shapes=[pltpu.VMEM((tt, 2 * S5_N), F32), pltpu.VMEM((SUBLANES, 2 * S5_N), F32)],
        compiler_params=_params(("parallel", "arbitrary")),
        name="s5_prompt",
    )(z_main, bmat, cmat, lam_t, pw_t, dskip, gw, gb)


def _s5_sample_kernel(n_t, u_ref, x0re_ref, x0im_ref, bmat_ref, cmat_ref, lam_ref, d_ref, gw_ref,
                      gb_ref, y_ref, xre_o, xim_o, x_scr):
    u = u_ref[...]
    n_b = u.shape[0] // n_t
    x_scr[...] = _dot(u.astype(BF16), bmat_ref[...])
    for lc in range(S5_N // S5_LC):
        re_sl = pl.ds(lc * S5_LC, S5_LC)
        im_sl = pl.ds(S5_N + lc * S5_LC, S5_LC)
        l_re, l_im = lam_ref[0:1, re_sl], lam_ref[1:2, re_sl]
        xr, xi = x0re_ref[:, re_sl], x0im_ref[:, re_sl]
        for t in range(n_t):
            rows = pl.ds(t * n_b, n_b)
            xr, xi = (l_re * xr - l_im * xi + x_scr[rows, re_sl],
                      l_re * xi + l_im * xr + x_scr[rows, im_sl])
            x_scr[rows, re_sl] = xr
            x_scr[rows, im_sl] = xi
        xre_o[:, re_sl] = xr
        xim_o[:, re_sl] = xi
    y_ref[...] = _s5_tail(x_scr[...], u, cmat_ref, d_ref, gw_ref, gb_ref)


def _s5_sample(z_main, row0, n_b, n_t, x0re, x0im, bmat, cmat, lam_t, dskip, gw, gb):
    rows = n_b * n_t
    cb = (RW_PAD + POOL_W) // S5_W
    const = lambda shape: pl.BlockSpec(shape, lambda i: (0,) * len(shape))
    return pl.pallas_call(
        functools.partial(_s5_sample_kernel, n_t),
        grid=(1,),
        in_specs=[pl.BlockSpec((rows, S5_W), lambda i: (row0 // rows, cb)),
                  const((n_b, S5_N)), const((n_b, S5_N)),
                  const((S5_W, 2 * S5_N)), const((2 * S5_N, S5_W)), const((SUBLANES, S5_N)),
                  const((1, S5_W)), const((S5_W, S5_W)), const((1, S5_W))],
        out_specs=[const((rows, S5_W)), const((n_b, S5_N)), const((n_b, S5_N))],
        out_shape=[jax.ShapeDtypeStruct((rows, S5_W), BF16),
                   jax.ShapeDtypeStruct((n_b, S5_N), F32),
                   jax.ShapeDtypeStruct((n_b, S5_N), F32)],
        scratch_shapes=[pltpu.VMEM((rows, 2 * S5_N), F32)],
        compiler_params=_params(("arbitrary",)),
        name="s5_sample",
    )(z_main, x0re, x0im, bmat, cmat, lam_t, dskip, gw, gb)


def _branch_merge_kernel(ya_ref, yb_ref, yc_ref, zg_ref, wbr_ref, o_ref):
    d = o_ref.shape[1]
    m = jax.nn.sigmoid(zg_ref[:, :d]) * _dot(ya_ref[...], wbr_ref[:RW, :])
    m = m + jax.nn.sigmoid(zg_ref[:, d:2 * d]) * _dot(yb_ref[...], wbr_ref[RW:RW + POOL_W, :])
    m = m + jax.nn.sigmoid(zg_ref[:, 2 * d:]) * _dot(yc_ref[...], wbr_ref[RW + POOL_W:, :])
    o_ref[...] = m.astype(BF16)


def _outproj_kernel(m_ref, wout_ref, x_ref, g1_ref, n2_ref, sc2_ref, sh2_ref, rw_ref, rb_ref,
                    xo_ref, h2_ref, lg_ref):
    rows = x_ref.shape[0]
    xn = x_ref[...] + _tile_rows(g1_ref[...], rows) * _dot(m_ref[...], wout_ref[...])
    xo_ref[...] = xn
    h2 = _norm_mod(xn, n2_ref[...], sc2_ref[...], sh2_ref[...])
    h2_ref[...] = h2
    lg_ref[...] = _dot3(h2, rw_ref[...]) + rb_ref[...]


def _merge(ya, yb, yc, zg, wbr, wout, x, g1_t, n2g, sc2_t, sh2_t, rw, rb, mod_idx, tm):
    m, d = x.shape
    row = lambda w: pl.BlockSpec((tm, w), lambda i: (i, 0))
    const = lambda shape: pl.BlockSpec(shape, lambda i: (0,) * len(shape), pipeline_mode=pl.Buffered(1))
    mod = pl.BlockSpec((MOD_ROWS, d), lambda i: (mod_idx(i), 0))
    merged = pl.pallas_call(
        _branch_merge_kernel,
        grid=(m // tm,),
        in_specs=[row(RW), row(POOL_W), row(S5_W), row(3 * d), const(wbr.shape)],
        out_specs=row(d),
        out_shape=jax.ShapeDtypeStruct((m, d), BF16),
        compiler_params=_params(("parallel",)),
        name="branch_merge",
    )(ya, yb, yc, zg, wbr)
    return pl.pallas_call(
        _outproj_kernel,
        grid=(m // tm,),
        in_specs=[row(d), const(wout.shape), row(d), mod, const((1, d)), mod, mod, const(rw.shape),
                  const((1, LANES))],
        out_specs=[row(d), row(d), row(LANES)],
        out_shape=[jax.ShapeDtypeStruct((m, d), F32), jax.ShapeDtypeStruct((m, d), F32),
                   jax.ShapeDtypeStruct((m, LANES), F32)],
        compiler_params=_params(("parallel",)),
        name="outproj",
    )(merged, wout, x, g1_t, n2g, sc2_t, sh2_t, rw, rb)


MOE_BM = 512
MOE_TF = 512


def _moe_kernel(be_ref, nb_ref, x_ref, wg_ref, wu_ref, bg_ref, bu_ref, wd_ref, bd_ref, o_ref):
    s, f = pl.program_id(0), pl.program_id(1)
    used = s < nb_ref[0]

    @pl.when(jnp.logical_and(jnp.logical_not(used), f == 0))
    def _():
        o_ref[...] = jnp.zeros_like(o_ref)

    @pl.when(used)
    def _():
        x = x_ref[...].astype(BF16)
        g = _dot(x, wg_ref[0].astype(BF16)) + bg_ref[0]
        u = _dot(x, wu_ref[0].astype(BF16)) + bu_ref[0]
        g = jnp.minimum(g, SW_LIMIT)
        u = jnp.clip(u, -SW_LIMIT, SW_LIMIT)
        act = (u + 1.0) * (g * jax.nn.sigmoid(SW_ALPHA * g))
        part = _dot(act.astype(BF16), wd_ref[0].astype(BF16))

        @pl.when(f == 0)
        def _():
            o_ref[...] = part + bd_ref[0]

        @pl.when(f > 0)
        def _():
            o_ref[...] += part


def _moe(xs, blk_e, n_used, layer, w_gu, b_gu, w_down, b_down):
    rows, d = xs.shape
    n_l, n_e, _, two_ff = w_gu.shape
    dff = two_ff // 2
    nf = dff // MOE_TF
    n_blk = rows // MOE_BM
    e0 = layer * n_e

    def f_eff(s, f, nb):
        return jnp.where(s < nb[0], f, nf - 1)

    grid_spec = pltpu.PrefetchScalarGridSpec(
        num_scalar_prefetch=2,
        grid=(n_blk, nf),
        in_specs=[
            pl.BlockSpec((MOE_BM, d), lambda s, f, be, nb: (jnp.minimum(s, nb[0] - 1), 0)),
            pl.BlockSpec((1, d, MOE_TF), lambda s, f, be, nb: (e0 + be[s], 0, f_eff(s, f, nb))),
            pl.BlockSpec((1, d, MOE_TF), lambda s, f, be, nb: (e0 + be[s], 0, nf + f_eff(s, f, nb))),
            pl.BlockSpec((1, 1, MOE_TF), lambda s, f, be, nb: (e0 + be[s], 0, f_eff(s, f, nb))),
            pl.BlockSpec((1, 1, MOE_TF), lambda s, f, be, nb: (e0 + be[s], 0, nf + f_eff(s, f, nb))),
            pl.BlockSpec((1, MOE_TF, d), lambda s, f, be, nb: (e0 + be[s], f_eff(s, f, nb), 0)),
            pl.BlockSpec((1, 1, d), lambda s, f, be, nb: (e0 + be[s], 0, 0)),
        ],
        out_specs=pl.BlockSpec((MOE_BM, d), lambda s, f, be, nb: (s, 0)),
    )
    w_gu = w_gu.reshape(n_l * n_e, d, two_ff)
    b_gu = b_gu.reshape(n_l * n_e, 1, two_ff)
    return pl.pallas_call(
        _moe_kernel,
        grid_spec=grid_spec,
        out_shape=jax.ShapeDtypeStruct((rows, d), F32),
        compiler_params=_params(("arbitrary", "arbitrary")),
        name="moe",
    )(blk_e, n_used, xs, w_gu, w_gu, b_gu, b_gu, w_down.reshape(n_l * n_e, dff, d),
      b_down.reshape(n_l * n_e, 1, d))


def _route(logits, n_rows_pad):
    n_tok = logits.shape[0]
    top_v, top_e = lax.top_k(logits, TOP_K)
    gate = jax.nn.softmax(top_v, axis=-1)
    flat_e = top_e.reshape(-1)
    onehot = (flat_e[:, None] == jnp.arange(N_EXP, dtype=jnp.int32)[None, :]).astype(jnp.int32)
    csum = jnp.cumsum(onehot, axis=0)
    counts = csum[-1]
    rank = jnp.sum((csum - 1) * onehot, axis=1)
    padded = (counts + MOE_BM - 1) // MOE_BM * MOE_BM
    pad_end = jnp.cumsum(padded)
    pad_start = pad_end - padded
    dest = (pad_start[flat_e] + rank).astype(jnp.int32)
    flat_tok = jnp.arange(n_tok * TOP_K, dtype=jnp.int32) // TOP_K
    rows_tok = jnp.zeros((n_rows_pad,), jnp.int32).at[dest].set(flat_tok)
    n_rows_used = pad_end[-1].astype(jnp.int32)
    starts = jnp.arange(n_rows_pad // MOE_BM, dtype=jnp.int32) * MOE_BM
    starts = jnp.minimum(starts, n_rows_used - MOE_BM)
    blk_e = jnp.minimum(jnp.searchsorted(pad_end, starts, side='right'), N_EXP - 1).astype(jnp.int32)
    return gate, dest.reshape(n_tok, TOP_K), rows_tok, blk_e, n_rows_used.reshape(1)


def _combine_kernel(x_ref, y0_ref, y1_ref, y2_ref, y3_ref, gw_ref, g2_ref, o_ref):
    rows = x_ref.shape[0]
    gw = gw_ref[...]
    acc = gw[:, 0:1] * y0_ref[...]
    for k, y_ref in enumerate((y1_ref, y2_ref, y3_ref), start=1):
        acc = acc + gw[:, k:k + 1] * y_ref[...]
    o_ref[...] = x_ref[...] + _tile_rows(g2_ref[...], rows) * acc


def _combine(x, yg, gw, g2_t, mod_idx, tm):
    m, d = x.shape
    n_i = m // tm
    slab = lambda k: pl.BlockSpec((tm, d), lambda i: (k * n_i + i, 0))
    return pl.pallas_call(
        _combine_kernel,
        grid=(n_i,),
        in_specs=[pl.BlockSpec((tm, d), lambda i: (i, 0)), slab(0), slab(1), slab(2), slab(3),
                  pl.BlockSpec((tm, TOP_K), lambda i: (i, 0)),
                  pl.BlockSpec((MOD_ROWS, d), lambda i: (mod_idx(i), 0))],
        out_specs=pl.BlockSpec((tm, d), lambda i: (i, 0)),
        out_shape=jax.ShapeDtypeStruct((m, d), F32),
        compiler_params=_params(("parallel",)),
        name="combine",
    )(x, yg, yg, yg, yg, gw, g2_t)


def _final_norm_kernel(x_ref, g_ref, o_ref):
    x = x_ref[...]
    o_ref[...] = x * lax.rsqrt(jnp.mean(x * x, axis=-1, keepdims=True) + NORM_EPS) * g_ref[...]


def _final_norm(x, g, tm):
    m, d = x.shape
    return pl.pallas_call(
        _final_norm_kernel,
        grid=(m // tm,),
        in_specs=[pl.BlockSpec((tm, d), lambda i: (i, 0)), pl.BlockSpec((1, d), lambda i: (0, 0))],
        out_specs=pl.BlockSpec((tm, d), lambda i: (i, 0)),
        out_shape=jax.ShapeDtypeStruct((m, d), F32),
        compiler_params=_params(("parallel",)),
        name="final_norm",
    )(x, g)


def _s5_params(a_re, a_im, log_dt, b_re, b_im, c_re, c_im):
    dt = jnp.exp(log_dt)[:, None]
    mag = jnp.exp(a_re * dt)
    lb_re, lb_im = mag * jnp.cos(a_im * dt), mag * jnp.sin(a_im * dt)
    nr, ni = lb_re - 1.0, lb_im
    den = a_re * a_re + a_im * a_im
    f_re = (nr * a_re + ni * a_im) / den
    f_im = (ni * a_re - nr * a_im) / den
    bb_re = f_re[..., None] * b_re - f_im[..., None] * b_im
    bb_im = f_re[..., None] * b_im + f_im[..., None] * b_re
    eye = jnp.eye(S5_G, dtype=F32)
    bd_in = lambda w: jnp.einsum('gpc,gh->gchp', w, eye).reshape(S5_W, S5_N)
    bd_out = lambda w: jnp.einsum('gcp,gh->gphc', w, eye).reshape(S5_N, S5_W)
    bmat = jnp.concatenate([bd_in(bb_re), bd_in(bb_im)], axis=1).astype(BF16)
    cmat = jnp.concatenate([bd_out(c_re), -bd_out(c_im)], axis=0).astype(BF16)
    l_re, l_im = lb_re.reshape(1, S5_N), lb_im.reshape(1, S5_N)
    pows = [(l_re, l_im)]
    for _ in range(SUBLANES - 1):
        p_re, p_im = pows[-1]
        pows.append((p_re * l_re - p_im * l_im, p_re * l_im + p_im * l_re))
    lam_t = jnp.concatenate([pows[0][0], pows[0][1], pows[1][0], pows[1][1], pows[3][0], pows[3][1],
                             jnp.zeros((2, S5_N), F32)], axis=0)
    pw_t = jnp.concatenate([jnp.concatenate([p[0] for p in pows], axis=0),
                            jnp.concatenate([p[1] for p in pows], axis=0)], axis=1)
    return bmat, cmat, lam_t, pw_t


def _pad_cols(w, width):
    return jnp.pad(w, ((0, 0), (0, width - w.shape[1])))


def kernel(x_prompt, x_sample, c_prompt, c_sample, state_wkv, state_shift, state_pool, state_s5_re, state_s5_im, norm1_g, norm2_g, final_norm_g, w_ada, b_ada, w_in, rw_mu, rw_w0, rw_w2, rw_a0, rw_a2, rw_g2, rw_kk, rw_ka, rw_rk, rw_lnx_g, rw_lnx_b, pool_w, pool_scale, s5_a_re, s5_a_im, s5_log_dt, s5_b_re, s5_b_im, s5_c_re, s5_c_im, s5_d, s5_glu_w, s5_glu_b, w_br, w_out, router_w, router_b, moe_w_gu, moe_b_gu, moe_w_down, moe_b_down):
    bp, lp, d = x_prompt.shape
    bs, ls, _ = x_sample.shape
    depth = w_in.shape[0]
    mp, ms = bp * lp, bs * ls
    m = mp + ms
    past_len = 16384
    assert bs == MOD_ROWS and bp * HEADS * 2 == LANES and lp % 1024 == 0 and ms == 1024

    def mod_idx_for(tm):
        n_p = mp // tm
        return lambda i: jnp.where(i < n_p, (i * tm) // lp, bp)

    tm_big, tm_merge, tm_mid = 1024, 512, 256
    ones_bd = jnp.pad(jnp.kron(jnp.eye(HEADS, dtype=F32), jnp.ones((1, HEAD), F32)),
                      ((0, LANES - HEADS), (0, 0))).astype(BF16)

    x = _stack_rows(x_prompt.reshape(mp, d), jnp.swapaxes(x_sample, 0, 1).reshape(ms, d))
    c_all = jnp.concatenate([c_prompt, c_sample, jnp.zeros((4, d), F32)], axis=0)
    mod = _adaln(c_all, w_ada, b_ada)

    mod6 = mod.reshape(depth, mod.shape[1], 6, d)
    tables = jnp.concatenate([jnp.repeat(mod6[:, :bp], MOD_ROWS, axis=1), mod6[:, bp:bp + bs]], axis=1)
    tables = tables.transpose(0, 2, 1, 3)
    w_main_all = jnp.concatenate([w_in[:, :, :RW_PROJ], jnp.zeros((depth, d, RW_PAD - RW_PROJ), F32),
                                  w_in[:, :, RW_PROJ:RW_PROJ + POOL_W + S5_W]], axis=2).astype(BF16)
    w_gate_all = w_in[:, :, RW_PROJ + POOL_W + S5_W:].astype(BF16)
    wl_all = jnp.zeros((depth, LORA_IN, 3 * RW), F32)
    wl_all = wl_all.at[:, :R_DECAY, :RW].set(rw_w2)
    wl_all = wl_all.at[:, R_DECAY:R_DECAY + R_AAA, RW:2 * RW].set(rw_a2)
    wl_all = wl_all.at[:, R_DECAY + R_AAA:R_DECAY + R_AAA + R_GATE, 2 * RW:].set(rw_g2).astype(BF16)
    s5_all = jax.vmap(_s5_params)(s5_a_re, s5_a_im, s5_log_dt, s5_b_re, s5_b_im, s5_c_re, s5_c_im)

    outs_p, outs_s = [], []
    for l in range(depth):
        shift1, scale1, gate1, shift2, scale2, gate2 = (tables[l, i] for i in range(6))

        n1 = norm1_g[l].reshape(1, d)
        z_main = _inproj(x, n1, scale1, shift1, w_main_all, l, mod_idx_for(tm_big), tm_big)
        z_gate = _inproj(x, n1, scale1, shift1, w_gate_all, l, mod_idx_for(tm_big), tm_big)

        mu = _pad_cols(rw_mu[l].reshape(1, RW_PROJ), RW_PAD)
        wl = wl_all[l]
        vecs = [v.reshape(1, RW) for v in (rw_w0[l], rw_a0[l], rw_kk[l], rw_ka[l], rw_rk[l])]
        lnx_g, lnx_b = rw_lnx_g[l].reshape(1, RW), rw_lnx_b[l].reshape(1, RW)
        *ops_p, g_p, bon_p = _rwkv_pre_prompt(z_main, bp, lp, mu, *vecs, wl, ones_bd)
        ops_p = [a.reshape(lp, a.shape[0] // lp, LANES) for a in ops_p]
        y_pt, st_p = _rwkv_scan_prompt(*ops_p[:3], *ops_p[4:], ops_p[3],
                                       jnp.zeros((HEAD // 2, HEAD, LANES), F32), 64)
        y_a_p = _rwkv_post_prompt(y_pt.reshape(lp * (HEAD // 2), LANES), bon_p, g_p, bp, lp,
                                  lnx_g, lnx_b, ones_bd)
        new_wkv_p = st_p.reshape(HEAD // 2, HEAD, 2, bp, HEADS).transpose(3, 4, 0, 2, 1)
        new_wkv_p = new_wkv_p.reshape(bp, HEADS, HEAD, HEAD)

        *ops_s, g_s, bon_s = _rwkv_pre_sample(z_main, _pad_cols(state_shift[l], RW_PAD), mp, bs, ls,
                                              mu, *vecs, wl, ones_bd)
        y_st, st_s = _rwkv_scan_sample(*ops_s[:3], *ops_s[4:], ops_s[3],
                                       state_wkv.reshape(depth * bs, HEADS * HEAD * HEAD), l)
        y_a_s = _rwkv_post_sample(y_st, bon_s, g_s, bs, ls, lnx_g, lnx_b, ones_bd)
        new_wkv_s = st_s.reshape(bs, HEADS, HEAD, HEAD)
        y_a = jnp.concatenate([y_a_p, y_a_s], axis=0)
        new_shift_p = z_main[lp - 1:mp:lp, :RW_PROJ]
        new_shift_s = z_main[m - bs:, :RW_PROJ]

        pw = pool_w[l].astype(BF16)
        ps = pool_scale[l].reshape(1, POOL_W)
        y_b_p = _pool_prompt(z_main, bp, lp, pw, ps)
        zb_s = z_main[mp:, RW_PAD:RW_PAD + POOL_W].reshape(ls, bs, POOL_W)
        full_s = jnp.concatenate([jnp.swapaxes(state_pool[l], 0, 1), zb_s], axis=0)
        y_b_s = _pool_sample(full_s, ls, past_len, pw, ps)
        y_b = jnp.concatenate([y_b_p, y_b_s], axis=0)
        new_pool_p = z_main[:mp, RW_PAD:RW_PAD + POOL_W].reshape(bp, lp, POOL_W)[:, lp - POOL_BUF:]
        new_pool_s = jnp.swapaxes(full_s[full_s.shape[0] - POOL_BUF:], 0, 1)

        bmat, cmat, lam_t, pw_t = (t[l] for t in s5_all)
        dskip, gw, gb = s5_d[l].reshape(1, S5_W), s5_glu_w[l].astype(BF16), s5_glu_b[l].reshape(1, S5_W)
        y_c_p, re_p, im_p = _s5_prompt(z_main, bp, lp, 1024, bmat, cmat, lam_t, pw_t, dskip, gw, gb)
        y_c_s, re_s, im_s = _s5_sample(z_main, mp, bs, ls, state_s5_re[l].reshape(bs, S5_N),
                                       state_s5_im[l].reshape(bs, S5_N), bmat, cmat, lam_t, dskip, gw, gb)
        y_c = jnp.concatenate([y_c_p, y_c_s], axis=0)

        rw_pad = _pad_cols(router_w[l], LANES)
        rb_pad = jnp.concatenate([router_b[l], jnp.full((LANES - N_EXP,), -1e30, F32)]).reshape(1, LANES)
        x, h2, logits = _merge(y_a, y_b, y_c, z_gate, w_br[l].astype(BF16), w_out[l].astype(BF16), x, gate1,
                               norm2_g[l].reshape(1, d), scale2, shift2, rw_pad, rb_pad,
                               mod_idx_for(tm_merge), tm_merge)

        n_rows_pad = (m * TOP_K // MOE_BM + N_EXP) * MOE_BM
        gate_w, dest, rows_tok, blk_e, n_rows_used = _route(logits[:, :N_EXP], n_rows_pad)
        yb = _moe(h2[rows_tok], blk_e, n_rows_used // MOE_BM, l, moe_w_gu, moe_b_gu, moe_w_down, moe_b_down)
        yg = yb[dest.T.reshape(-1)]
        x = _combine(x, yg, gate_w, gate2, mod_idx_for(tm_mid), tm_mid)

        outs_p.append((new_shift_p, new_wkv_p, new_pool_p, re_p.reshape(bp, S5_G, S5_P),
                       im_p.reshape(bp, S5_G, S5_P)))
        outs_s.append((new_shift_s, new_wkv_s, new_pool_s, re_s.reshape(bs, S5_G, S5_P),
                       im_s.reshape(bs, S5_G, S5_P)))

    y = _final_norm(x, final_norm_g.reshape(1, d), tm_mid)
    y_prompt = y[:mp].reshape(bp, lp, d)
    y_sample = jnp.swapaxes(y[mp:].reshape(ls, bs, d), 0, 1)
    p_shift, p_wkv, p_pool, p_re, p_im = (jnp.stack([o[j] for o in outs_p]) for j in range(5))
    s_shift, s_wkv, s_pool, s_re, s_im = (jnp.stack([o[j] for o in outs_s]) for j in range(5))
    return (y_prompt, y_sample, p_wkv, p_shift, p_pool, p_re, p_im,
            s_wkv, s_shift, s_pool, s_re, s_im)
```

```python
import functools

import jax
import jax.numpy as jnp
from jax import lax
from jax.experimental import pallas as pl
from jax.experimental.pallas import tpu as pltpu

F32, BF16 = jnp.float32, jnp.bfloat16

LANES = 128
SUBLANES = 8
VMEM_LIMIT = 56 * 1024 * 1024

HEAD = 64
HEADS = 16
RW = HEAD * HEADS
R_DECAY, R_AAA, R_GATE = 64, 64, 160
RW_PROJ = 3 * RW + R_DECAY + R_AAA + R_GATE
RW_PAD = 3584
LORA_IN = RW_PAD - 3 * RW
POOL_W = 512
WINDOWS = (2, 4, 8, 16)
POOL_G = POOL_W // len(WINDOWS)
POOL_BUF = max(WINDOWS) - 1
S5_W = 512
S5_G, S5_CH, S5_P = 32, 16, 64
S5_N = S5_G * S5_P
MAIN_W = RW_PAD + POOL_W + S5_W
N_EXP, TOP_K = 32, 4
SW_LIMIT, SW_ALPHA = 7.0, 1.702
NORM_EPS, GN_EPS = 1e-5, 64e-5
MOD_ROWS = 128


def _params(sem):
    return pltpu.CompilerParams(dimension_semantics=sem, vmem_limit_bytes=VMEM_LIMIT)


def _dot(a, b):
    return jnp.dot(a, b, preferred_element_type=F32)


def _dot3(a, b):
    a_hi, b_hi = a.astype(BF16), b.astype(BF16)
    a_lo = (a - a_hi.astype(F32)).astype(BF16)
    b_lo = (b - b_hi.astype(F32)).astype(BF16)
    return _dot(a_hi, b_hi) + (_dot(a_hi, b_lo) + _dot(a_lo, b_hi))


def _segsum(x, ones_bd):
    def split(v):
        hi = v.astype(BF16)
        return hi, (v - hi.astype(F32)).astype(BF16)

    nt = (((1,), (1,)), ((), ()))
    hi, lo = split(x)
    sums = (lax.dot_general(hi, ones_bd, nt, preferred_element_type=F32)
            + lax.dot_general(lo, ones_bd, nt, preferred_element_type=F32))
    hi, lo = split(sums)
    return _dot(hi, ones_bd) + _dot(lo, ones_bd)


def _sublane_transpose8(xs):
    xs = list(xs)
    sub = lax.broadcasted_iota(jnp.int32, xs[0].shape, 0)
    for d in (4, 2, 1):
        keep = (sub & d) == 0
        for k in range(SUBLANES):
            if k & d:
                continue
            lo, hi = xs[k], xs[k + d]
            xs[k] = jnp.where(keep, lo, pltpu.roll(hi, d, axis=0))
            xs[k + d] = jnp.where(keep, pltpu.roll(lo, SUBLANES - d, axis=0), hi)
    return xs


def _tile_rows(t, rows):
    return jnp.broadcast_to(t[None], (rows // MOD_ROWS,) + t.shape).reshape(rows, t.shape[-1])


def _stack_rows_kernel(a_ref, b_ref, o_ref):
    i = pl.program_id(0)

    @pl.when(i < pl.num_programs(0) - 1)
    def _():
        o_ref[...] = a_ref[...]

    @pl.when(i == pl.num_programs(0) - 1)
    def _():
        o_ref[...] = b_ref[...]


def _stack_rows(a, b):
    tm, d = b.shape
    n_a = a.shape[0] // tm
    return pl.pallas_call(
        _stack_rows_kernel,
        grid=(n_a + 1,),
        in_specs=[pl.BlockSpec((tm, d), lambda i: (jnp.minimum(i, n_a - 1), 0)),
                  pl.BlockSpec((tm, d), lambda i: (0, 0))],
        out_specs=pl.BlockSpec((tm, d), lambda i: (i, 0)),
        out_shape=jax.ShapeDtypeStruct((a.shape[0] + tm, d), a.dtype),
        compiler_params=_params(("arbitrary",)),
        name="stack_rows",
    )(a, b)


def _ada_kernel(c_ref, w_ref, b_ref, o_ref):
    c = c_ref[...]
    a = (c * jax.nn.sigmoid(c)).astype(BF16)
    o_ref[0] = _dot(a, w_ref[0].astype(BF16)) + b_ref[0]


def _adaln(c_all, w_ada, b_ada):
    nl, d, n = w_ada.shape
    r = c_all.shape[0]
    tn = 1024
    return pl.pallas_call(
        _ada_kernel,
        grid=(nl, n // tn),
        in_specs=[pl.BlockSpec((r, d), lambda l, j: (0, 0)),
                  pl.BlockSpec((1, d, tn), lambda l, j: (l, 0, j)),
                  pl.BlockSpec((1, 1, tn), lambda l, j: (l, 0, j))],
        out_specs=pl.BlockSpec((1, r, tn), lambda l, j: (l, 0, j)),
        out_shape=jax.ShapeDtypeStruct((nl, r, n), F32),
        compiler_params=_params(("parallel", "parallel")),
        name="adaln",
    )(c_all, w_ada, b_ada.reshape(nl, 1, n))


def _norm_mod(x, g, scale, shift):
    y = x * lax.rsqrt(jnp.mean(x * x, axis=-1, keepdims=True) + NORM_EPS) * g
    rows = x.shape[0]
    return y * (1.0 + _tile_rows(scale, rows)) + _tile_rows(shift, rows)


def _inproj_kernel(x_ref, g_ref, sc_ref, sh_ref, w_ref, o_ref, h_scr):
    @pl.when(pl.program_id(1) == 0)
    def _():
        h_scr[...] = _norm_mod(x_ref[...], g_ref[...], sc_ref[...], sh_ref[...]).astype(BF16)

    o_ref[...] = _dot(h_scr[...], w_ref[0])


def _inproj(x, g, scale_t, shift_t, w, layer, mod_idx, tm):
    m, d = x.shape
    n = w.shape[2]
    tn = 512
    return pl.pallas_call(
        _inproj_kernel,
        grid=(m // tm, n // tn),
        in_specs=[pl.BlockSpec((tm, d), lambda i, j: (i, 0)),
                  pl.BlockSpec((1, d), lambda i, j: (0, 0)),
                  pl.BlockSpec((MOD_ROWS, d), lambda i, j: (mod_idx(i), 0)),
                  pl.BlockSpec((MOD_ROWS, d), lambda i, j: (mod_idx(i), 0)),
                  pl.BlockSpec((1, d, tn), lambda i, j: (layer, 0, j))],
        out_specs=pl.BlockSpec((tm, tn), lambda i, j: (i, j)),
        out_shape=jax.ShapeDtypeStruct((m, n), F32),
        scratch_shapes=[pltpu.VMEM((tm, d), BF16)],
        compiler_params=_params(("parallel", "arbitrary")),
        name="inproj",
    )(x, g, scale_t, shift_t, w)


def _softplus(x):
    return jnp.maximum(x, 0.0) + jnp.log1p(jnp.exp(-jnp.abs(x)))


RW_TT = 128
HALF = LANES // 2


def _rwkv_pre_math(z, zp, mu_ref, w0_ref, a0_ref, kkw_ref, ka_ref, rk_ref, wl_ref, ones_ref):
    zs = z + (zp - z) * mu_ref[...]
    r, k, v = zs[:, :RW], zs[:, RW:2 * RW], zs[:, 2 * RW:3 * RW]
    l0 = zs[:, 3 * RW:3 * RW + LANES]
    lane = lax.broadcasted_iota(jnp.int32, l0.shape, 1)
    l0 = jnp.where(lane < R_DECAY, jnp.tanh(l0), l0)
    l1 = jax.nn.sigmoid(zs[:, 3 * RW + LANES:])
    lin = jnp.concatenate([l0, l1], axis=1).astype(BF16)
    lo = _dot(lin, wl_ref[...])
    w_log = -_softplus(-(w0_ref[...] + lo[:, :RW])) - 0.5
    a = jax.nn.sigmoid(a0_ref[...] + lo[:, RW:2 * RW])
    ones_bd = ones_ref[...]
    kk = k * kkw_ref[...]
    kkn = kk / jnp.maximum(jnp.sqrt(_segsum(kk * kk, ones_bd)), 1e-12)
    kh = k * (1.0 + (a - 1.0) * ka_ref[...])
    bonus = _segsum(r * kh * rk_ref[...], ones_bd) * v
    return r, jnp.exp(-jnp.exp(w_log)), kh, v, kkn, kkn * a, lo[:, 2 * RW:], bonus


def _rwkv_pre_prompt_kernel(z_ref, prev_ref, mu_ref, w0_ref, a0_ref, kkw_ref, ka_ref, rk_ref, wl_ref,
                            ones_ref, r_o, w_o, k_o, v_o, kk_o, b_o, g_o, bon_o, zt_scr):
    bi = pl.program_id(1)
    z = z_ref[...]
    row = lax.broadcasted_iota(jnp.int32, z.shape, 0)
    prev_row = jnp.where(pl.program_id(0) == 0, 0.0, prev_ref[SUBLANES - 1:SUBLANES, :])
    zp = jnp.where(row == 0, prev_row, pltpu.roll(z, 1, axis=0))
    r, w, kh, v, kkn, b, g, bonus = _rwkv_pre_math(z, zp, mu_ref, w0_ref, a0_ref, kkw_ref, ka_ref,
                                                   rk_ref, wl_ref, ones_ref)
    g_o[...] = g
    bon_o[...] = bonus
    for n, x in enumerate((r, w, kh, v, kkn, b)):
        for p in range(RW // LANES):
            row0 = pl.multiple_of((bi * (RW // LANES) + p) * LANES, LANES)
            zt_scr[n, pl.ds(row0, LANES), :] = x[:, p * LANES:(p + 1) * LANES].T

    @pl.when(bi == pl.num_programs(1) - 1)
    def _():
        n_bh = zt_scr.shape[1] // HEAD

        def channel_rows(n, c0):
            cols = [[] for _ in range(SUBLANES)]
            for j in range(n_bh // SUBLANES):
                blk = [zt_scr[n, pl.ds(pl.multiple_of((j * SUBLANES + k) * HEAD + c0, SUBLANES), SUBLANES), :]
                       for k in range(SUBLANES)]
                for m, piece in enumerate(_sublane_transpose8(blk)):
                    cols[m].append(piece)
            return [jnp.concatenate(pieces, axis=0) for pieces in cols]

        def store_rows(o_ref, mats, rows_per_t, r0):
            for jt in range(RW_TT // SUBLANES):
                blk = [mt[jt * SUBLANES:(jt + 1) * SUBLANES] for mt in mats]
                for s, piece in enumerate(_sublane_transpose8(blk)):
                    row = pl.multiple_of((jt * SUBLANES + s) * rows_per_t + r0, SUBLANES)
                    o_ref[pl.ds(row, SUBLANES), :] = piece

        for n, o_ref in enumerate((r_o, w_o, k_o, v_o, kk_o, b_o)):
            if o_ref is v_o:
                def group(gi, c, n=n, o_ref=o_ref):
                    c0 = pl.multiple_of(gi * 2 * SUBLANES, 2 * SUBLANES)
                    a = channel_rows(n, c0) + channel_rows(n, c0 + SUBLANES)
                    mats = [jnp.concatenate([a[2 * q], a[2 * q + 1]], axis=0).T for q in range(SUBLANES)]
                    store_rows(o_ref, mats, HEAD // 2, pl.multiple_of(gi * SUBLANES, SUBLANES))
                    return c

                lax.fori_loop(0, HEAD // (2 * SUBLANES), group, 0)
            else:
                def group(gi, c, n=n, o_ref=o_ref):
                    c0 = pl.multiple_of(gi * SUBLANES, SUBLANES)
                    mats = [jnp.concatenate([a, a], axis=0).T for a in channel_rows(n, c0)]
                    store_rows(o_ref, mats, HEAD, c0)
                    return c

                lax.fori_loop(0, HEAD // SUBLANES, group, 0)


def _rwkv_pre_prompt(z_main, n_b, n_t, mu, w0, a0, kkw, ka, rk, wl, ones_bd):
    nt = n_t // RW_TT
    vec = lambda w: pl.BlockSpec((1, w), lambda i, b: (0, 0))
    keyed = pl.BlockSpec((RW_TT * HEAD, LANES), lambda i, b: (i, 0), pipeline_mode=pl.Buffered(1))
    paired = pl.BlockSpec((RW_TT * (HEAD // 2), LANES), lambda i, b: (i, 0), pipeline_mode=pl.Buffered(1))
    keyed_shape = jax.ShapeDtypeStruct((n_t * HEAD, LANES), F32)
    paired_shape = jax.ShapeDtypeStruct((n_t * (HEAD // 2), LANES), F32)
    nat = pl.BlockSpec((RW_TT, RW), lambda i, b: (b * nt + i, 0))
    return pl.pallas_call(
        _rwkv_pre_prompt_kernel,
        grid=(nt, n_b),
        in_specs=[pl.BlockSpec((RW_TT, RW_PAD), lambda i, b: (b * nt + i, 0)),
                  pl.BlockSpec((SUBLANES, RW_PAD),
                               lambda i, b: (jnp.maximum((b * nt + i) * (RW_TT // SUBLANES) - 1, 0), 0)),
                  vec(RW_PAD), vec(RW), vec(RW), vec(RW), vec(RW), vec(RW),
                  pl.BlockSpec((LORA_IN, 3 * RW), lambda i, b: (0, 0)),
                  pl.BlockSpec((LANES, RW), lambda i, b: (0, 0))],
        out_specs=[keyed] * 3 + [paired] + [keyed] * 2 + [nat] * 2,
        out_shape=[keyed_shape] * 3 + [paired_shape] + [keyed_shape] * 2
        + [jax.ShapeDtypeStruct((n_b * n_t, RW), F32)] * 2,
        scratch_shapes=[pltpu.VMEM((6, n_b * RW, RW_TT), F32)],
        compiler_params=_params(("arbitrary", "arbitrary")),
        name="rwkv_pre_prompt",
    )(z_main, z_main, mu, w0, a0, kkw, ka, rk, wl, ones_bd)


def _rwkv_pre_sample_kernel(z_ref, zp_ref, st_ref, mu_ref, w0_ref, a0_ref, kkw_ref, ka_ref, rk_ref, wl_ref,
                            ones_ref, r_o, w_o, k_o, v_o, kk_o, b_o, g_o, bon_o):
    zp = jnp.where(pl.program_id(0) == 0, st_ref[...], zp_ref[...])
    r, w, kh, v, kkn, b, g, bonus = _rwkv_pre_math(z_ref[...], zp, mu_ref, w0_ref, a0_ref, kkw_ref, ka_ref,
                                                   rk_ref, wl_ref, ones_ref)
    g_o[...] = g
    bon_o[...] = bonus
    for x, o_ref in zip((r, w, kh, v, kkn, b), (r_o, w_o, k_o, v_o, kk_o, b_o)):
        for p in range(RW // LANES):
            xt = x[:, p * LANES:(p + 1) * LANES].T
            o_ref[0, :, 2 * p * LANES:(2 * p + 1) * LANES] = xt[:HEAD]
            o_ref[0, :, (2 * p + 1) * LANES:(2 * p + 2) * LANES] = xt[HEAD:]


def _rwkv_pre_sample(z_main, state, row0, n_b, n_t, mu, w0, a0, kkw, ka, rk, wl, ones_bd):
    blk0 = row0 // n_b
    vec = lambda w: pl.BlockSpec((1, w), lambda t: (0, 0))
    keyed = pl.BlockSpec((1, HEAD, HEADS * n_b), lambda t: (t, 0, 0))
    nat = pl.BlockSpec((n_b, RW), lambda t: (t, 0))
    return pl.pallas_call(
        _rwkv_pre_sample_kernel,
        grid=(n_t,),
        in_specs=[pl.BlockSpec((n_b, RW_PAD), lambda t: (blk0 + t, 0)),
                  pl.BlockSpec((n_b, RW_PAD), lambda t: (blk0 + jnp.maximum(t - 1, 0), 0)),
                  pl.BlockSpec((n_b, RW_PAD), lambda t: (0, 0)),
                  vec(RW_PAD), vec(RW), vec(RW), vec(RW), vec(RW), vec(RW),
                  pl.BlockSpec((LORA_IN, 3 * RW), lambda t: (0, 0)),
                  pl.BlockSpec((LANES, RW), lambda t: (0, 0))],
        out_specs=[keyed] * 6 + [nat] * 2,
        out_shape=[jax.ShapeDtypeStruct((n_t, HEAD, HEADS * n_b), F32)] * 6
        + [jax.ShapeDtypeStruct((n_t * n_b, RW), F32)] * 2,
        compiler_params=_params(("arbitrary",)),
        name="rwkv_pre_sample",
    )(z_main, z_main, state, mu, w0, a0, kkw, ka, rk, wl, ones_bd)


def _rwkv_scan_kernel(nat_state, r_ref, w_ref, k_ref, kk_ref, b_ref, v_ref, s0_ref, y_ref, sT_ref, s_scr):
    tc = pl.program_id(1)
    n_t = r_ref.shape[0]
    n_i = s_scr.shape[0]

    @pl.when(tc == 0)
    def _():
        if nat_state:
            s_scr[...] = s0_ref[...].T.reshape(s_scr.shape)
        else:
            s_scr[...] = s0_ref[...]

    def step(t, carry):
        r_t, w_t, k_t, kk_t, b_t = r_ref[t], w_ref[t], k_ref[t], kk_ref[t], b_ref[t]

        def rows8(ib, c2):
            i0 = pl.multiple_of(ib * SUBLANES, SUBLANES)
            v8 = v_ref[t, pl.ds(i0, SUBLANES), :]
            ys = []
            for ii in range(SUBLANES):
                s_old = s_scr[i0 + ii]
                sa = jnp.sum(s_old * kk_t, axis=0, keepdims=True)
                s_new = s_old * w_t - sa * b_t + v8[ii:ii + 1] * k_t
                ys.append(jnp.sum(s_new * r_t, axis=0, keepdims=True))
                s_scr[i0 + ii] = s_new
            y_ref[t, pl.ds(i0, SUBLANES), :] = jnp.concatenate(ys, axis=0)
            return c2

        return lax.fori_loop(0, n_i // SUBLANES, rows8, carry)

    lax.fori_loop(0, n_t, step, 0)

    @pl.when(tc == pl.num_programs(1) - 1)
    def _():
        if nat_state:
            sT_ref[...] = s_scr[...].reshape(n_i * s_scr.shape[1], LANES).T
        else:
            sT_ref[...] = s_scr[...]


def _rwkv_scan_prompt(r, w, k, kk, b, v, s0, tchunk):
    n_t = r.shape[0]
    n_i = s0.shape[0]
    op = pl.BlockSpec((tchunk, HEAD, LANES), lambda l, t: (t, 0, 0))
    vy = pl.BlockSpec((tchunk, n_i, LANES), lambda l, t: (t, 0, 0))
    st = pl.BlockSpec((n_i, HEAD, LANES), lambda l, t: (0, 0, 0))
    return pl.pallas_call(
        functools.partial(_rwkv_scan_kernel, False),
        grid=(1, n_t // tchunk),
        in_specs=[op] * 5 + [vy, st],
        out_specs=[vy, st],
        out_shape=[jax.ShapeDtypeStruct((n_t, n_i, LANES), F32),
                   jax.ShapeDtypeStruct((n_i, HEAD, LANES), F32)],
        scratch_shapes=[pltpu.VMEM((n_i, HEAD, LANES), F32)],
        compiler_params=_params(("arbitrary", "arbitrary")),
        name="rwkv_scan_prompt",
    )(r, w, k, kk, b, v, s0)


def _rwkv_scan_sample(r, w, k, kk, b, v, s0, layer):
    n_t, _, n_l = r.shape
    n_b = n_l // HEADS
    op = pl.BlockSpec((n_t, HEAD, LANES), lambda h, t: (0, 0, h))
    st = pl.BlockSpec((n_b, HEAD * HEAD), lambda h, t: (layer, h))
    st_out = pl.BlockSpec((n_b, HEAD * HEAD), lambda h, t: (0, h))
    return pl.pallas_call(
        functools.partial(_rwkv_scan_kernel, True),
        grid=(HEADS, 1),
        in_specs=[op] * 6 + [st],
        out_specs=[op, st_out],
        out_shape=[jax.ShapeDtypeStruct((n_t, HEAD, n_l), F32),
                   jax.ShapeDtypeStruct((n_b, HEADS * HEAD * HEAD), F32)],
        scratch_shapes=[pltpu.VMEM((HEAD, HEAD, LANES), F32)],
        compiler_params=_params(("parallel", "arbitrary")),
        name="rwkv_scan_sample",
    )(r, w, k, kk, b, v, s0)


def _rwkv_post_math(y, bon_ref, g_ref, lg_ref, lb_ref, ones_ref):
    ones_bd = ones_ref[...]
    yc = y - _segsum(y, ones_bd) * (1.0 / HEAD)
    var = _segsum(yc * yc, ones_bd) * (1.0 / HEAD)
    yn = yc * lax.rsqrt(var + GN_EPS) * lg_ref[...] + lb_ref[...]
    return ((yn + bon_ref[...]) * g_ref[...]).astype(BF16)


def _rwkv_post_prompt_kernel(y_ref, bon_ref, g_ref, lg_ref, lb_ref, ones_ref, o_ref, zt_scr):
    bi = pl.program_id(1)

    @pl.when(bi == 0)
    def _():
        n_bh = zt_scr.shape[0] // HEAD

        def group(gi, c):
            q0 = pl.multiple_of(gi * SUBLANES, SUBLANES)
            mats = [[] for _ in range(SUBLANES)]
            for jt in range(RW_TT // SUBLANES):
                blk = [y_ref[pl.ds(pl.multiple_of((jt * SUBLANES + s) * (HEAD // 2) + q0, SUBLANES), SUBLANES), :]
                       for s in range(SUBLANES)]
                for m, piece in enumerate(_sublane_transpose8(blk)):
                    mats[m].append(piece)
            vals = []
            for pieces in mats:
                mt = jnp.concatenate(pieces, axis=0).T
                vals += [mt[:HALF], mt[HALF:]]
            for half in range(2):
                for j in range(n_bh // SUBLANES):
                    blk = [v[j * SUBLANES:(j + 1) * SUBLANES] for v in vals[half * SUBLANES:(half + 1) * SUBLANES]]
                    for k, piece in enumerate(_sublane_transpose8(blk)):
                        row = pl.multiple_of((j * SUBLANES + k) * HEAD + 2 * q0 + half * SUBLANES, SUBLANES)
                        zt_scr[pl.ds(row, SUBLANES), :] = piece
            return c

        lax.fori_loop(0, HEAD // (2 * SUBLANES), group, 0)

    pieces = []
    for p in range(RW // LANES):
        row0 = pl.multiple_of((bi * (RW // LANES) + p) * LANES, LANES)
        pieces.append(zt_scr[pl.ds(row0, LANES), :].T)
    y = jnp.concatenate(pieces, axis=1)
    o_ref[...] = _rwkv_post_math(y, bon_ref, g_ref, lg_ref, lb_ref, ones_ref)


def _rwkv_post_prompt(y, bonus, g, n_b, n_t, lnx_g, lnx_b, ones_bd):
    nt = n_t // RW_TT
    nat = pl.BlockSpec((RW_TT, RW), lambda i, b: (b * nt + i, 0))
    vec = pl.BlockSpec((1, RW), lambda i, b: (0, 0))
    return pl.pallas_call(
        _rwkv_post_prompt_kernel,
        grid=(nt, n_b),
        in_specs=[pl.BlockSpec((RW_TT * (HEAD // 2), LANES), lambda i, b: (i, 0)), nat, nat, vec, vec,
                  pl.BlockSpec((LANES, RW), lambda i, b: (0, 0))],
        out_specs=nat,
        out_shape=jax.ShapeDtypeStruct((n_b * n_t, RW), BF16),
        scratch_shapes=[pltpu.VMEM((n_b * RW, RW_TT), F32)],
        compiler_params=_params(("arbitrary", "arbitrary")),
        name="rwkv_post_prompt",
    )(y, bonus, g, lnx_g, lnx_b, ones_bd)


def _rwkv_post_sample_kernel(y_ref, bon_ref, g_ref, lg_ref, lb_ref, ones_ref, o_ref):
    pieces = []
    for p in range(RW // LANES):
        m = jnp.concatenate([y_ref[0, :, 2 * p * LANES:(2 * p + 1) * LANES],
                             y_ref[0, :, (2 * p + 1) * LANES:(2 * p + 2) * LANES]], axis=0)
        pieces.append(m.T)
    y = jnp.concatenate(pieces, axis=1)
    o_ref[...] = _rwkv_post_math(y, bon_ref, g_ref, lg_ref, lb_ref, ones_ref)


def _rwkv_post_sample(y, bonus, g, n_b, n_t, lnx_g, lnx_b, ones_bd):
    nat = pl.BlockSpec((n_b, RW), lambda t: (t, 0))
    vec = pl.BlockSpec((1, RW), lambda t: (0, 0))
    return pl.pallas_call(
        _rwkv_post_sample_kernel,
        grid=(n_t,),
        in_specs=[pl.BlockSpec((1, HEAD, HEADS * n_b), lambda t: (t, 0, 0)), nat, nat, vec, vec,
                  pl.BlockSpec((LANES, RW), lambda t: (0, 0))],
        out_specs=nat,
        out_shape=jax.ShapeDtypeStruct((n_t * n_b, RW), BF16),
        compiler_params=_params(("parallel",)),
        name="rwkv_post_sample",
    )(y, bonus, g, lnx_g, lnx_b, ones_bd)


def _pool_prompt_kernel(start_pos, z_ref, pw_ref, ps_ref, y_ref):
    n = z_ref.shape[0]
    row = lax.broadcasted_iota(jnp.int32, (n, POOL_G), 0)
    for gi, win in enumerate(WINDOWS):
        sl = slice(gi * POOL_G, (gi + 1) * POOL_G)
        x = z_ref[:, sl]
        s = x
        k = 1
        while k < win:
            s = s + jnp.where(row >= k, pltpu.roll(s, k, axis=0), 0.0)
            k *= 2
        cnt = jnp.minimum(win, row + (start_pos + 1)).astype(F32)
        d = s / cnt - x
        y_ref[:, sl] = (_dot(d.astype(BF16), pw_ref[gi]) * ps_ref[:, sl]).astype(BF16)


def _pool_prompt(z_main, n_b, n_t, pw, ps):
    cb = RW_PAD // POOL_W
    return pl.pallas_call(
        functools.partial(_pool_prompt_kernel, 0),
        grid=(n_b,),
        in_specs=[pl.BlockSpec((n_t, POOL_W), lambda b: (b, cb)),
                  pl.BlockSpec((len(WINDOWS), POOL_G, POOL_G), lambda b: (0, 0, 0)),
                  pl.BlockSpec((1, POOL_W), lambda b: (0, 0))],
        out_specs=pl.BlockSpec((n_t, POOL_W), lambda b: (b, 0)),
        out_shape=jax.ShapeDtypeStruct((n_b * n_t, POOL_W), BF16),
        compiler_params=_params(("parallel",)),
        name="pool_prompt",
    )(z_main, pw, ps)


def _pool_sample_kernel(start_pos, n_t, full_ref, pw_ref, ps_ref, y_ref):
    n_b = full_ref.shape[1]
    for gi, win in enumerate(WINDOWS):
        sl = slice(gi * POOL_G, (gi + 1) * POOL_G)
        f = full_ref[:, :, sl]
        s = f
        k = 1
        while k < win:
            s = s[k:] + s[:-k]
            k *= 2
        s = s[s.shape[0] - n_t:]
        x = f[POOL_BUF:]
        pos = start_pos + lax.broadcasted_iota(jnp.int32, s.shape, 0)
        cnt = jnp.minimum(win, pos + 1).astype(F32)
        d = (s / cnt - x).reshape(n_t * n_b, POOL_G)
        y_ref[:, sl] = (_dot(d.astype(BF16), pw_ref[gi]) * ps_ref[:, sl]).astype(BF16)


def _pool_sample(full, n_t, start_pos, pw, ps):
    n_f, n_b, _ = full.shape
    return pl.pallas_call(
        functools.partial(_pool_sample_kernel, start_pos, n_t),
        grid=(1,),
        in_specs=[pl.BlockSpec((n_f, n_b, POOL_W), lambda i: (0, 0, 0)),
                  pl.BlockSpec((len(WINDOWS), POOL_G, POOL_G), lambda i: (0, 0, 0)),
                  pl.BlockSpec((1, POOL_W), lambda i: (0, 0))],
        out_specs=pl.BlockSpec((n_t * n_b, POOL_W), lambda i: (0, 0)),
        out_shape=jax.ShapeDtypeStruct((n_t * n_b, POOL_W), BF16),
        compiler_params=_params(("arbitrary",)),
        name="pool_sample",
    )(full, pw, ps)


def _s5_tail(x_all, u, cmat_ref, d_ref, gw_ref, gb_ref):
    y = _dot(x_all.astype(BF16), cmat_ref[...]) + d_ref[...] * u
    y = jax.nn.gelu(y)
    return (y * jax.nn.sigmoid(_dot(y.astype(BF16), gw_ref[...]) + gb_ref[...])).astype(BF16)


S5_LC = 512


def _s5_prompt_kernel(u_ref, bmat_ref, cmat_ref, lam_ref, pw_ref, d_ref, gw_ref, gb_ref,
                      y_ref, xre_o, xim_o, x_scr, c_scr):
    tc = pl.program_id(1)

    @pl.when(tc == 0)
    def _():
        c_scr[...] = jnp.zeros_like(c_scr)

    u = u_ref[...]
    n = u.shape[0]
    x_scr[...] = _dot(u.astype(BF16), bmat_ref[...])
    row = lax.broadcasted_iota(jnp.int32, (SUBLANES, S5_LC), 0)
    for lc in range(S5_N // S5_LC):
        re_sl = pl.ds(lc * S5_LC, S5_LC)
        im_sl = pl.ds(S5_N + lc * S5_LC, S5_LC)
        lam = [(lam_ref[2 * j:2 * j + 1, re_sl], lam_ref[2 * j + 1:2 * j + 2, re_sl]) for j in range(3)]
        p_re, p_im = pw_ref[:, re_sl], pw_ref[:, im_sl]

        def blk(rb, carry, re_sl=re_sl, im_sl=im_sl, lam=lam, p_re=p_re, p_im=p_im):
            c_re, c_im = carry
            rows = pl.ds(pl.multiple_of(rb * SUBLANES, SUBLANES), SUBLANES)
            xr, xi = x_scr[rows, re_sl], x_scr[rows, im_sl]
            for j, (l_re, l_im) in enumerate(lam):
                kshift = 1 << j
                sr = jnp.where(row >= kshift, pltpu.roll(xr, kshift, axis=0), 0.0)
                si = jnp.where(row >= kshift, pltpu.roll(xi, kshift, axis=0), 0.0)
                xr, xi = xr + (l_re * sr - l_im * si), xi + (l_re * si + l_im * sr)
            xr, xi = xr + (p_re * c_re - p_im * c_im), xi + (p_re * c_im + p_im * c_re)
            x_scr[rows, re_sl] = xr
            x_scr[rows, im_sl] = xi
            return xr[SUBLANES - 1:], xi[SUBLANES - 1:]

        c_re, c_im = lax.fori_loop(0, n // SUBLANES, blk, (c_scr[0:1, re_sl], c_scr[0:1, im_sl]))
        c_scr[0:1, re_sl] = c_re
        c_scr[0:1, im_sl] = c_im

    y_ref[...] = _s5_tail(x_scr[...], u, cmat_ref, d_ref, gw_ref, gb_ref)

    @pl.when(tc == pl.num_programs(1) - 1)
    def _():
        xre_o[0] = c_scr[0:1, :S5_N]
        xim_o[0] = c_scr[0:1, S5_N:]


def _s5_prompt(z_main, n_b, n_t, tt, bmat, cmat, lam_t, pw_t, dskip, gw, gb):
    cb = (RW_PAD + POOL_W) // S5_W
    n_tc = n_t // tt
    const = lambda shape: pl.BlockSpec(shape, lambda b, t: (0,) * len(shape))
    return pl.pallas_call(
        _s5_prompt_kernel,
        grid=(n_b, n_tc),
        in_specs=[pl.BlockSpec((tt, S5_W), lambda b, t: (b * n_tc + t, cb)),
                  const((S5_W, 2 * S5_N)), const((2 * S5_N, S5_W)),
                  const((SUBLANES, S5_N)), const((SUBLANES, 2 * S5_N)),
                  const((1, S5_W)), const((S5_W, S5_W)), const((1, S5_W))],
        out_specs=[pl.BlockSpec((tt, S5_W), lambda b, t: (b * n_tc + t, 0)),
                   pl.BlockSpec((1, 1, S5_N), lambda b, t: (b, 0, 0)),
                   pl.BlockSpec((1, 1, S5_N), lambda b, t: (b, 0, 0))],
        out_shape=[jax.ShapeDtypeStruct((n_b * n_t, S5_W), BF16),
                   jax.ShapeDtypeStruct((n_b, 1, S5_N), F32),
                   jax.ShapeDtypeStruct((n_b, 1, S5_N), F32)],
        scratch_shapes=[pltpu.VMEM((tt, 2 * S5_N), F32), pltpu.VMEM((SUBLANES, 2 * S5_N), F32)],
        compiler_params=_params(("parallel", "arbitrary")),
        name="s5_prompt",
    )(z_main, bmat, cmat, lam_t, pw_t, dskip, gw, gb)


def _s5_sample_kernel(n_t, u_ref, x0re_ref, x0im_ref, bmat_ref, cmat_ref, lam_ref, d_ref, gw_ref,
                      gb_ref, y_ref, xre_o, xim_o, x_scr):
    u = u_ref[...]
    n_b = u.shape[0] // n_t
    x_scr[...] = _dot(u.astype(BF16), bmat_ref[...])
    for lc in range(S5_N // S5_LC):
        re_sl = pl.ds(lc * S5_LC, S5_LC)
        im_sl = pl.ds(S5_N + lc * S5_LC, S5_LC)
        l_re, l_im = lam_ref[0:1, re_sl], lam_ref[1:2, re_sl]
        xr, xi = x0re_ref[:, re_sl], x0im_ref[:, re_sl]
        for t in range(n_t):
            rows = pl.ds(t * n_b, n_b)
            xr, xi = (l_re * xr - l_im * xi + x_scr[rows, re_sl],
                      l_re * xi + l_im * xr + x_scr[rows, im_sl])
            x_scr[rows, re_sl] = xr
            x_scr[rows, im_sl] = xi
        xre_o[:, re_sl] = xr
        xim_o[:, re_sl] = xi
    y_ref[...] = _s5_tail(x_scr[...], u, cmat_ref, d_ref, gw_ref, gb_ref)


def _s5_sample(z_main, row0, n_b, n_t, x0re, x0im, bmat, cmat, lam_t, dskip, gw, gb):
    rows = n_b * n_t
    cb = (RW_PAD + POOL_W) // S5_W
    const = lambda shape: pl.BlockSpec(shape, lambda i: (0,) * len(shape))
    return pl.pallas_call(
        functools.partial(_s5_sample_kernel, n_t),
        grid=(1,),
        in_specs=[pl.BlockSpec((rows, S5_W), lambda i: (row0 // rows, cb)),
                  const((n_b, S5_N)), const((n_b, S5_N)),
                  const((S5_W, 2 * S5_N)), const((2 * S5_N, S5_W)), const((SUBLANES, S5_N)),
                  const((1, S5_W)), const((S5_W, S5_W)), const((1, S5_W))],
        out_specs=[const((rows, S5_W)), const((n_b, S5_N)), const((n_b, S5_N))],
        out_shape=[jax.ShapeDtypeStruct((rows, S5_W), BF16),
                   jax.ShapeDtypeStruct((n_b, S5_N), F32),
                   jax.ShapeDtypeStruct((n_b, S5_N), F32)],
        scratch_shapes=[pltpu.VMEM((rows, 2 * S5_N), F32)],
        compiler_params=_params(("arbitrary",)),
        name="s5_sample",
    )(z_main, x0re, x0im, bmat, cmat, lam_t, dskip, gw, gb)


def _branch_merge_kernel(n_p, yap_ref, yas_ref, ybp_ref, ybs_ref, ycp_ref, ycs_ref, zg_ref, wbr_ref, o_ref):
    d = o_ref.shape[1]
    is_p = pl.program_id(0) < n_p
    ya = jnp.where(is_p, yap_ref[...], yas_ref[...])
    yb = jnp.where(is_p, ybp_ref[...], ybs_ref[...])
    yc = jnp.where(is_p, ycp_ref[...], ycs_ref[...])
    m = jax.nn.sigmoid(zg_ref[:, :d]) * _dot(ya, wbr_ref[:RW, :])
    m = m + jax.nn.sigmoid(zg_ref[:, d:2 * d]) * _dot(yb, wbr_ref[RW:RW + POOL_W, :])
    m = m + jax.nn.sigmoid(zg_ref[:, 2 * d:]) * _dot(yc, wbr_ref[RW + POOL_W:, :])
    o_ref[...] = m.astype(BF16)


def _outproj_kernel(m_ref, wout_ref, x_ref, g1_ref, n2_ref, sc2_ref, sh2_ref, rw_ref, rb_ref,
                    xo_ref, h2_ref, lg_ref):
    rows = x_ref.shape[0]
    xn = x_ref[...] + _tile_rows(g1_ref[...], rows) * _dot(m_ref[...], wout_ref[...])
    xo_ref[...] = xn
    h2 = _norm_mod(xn, n2_ref[...], sc2_ref[...], sh2_ref[...])
    h2_ref[...] = h2
    lg_ref[...] = _dot3(h2, rw_ref[...]) + rb_ref[...]


def _merge(ya, yb, yc, zg, wbr, wout, x, g1_t, n2g, sc2_t, sh2_t, rw, rb, mod_idx, tm):
    m, d = x.shape
    row = lambda w: pl.BlockSpec((tm, w), lambda i: (i, 0))
    const = lambda shape: pl.BlockSpec(shape, lambda i: (0,) * len(shape), pipeline_mode=pl.Buffered(1))
    mod = pl.BlockSpec((MOD_ROWS, d), lambda i: (mod_idx(i), 0))
    n_p = ya[0].shape[0] // tm
    prow = lambda w: pl.BlockSpec((tm, w), lambda i: (jnp.minimum(i, n_p - 1), 0))
    srow = lambda w: pl.BlockSpec((tm, w), lambda i: (jnp.maximum(i - n_p, 0), 0))
    merged = pl.pallas_call(
        functools.partial(_branch_merge_kernel, n_p),
        grid=(m // tm,),
        in_specs=[prow(RW), srow(RW), prow(POOL_W), srow(POOL_W), prow(S5_W), srow(S5_W), row(3 * d),
                  const(wbr.shape)],
        out_specs=row(d),
        out_shape=jax.ShapeDtypeStruct((m, d), BF16),
        compiler_params=_params(("parallel",)),
        name="branch_merge",
    )(*ya, *yb, *yc, zg, wbr)
    return pl.pallas_call(
        _outproj_kernel,
        grid=(m // tm,),
        in_specs=[row(d), const(wout.shape), row(d), mod, const((1, d)), mod, mod, const(rw.shape),
                  const((1, LANES))],
        out_specs=[row(d), row(d), row(LANES)],
        out_shape=[jax.ShapeDtypeStruct((m, d), F32), jax.ShapeDtypeStruct((m, d), F32),
                   jax.ShapeDtypeStruct((m, LANES), F32)],
        compiler_params=_params(("parallel",)),
        name="outproj",
    )(merged, wout, x, g1_t, n2g, sc2_t, sh2_t, rw, rb)


MOE_BM = 512
MOE_TF = 512


def _moe_kernel(be_ref, nb_ref, x_ref, wg_ref, wu_ref, bg_ref, bu_ref, wd_ref, bd_ref, o_ref):
    s, f = pl.program_id(0), pl.program_id(1)
    used = s < nb_ref[0]

    @pl.when(jnp.logical_and(jnp.logical_not(used), f == 0))
    def _():
        o_ref[...] = jnp.zeros_like(o_ref)

    @pl.when(used)
    def _():
        x = x_ref[...].astype(BF16)
        g = _dot(x, wg_ref[0].astype(BF16)) + bg_ref[0]
        u = _dot(x, wu_ref[0].astype(BF16)) + bu_ref[0]
        g = jnp.minimum(g, SW_LIMIT)
        u = jnp.clip(u, -SW_LIMIT, SW_LIMIT)
        act = (u + 1.0) * (g * jax.nn.sigmoid(SW_ALPHA * g))
        part = _dot(act.astype(BF16), wd_ref[0].astype(BF16))

        @pl.when(f == 0)
        def _():
            o_ref[...] = part + bd_ref[0]

        @pl.when(f > 0)
        def _():
            o_ref[...] += part


def _moe(xs, blk_e, n_used, layer, w_gu, b_gu, w_down, b_down):
    rows, d = xs.shape
    n_l, n_e, _, two_ff = w_gu.shape
    dff = two_ff // 2
    nf = dff // MOE_TF
    n_blk = rows // MOE_BM
    e0 = layer * n_e

    def f_eff(s, f, nb):
        return jnp.where(s < nb[0], f, nf - 1)

    grid_spec = pltpu.PrefetchScalarGridSpec(
        num_scalar_prefetch=2,
        grid=(n_blk, nf),
        in_specs=[
            pl.BlockSpec((MOE_BM, d), lambda s, f, be, nb: (jnp.minimum(s, nb[0] - 1), 0)),
            pl.BlockSpec((1, d, MOE_TF), lambda s, f, be, nb: (e0 + be[s], 0, f_eff(s, f, nb))),
            pl.BlockSpec((1, d, MOE_TF), lambda s, f, be, nb: (e0 + be[s], 0, nf + f_eff(s, f, nb))),
            pl.BlockSpec((1, 1, MOE_TF), lambda s, f, be, nb: (e0 + be[s], 0, f_eff(s, f, nb))),
            pl.BlockSpec((1, 1, MOE_TF), lambda s, f, be, nb: (e0 + be[s], 0, nf + f_eff(s, f, nb))),
            pl.BlockSpec((1, MOE_TF, d), lambda s, f, be, nb: (e0 + be[s], f_eff(s, f, nb), 0)),
            pl.BlockSpec((1, 1, d), lambda s, f, be, nb: (e0 + be[s], 0, 0)),
        ],
        out_specs=pl.BlockSpec((MOE_BM, d), lambda s, f, be, nb: (s, 0)),
    )
    w_gu = w_gu.reshape(n_l * n_e, d, two_ff)
    b_gu = b_gu.reshape(n_l * n_e, 1, two_ff)
    return pl.pallas_call(
        _moe_kernel,
        grid_spec=grid_spec,
        out_shape=jax.ShapeDtypeStruct((rows, d), F32),
        compiler_params=_params(("arbitrary", "arbitrary")),
        name="moe",
    )(blk_e, n_used, xs, w_gu, w_gu, b_gu, b_gu, w_down.reshape(n_l * n_e, dff, d),
      b_down.reshape(n_l * n_e, 1, d))


def _route(logits, n_rows_pad):
    n_tok = logits.shape[0]
    top_v, top_e = lax.top_k(logits, TOP_K)
    gate = jax.nn.softmax(top_v, axis=-1)
    flat_e = top_e.reshape(-1)
    onehot = (flat_e[:, None] == jnp.arange(N_EXP, dtype=jnp.int32)[None, :]).astype(jnp.int32)
    csum = jnp.cumsum(onehot, axis=0)
    counts = csum[-1]
    rank = jnp.sum((csum - 1) * onehot, axis=1)
    padded = (counts + MOE_BM - 1) // MOE_BM * MOE_BM
    pad_end = jnp.cumsum(padded)
    pad_start = pad_end - padded
    dest = (pad_start[flat_e] + rank).astype(jnp.int32)
    flat_tok = jnp.arange(n_tok * TOP_K, dtype=jnp.int32) // TOP_K
    rows_tok = jnp.zeros((n_rows_pad,), jnp.int32).at[dest].set(flat_tok)
    n_rows_used = pad_end[-1].astype(jnp.int32)
    starts = jnp.arange(n_rows_pad // MOE_BM, dtype=jnp.int32) * MOE_BM
    starts = jnp.minimum(starts, n_rows_used - MOE_BM)
    blk_e = jnp.minimum(jnp.searchsorted(pad_end, starts, side='right'), N_EXP - 1).astype(jnp.int32)
    return gate, dest.reshape(n_tok, TOP_K), rows_tok, blk_e, n_rows_used.reshape(1)


def _combine_kernel(x_ref, y0_ref, y1_ref, y2_ref, y3_ref, gw_ref, g2_ref, o_ref):
    rows = x_ref.shape[0]
    gw = gw_ref[...]
    acc = gw[:, 0:1] * y0_ref[...]
    for k, y_ref in enumerate((y1_ref, y2_ref, y3_ref), start=1):
        acc = acc + gw[:, k:k + 1] * y_ref[...]
    o_ref[...] = x_ref[...] + _tile_rows(g2_ref[...], rows) * acc


def _combine(x, yg, gw, g2_t, mod_idx, tm):
    m, d = x.shape
    n_i = m // tm
    slab = lambda k: pl.BlockSpec((tm, d), lambda i: (k * n_i + i, 0))
    return pl.pallas_call(
        _combine_kernel,
        grid=(n_i,),
        in_specs=[pl.BlockSpec((tm, d), lambda i: (i, 0)), slab(0), slab(1), slab(2), slab(3),
                  pl.BlockSpec((tm, TOP_K), lambda i: (i, 0)),
                  pl.BlockSpec((MOD_ROWS, d), lambda i: (mod_idx(i), 0))],
        out_specs=pl.BlockSpec((tm, d), lambda i: (i, 0)),
        out_shape=jax.ShapeDtypeStruct((m, d), F32),
        compiler_params=_params(("parallel",)),
        name="combine",
    )(x, yg, yg, yg, yg, gw, g2_t)


def _final_norm_kernel(x_ref, g_ref, o_ref):
    x = x_ref[...]
    o_ref[...] = x * lax.rsqrt(jnp.mean(x * x, axis=-1, keepdims=True) + NORM_EPS) * g_ref[...]


def _final_norm(x, g, tm):
    m, d = x.shape
    return pl.pallas_call(
        _final_norm_kernel,
        grid=(m // tm,),
        in_specs=[pl.BlockSpec((tm, d), lambda i: (i, 0)), pl.BlockSpec((1, d), lambda i: (0, 0))],
        out_specs=pl.BlockSpec((tm, d), lambda i: (i, 0)),
        out_shape=jax.ShapeDtypeStruct((m, d), F32),
        compiler_params=_params(("parallel",)),
        name="final_norm",
    )(x, g)


def _s5_params(a_re, a_im, log_dt, b_re, b_im, c_re, c_im):
    dt = jnp.exp(log_dt)[:, None]
    mag = jnp.exp(a_re * dt)
    lb_re, lb_im = mag * jnp.cos(a_im * dt), mag * jnp.sin(a_im * dt)
    nr, ni = lb_re - 1.0, lb_im
    den = a_re * a_re + a_im * a_im
    f_re = (nr * a_re + ni * a_im) / den
    f_im = (ni * a_re - nr * a_im) / den
    bb_re = f_re[..., None] * b_re - f_im[..., None] * b_im
    bb_im = f_re[..., None] * b_im + f_im[..., None] * b_re
    eye = jnp.eye(S5_G, dtype=F32)
    bd_in = lambda w: jnp.einsum('gpc,gh->gchp', w, eye).reshape(S5_W, S5_N)
    bd_out = lambda w: jnp.einsum('gcp,gh->gphc', w, eye).reshape(S5_N, S5_W)
    bmat = jnp.concatenate([bd_in(bb_re), bd_in(bb_im)], axis=1).astype(BF16)
    cmat = jnp.concatenate([bd_out(c_re), -bd_out(c_im)], axis=0).astype(BF16)
    l_re, l_im = lb_re.reshape(1, S5_N), lb_im.reshape(1, S5_N)
    pows = [(l_re, l_im)]
    for _ in range(SUBLANES - 1):
        p_re, p_im = pows[-1]
        pows.append((p_re * l_re - p_im * l_im, p_re * l_im + p_im * l_re))
    lam_t = jnp.concatenate([pows[0][0], pows[0][1], pows[1][0], pows[1][1], pows[3][0], pows[3][1],
                             jnp.zeros((2, S5_N), F32)], axis=0)
    pw_t = jnp.concatenate([jnp.concatenate([p[0] for p in pows], axis=0),
                            jnp.concatenate([p[1] for p in pows], axis=0)], axis=1)
    return bmat, cmat, lam_t, pw_t


def _pad_cols(w, width):
    return jnp.pad(w, ((0, 0), (0, width - w.shape[1])))


def kernel(x_prompt, x_sample, c_prompt, c_sample, state_wkv, state_shift, state_pool, state_s5_re, state_s5_im, norm1_g, norm2_g, final_norm_g, w_ada, b_ada, w_in, rw_mu, rw_w0, rw_w2, rw_a0, rw_a2, rw_g2, rw_kk, rw_ka, rw_rk, rw_lnx_g, rw_lnx_b, pool_w, pool_scale, s5_a_re, s5_a_im, s5_log_dt, s5_b_re, s5_b_im, s5_c_re, s5_c_im, s5_d, s5_glu_w, s5_glu_b, w_br, w_out, router_w, router_b, moe_w_gu, moe_b_gu, moe_w_down, moe_b_down):
    bp, lp, d = x_prompt.shape
    bs, ls, _ = x_sample.shape
    depth = w_in.shape[0]
    mp, ms = bp * lp, bs * ls
    m = mp + ms
    past_len = 16384
    assert bs == MOD_ROWS and bp * HEADS * 2 == LANES and lp % 1024 == 0 and ms == 1024

    def mod_idx_for(tm):
        n_p = mp // tm
        return lambda i: jnp.where(i < n_p, (i * tm) // lp, bp)

    tm_big, tm_merge, tm_mid = 1024, 512, 256
    ones_bd = jnp.pad(jnp.kron(jnp.eye(HEADS, dtype=F32), jnp.ones((1, HEAD), F32)),
                      ((0, LANES - HEADS), (0, 0))).astype(BF16)

    x = _stack_rows(x_prompt.reshape(mp, d), jnp.swapaxes(x_sample, 0, 1).reshape(ms, d))
    c_all = jnp.concatenate([c_prompt, c_sample, jnp.zeros((4, d), F32)], axis=0)
    mod = _adaln(c_all, w_ada, b_ada)

    mod6 = mod.reshape(depth, mod.shape[1], 6, d)
    tables = jnp.concatenate([jnp.repeat(mod6[:, :bp], MOD_ROWS, axis=1), mod6[:, bp:bp + bs]], axis=1)
    tables = tables.transpose(0, 2, 1, 3)
    w_main_all = jnp.concatenate([w_in[:, :, :RW_PROJ], jnp.zeros((depth, d, RW_PAD - RW_PROJ), F32),
                                  w_in[:, :, RW_PROJ:RW_PROJ + POOL_W + S5_W]], axis=2).astype(BF16)
    w_gate_all = w_in[:, :, RW_PROJ + POOL_W + S5_W:].astype(BF16)
    wl_all = jnp.zeros((depth, LORA_IN, 3 * RW), F32)
    wl_all = wl_all.at[:, :R_DECAY, :RW].set(rw_w2)
    wl_all = wl_all.at[:, R_DECAY:R_DECAY + R_AAA, RW:2 * RW].set(rw_a2)
    wl_all = wl_all.at[:, R_DECAY + R_AAA:R_DECAY + R_AAA + R_GATE, 2 * RW:].set(rw_g2).astype(BF16)
    s5_all = jax.vmap(_s5_params)(s5_a_re, s5_a_im, s5_log_dt, s5_b_re, s5_b_im, s5_c_re, s5_c_im)

    outs_p, outs_s = [], []
    for l in range(depth):
        shift1, scale1, gate1, shift2, scale2, gate2 = (tables[l, i] for i in range(6))

        n1 = norm1_g[l].reshape(1, d)
        z_main = _inproj(x, n1, scale1, shift1, w_main_all, l, mod_idx_for(tm_big), tm_big)
        z_gate = _inproj(x, n1, scale1, shift1, w_gate_all, l, mod_idx_for(tm_big), tm_big)

        mu = _pad_cols(rw_mu[l].reshape(1, RW_PROJ), RW_PAD)
        wl = wl_all[l]
        vecs = [v.reshape(1, RW) for v in (rw_w0[l], rw_a0[l], rw_kk[l], rw_ka[l], rw_rk[l])]
        lnx_g, lnx_b = rw_lnx_g[l].reshape(1, RW), rw_lnx_b[l].reshape(1, RW)
        *ops_p, g_p, bon_p = _rwkv_pre_prompt(z_main, bp, lp, mu, *vecs, wl, ones_bd)
        ops_p = [a.reshape(lp, a.shape[0] // lp, LANES) for a in ops_p]
        y_pt, st_p = _rwkv_scan_prompt(*ops_p[:3], *ops_p[4:], ops_p[3],
                                       jnp.zeros((HEAD // 2, HEAD, LANES), F32), 64)
        y_a_p = _rwkv_post_prompt(y_pt.reshape(lp * (HEAD // 2), LANES), bon_p, g_p, bp, lp,
                                  lnx_g, lnx_b, ones_bd)
        new_wkv_p = st_p.reshape(HEAD // 2, HEAD, 2, bp, HEADS).transpose(3, 4, 0, 2, 1)
        new_wkv_p = new_wkv_p.reshape(bp, HEADS, HEAD, HEAD)

        *ops_s, g_s, bon_s = _rwkv_pre_sample(z_main, _pad_cols(state_shift[l], RW_PAD), mp, bs, ls,
                                              mu, *vecs, wl, ones_bd)
        y_st, st_s = _rwkv_scan_sample(*ops_s[:3], *ops_s[4:], ops_s[3],
                                       state_wkv.reshape(depth * bs, HEADS * HEAD * HEAD), l)
        y_a_s = _rwkv_post_sample(y_st, bon_s, g_s, bs, ls, lnx_g, lnx_b, ones_bd)
        new_wkv_s = st_s.reshape(bs, HEADS, HEAD, HEAD)
        y_a = (y_a_p, y_a_s)
        new_shift_p = z_main[lp - 1:mp:lp, :RW_PROJ]
        new_shift_s = z_main[m - bs:, :RW_PROJ]

        pw = pool_w[l].astype(BF16)
        ps = pool_scale[l].reshape(1, POOL_W)
        y_b_p = _pool_prompt(z_main, bp, lp, pw, ps)
        zb_s = z_main[mp:, RW_PAD:RW_PAD + POOL_W].reshape(ls, bs, POOL_W)
        full_s = jnp.concatenate([jnp.swapaxes(state_pool[l], 0, 1), zb_s], axis=0)
        y_b_s = _pool_sample(full_s, ls, past_len, pw, ps)
        y_b = (y_b_p, y_b_s)
        new_pool_p = z_main[:mp, RW_PAD:RW_PAD + POOL_W].reshape(bp, lp, POOL_W)[:, lp - POOL_BUF:]
        new_pool_s = jnp.swapaxes(full_s[full_s.shape[0] - POOL_BUF:], 0, 1)

        bmat, cmat, lam_t, pw_t = (t[l] for t in s5_all)
        dskip, gw, gb = s5_d[l].reshape(1, S5_W), s5_glu_w[l].astype(BF16), s5_glu_b[l].reshape(1, S5_W)
        y_c_p, re_p, im_p = _s5_prompt(z_main, bp, lp, 1024, bmat, cmat, lam_t, pw_t, dskip, gw, gb)
        y_c_s, re_s, im_s = _s5_sample(z_main, mp, bs, ls, state_s5_re[l].reshape(bs, S5_N),
                                       state_s5_im[l].reshape(bs, S5_N), bmat, cmat, lam_t, dskip, gw, gb)
        y_c = (y_c_p, y_c_s)

        rw_pad = _pad_cols(router_w[l], LANES)
        rb_pad = jnp.concatenate([router_b[l], jnp.full((LANES - N_EXP,), -1e30, F32)]).reshape(1, LANES)
        x, h2, logits = _merge(y_a, y_b, y_c, z_gate, w_br[l].astype(BF16), w_out[l].astype(BF16), x, gate1,
                               norm2_g[l].reshape(1, d), scale2, shift2, rw_pad, rb_pad,
                               mod_idx_for(tm_merge), tm_merge)

        n_rows_pad = (m * TOP_K // MOE_BM + N_EXP) * MOE_BM
        gate_w, dest, rows_tok, blk_e, n_rows_used = _route(logits[:, :N_EXP], n_rows_pad)
        yb = _moe(h2[rows_tok], blk_e, n_rows_used // MOE_BM, l, moe_w_gu, moe_b_gu, moe_w_down, moe_b_down)
        yg = yb[dest.T.reshape(-1)]
        x = _combine(x, yg, gate_w, gate2, mod_idx_for(tm_mid), tm_mid)

        outs_p.append((new_shift_p, new_wkv_p, new_pool_p, re_p.reshape(bp, S5_G, S5_P),
                       im_p.reshape(bp, S5_G, S5_P)))
        outs_s.append((new_shift_s, new_wkv_s, new_pool_s, re_s.reshape(bs, S5_G, S5_P),
                       im_s.reshape(bs, S5_G, S5_P)))

    y = _final_norm(x, final_norm_g.reshape(1, d), tm_mid)
    y_prompt = y[:mp].reshape(bp, lp, d)
    y_sample = jnp.swapaxes(y[mp:].reshape(ls, bs, d), 0, 1)
    p_shift, p_wkv, p_pool, p_re, p_im = (jnp.stack([o[j] for o in outs_p]) for j in range(5))
    s_shift, s_wkv, s_pool, s_re, s_im = (jnp.stack([o[j] for o in outs_s]) for j in range(5))
    return (y_prompt, y_sample, p_wkv, p_shift, p_pool, p_re, p_im,
            s_wkv, s_shift, s_pool, s_re, s_im)
```
